```python
import jax, jax.numpy as jnp
from jax import lax
import numpy as np

D_MODEL = 4096
BATCH = 32
SEQ = 256
DEPTH = 2
DEC_BATCH = 8
DEC_SEQ = 2048
PAST_LEN = 512

GRID_W = 64
A_WIDTH = D_MODEL * 3 // 8
A_HEAD = 64
A_HEADS = A_WIDTH // A_HEAD
DECAY_R = 64
AAA_R = 64
GATE_R = 224
HEAD_DIM = 128
B_WIDTH = D_MODEL * 3 // 8
B_HEADS = B_WIDTH // HEAD_DIM
B_KV = B_HEADS // 3
B_GROUP = B_HEADS // B_KV
B_KV_WIDTH = B_KV * HEAD_DIM
WIN = 128
QBLK = 128
C_WIDTH = D_MODEL // 4
C_HEADS = C_WIDTH // HEAD_DIM
NB_ROWS = 8
NB_COLS = 16
COL_BLK = 16
COL_SPAN = 32
N_EXP = 16
EXP_FF = D_MODEL // 2
CAP_FACTOR = 2
A_COLS = 3 * A_WIDTH + 2 * DECAY_R + 2 * AAA_R + GATE_R
B_COLS = B_WIDTH + 2 * B_KV_WIDTH
C_COLS = 3 * C_WIDTH
GATE_COLS = 3 * D_MODEL
IN_W = A_COLS + B_COLS + C_COLS + GATE_COLS
ALPHA = (2 * DEPTH) ** 0.25
BETA = (8 * DEPTH) ** -0.25
ROPE_THETA = 10000.0
LN_EPS = 1e-6
GN_EPS = 64e-5
NEG = -1e30

kernel_name = 'hybrid_diffusion_prefix_trunk_step'


def layer_norm(x, g=None, b=None, eps=LN_EPS):
    xf = x.astype(jnp.float32)
    mu = xf.mean(-1, keepdims=True)
    var = jnp.mean(jnp.square(xf - mu), -1, keepdims=True)
    y = (xf - mu) * lax.rsqrt(var + eps)
    if g is not None:
        y = y * g + b
    return y.astype(x.dtype)


def ada_mods(cvec, w, b):
    m = jax.nn.silu(cvec) @ w + b
    return m.reshape(m.shape[:-1] + (6, D_MODEL))


def modulate(x, m, i):
    return layer_norm(x) * (1 + m[..., i + 1, :]) + m[..., i, :]


def axial_rope(x):
    n_tok = x.shape[1]
    t = jnp.arange(n_tok)
    row = (t // GRID_W).astype(jnp.float32)
    col = (t % GRID_W).astype(jnp.float32)
    half = HEAD_DIM // 2
    quarter = half // 2
    inv = ROPE_THETA ** (-jnp.arange(quarter, dtype=jnp.float32) / quarter)

    def rot(u, pos):
        ang = pos[:, None] * inv[None]
        cos = jnp.cos(ang).reshape(1, n_tok, 1, quarter).astype(u.dtype)
        sin = jnp.sin(ang).reshape(1, n_tok, 1, quarter).astype(u.dtype)
        u1, u2 = u[..., :quarter], u[..., quarter:]
        return jnp.concatenate([u1 * cos - u2 * sin, u1 * sin + u2 * cos], -1)

    return jnp.concatenate([rot(x[..., :half], row), rot(x[..., half:], col)], -1)


def split_in(u):
    bounds = np.cumsum([A_COLS, B_WIDTH, B_KV_WIDTH, B_KV_WIDTH, C_WIDTH, C_WIDTH, C_WIDTH]).tolist()
    parts = jnp.split(u, bounds, axis=-1)
    gates = parts[-1].reshape(u.shape[:-1] + (3, D_MODEL))
    return (*parts[:-1], gates)


def rwkv_scan(s0, r, decay, kk, a, k, v, reverse):
    def step(S, inp):
        r_t, w_t, kk_t, a_t, k_t, v_t = inp
        sa = jnp.einsum('bhvk,bhk->bhv', S, kk_t)
        S = (S * w_t[:, :, None, :] - sa[..., None] * (kk_t * a_t)[:, :, None, :]
             + v_t[..., None] * k_t[:, :, None, :])
        return S, jnp.einsum('bhvk,bhk->bhv', S, r_t)
    xs = tuple(jnp.swapaxes(u, 0, 1).astype(s0.dtype) for u in (r, decay, kk, a, k, v))
    s_fin, y = lax.scan(step, s0, xs, reverse=reverse)
    return s_fin, jnp.swapaxes(y, 0, 1)


def rwkv_branch(ua, rw, s0_fwd, s0_bwd):
    bsz, T = ua.shape[:2]
    up = jnp.pad(ua, ((0, 0), (1, 1), (0, 0)))
    ua = ua + rw['mu'] * (0.5 * (up[:, :-2] + up[:, 2:]) - ua)
    o3 = 3 * A_WIDTH
    r, k, v, wlo, alo, glo = jnp.split(
        ua, [A_WIDTH, 2 * A_WIDTH, o3, o3 + 2 * DECAY_R, o3 + 2 * DECAY_R + 2 * AAA_R], axis=-1)
    wlo = wlo.reshape(bsz, T, 2, DECAY_R)
    alo = alo.reshape(bsz, T, 2, AAA_R)
    wl = rw['w0'] + jnp.einsum('btdr,drc->btdc', jnp.tanh(wlo), rw['w2'])
    decay = jnp.exp(-jnp.exp(-jax.nn.softplus(-wl) - 0.5))
    a = jax.nn.sigmoid(rw['a0'] + jnp.einsum('btdr,drc->btdc', alo, rw['a2']))
    g = jax.nn.sigmoid(glo) @ rw['g2']
    kdir = k[:, :, None, :] * (1 + (a - 1) * rw['k_a'])

    def heads(u):
        return u.reshape(bsz, T, A_HEADS, A_HEAD)

    kk = heads(k * rw['k_k'])
    kk = kk / jnp.maximum(jnp.sqrt(jnp.sum(jnp.square(kk), -1, keepdims=True)), 1e-12)
    rh, kh, vh = heads(r), heads(k), heads(v)
    s_fwd, y_fwd = rwkv_scan(s0_fwd, rh, heads(decay[:, :, 0]), kk, heads(a[:, :, 0]),
                             heads(kdir[:, :, 0]), vh, False)
    s_bwd, y_bwd = rwkv_scan(s0_bwd, rh, heads(decay[:, :, 1]), kk, heads(a[:, :, 1]),
                             heads(kdir[:, :, 1]), vh, True)
    y = (y_fwd + y_bwd).astype(jnp.float32)
    mu_y = y.mean(-1, keepdims=True)
    var = jnp.mean(jnp.square(y - mu_y), -1, keepdims=True)
    y = ((y - mu_y) * lax.rsqrt(var + GN_EPS)).reshape(bsz, T, A_WIDTH) * rw['ln_w'] + rw['ln_b']
    bonus = jnp.sum(rh * kh * rw['r_k'], -1, keepdims=True) * vh
    out = (y.astype(ua.dtype) + bonus.reshape(bsz, T, A_WIDTH)) * g
    return out, s_fwd, s_bwd


def dense_attention(q, k, v, sink):
    bsz, T = q.shape[:2]
    S = k.shape[1]
    nb = T // QBLK
    qb = jnp.moveaxis(q.reshape((bsz, nb, QBLK) + q.shape[2:]), 1, 0)
    scale = HEAD_DIM ** -0.5

    def blk(qi):
        s = jnp.einsum('bqkgd,bskd->bkgqs', qi, k).astype(jnp.float32) * scale
        if sink is not None:
            sk = jnp.broadcast_to(sink.astype(jnp.float32)[None, :, :, None, None], s.shape[:-1] + (1,))
            s = jnp.concatenate([s, sk], -1)
        p = jax.nn.softmax(s, axis=-1)[..., :S].astype(v.dtype)
        return jnp.einsum('bkgqs,bskd->bqkgd', p, v)

    out = lax.map(blk, qb)
    return jnp.moveaxis(out, 0, 1).reshape(bsz, T, -1)


def window_attention(q, k, v, kc, vc, sink):
    bsz, N = q.shape[:2]
    P = kc.shape[1]
    nb = N // QBLK
    span = QBLK + 2 * WIN
    kp = jnp.pad(k, ((0, 0), (WIN, WIN), (0, 0), (0, 0)))
    vp = jnp.pad(v, ((0, 0), (WIN, WIN), (0, 0), (0, 0)))
    qb = jnp.moveaxis(q.reshape((bsz, nb, QBLK) + q.shape[2:]), 1, 0)
    qi = jnp.arange(QBLK)[:, None]
    kj = jnp.arange(span)[None, :]
    band = (kj - qi >= 0) & (kj - qi <= 2 * WIN)
    scale = HEAD_DIM ** -0.5

    def blk(args):
        b, qblk = args
        start = b * QBLK
        kb = lax.dynamic_slice_in_dim(kp, start, span, axis=1)
        vb = lax.dynamic_slice_in_dim(vp, start, span, axis=1)
        pos = start - WIN + kj
        mask = band & (pos >= 0) & (pos < N)
        s_loc = jnp.einsum('bqkgd,bskd->bkgqs', qblk, kb).astype(jnp.float32) * scale
        s_loc = jnp.where(mask, s_loc, NEG)
        s_ctx = jnp.einsum('bqkgd,bskd->bkgqs', qblk, kc).astype(jnp.float32) * scale
        sk = jnp.broadcast_to(sink.astype(jnp.float32)[None, :, :, None, None], s_loc.shape[:-1] + (1,))
        p = jax.nn.softmax(jnp.concatenate([s_loc, s_ctx, sk], -1), axis=-1).astype(v.dtype)
        return (jnp.einsum('bkgqs,bskd->bqkgd', p[..., :span], vb)
                + jnp.einsum('bkgqs,bskd->bqkgd', p[..., span:span + P], vc))

    out = lax.map(blk, (jnp.arange(nb), qb))
    return jnp.moveaxis(out, 0, 1).reshape(bsz, N, -1)


def neighbourhood_attention(q, k, v, kc, vc, rpb):
    bsz, N, H, hd = q.shape
    rows = N // GRID_W
    kh = min(NB_ROWS, rows)
    n_cb = GRID_W // COL_BLK
    P = kc.shape[1]
    scale = hd ** -0.5
    qg = q.reshape(bsz, rows, n_cb, COL_BLK, H, hd)
    r_idx = jnp.arange(rows)
    row_start = jnp.clip(r_idx - kh // 2, 0, rows - kh)
    key_rows = row_start[:, None] + jnp.arange(kh)[None]
    cb_start = np.clip(np.arange(n_cb) * COL_BLK - NB_COLS // 2, 0, GRID_W - COL_SPAN)
    key_cols = cb_start[:, None] + np.arange(COL_SPAN)[None]
    q_cols = np.arange(n_cb)[:, None] * COL_BLK + np.arange(COL_BLK)[None]
    win_start = np.clip(q_cols - NB_COLS // 2, 0, GRID_W - NB_COLS)
    kc3 = key_cols[:, None, :]
    col_mask = (kc3 >= win_start[..., None]) & (kc3 < win_start[..., None] + NB_COLS)
    col_off = np.clip(kc3 - q_cols[..., None] + NB_COLS - 1, 0, 2 * NB_COLS - 2)
    kcol = k.reshape(bsz, rows, GRID_W, H, hd)[:, :, key_cols]
    vcol = v.reshape(bsz, rows, GRID_W, H, hd)[:, :, key_cols]
    rpb = rpb.astype(jnp.float32)

    def row_fn(args):
        r, qr, kr_idx = args
        kr = jnp.take(kcol, kr_idx, axis=1)
        vr = jnp.take(vcol, kr_idx, axis=1)
        s = jnp.einsum('bmqhd,bkmshd->bhmqks', qr, kr).astype(jnp.float32) * scale
        row_off = kr_idx - r + NB_ROWS - 1
        bias = jnp.transpose(rpb[:, row_off][:, :, col_off], (0, 2, 3, 1, 4))
        s = jnp.where(col_mask[None, None, :, :, None, :], s + bias[None], NEG)
        s = s.reshape(bsz, H, n_cb, COL_BLK, kh * COL_SPAN)
        s_ctx = jnp.einsum('bmqhd,bphd->bhmqp', qr, kc).astype(jnp.float32) * scale
        p = jax.nn.softmax(jnp.concatenate([s, s_ctx], -1), axis=-1).astype(v.dtype)
        p_loc = p[..., :kh * COL_SPAN].reshape(bsz, H, n_cb, COL_BLK, kh, COL_SPAN)
        return (jnp.einsum('bhmqks,bkmshd->bmqhd', p_loc, vr)
                + jnp.einsum('bhmqp,bphd->bmqhd', p[..., kh * COL_SPAN:], vc))

    out = lax.map(row_fn, (r_idx, jnp.moveaxis(qg, 1, 0), key_rows))
    return jnp.moveaxis(out, 0, 1).reshape(bsz, N, H * hd)


def merge(gates, ya, yb, yc, lp):
    g = jax.nn.sigmoid(gates)
    m = (g[..., 0, :] * (ya @ lp['w_br_a']) + g[..., 1, :] * (yb @ lp['w_br_b'])
         + g[..., 2, :] * (yc @ lp['w_br_c']))
    return m @ lp['w_out']


def context_mixer(h, lp, rw):
    bsz, T = h.shape[:2]
    ua, qb, kb, vb, qn, kn, vn, gates = split_in(h @ lp['w_in'])
    s0 = jnp.zeros((bsz, A_HEADS, A_HEAD, A_HEAD), h.dtype)
    ya, s_fwd, s_bwd = rwkv_branch(ua, rw, s0, s0)
    kbh = kb.reshape(bsz, T, B_KV, HEAD_DIM)
    vbh = vb.reshape(bsz, T, B_KV, HEAD_DIM)
    yb = dense_attention(qb.reshape(bsz, T, B_KV, B_GROUP, HEAD_DIM), kbh, vbh,
                         lp['sink'].reshape(B_KV, B_GROUP))
    knh = kn.reshape(bsz, T, C_HEADS, HEAD_DIM)
    vnh = vn.reshape(bsz, T, C_HEADS, HEAD_DIM)
    yc = dense_attention(qn.reshape(bsz, T, C_HEADS, 1, HEAD_DIM), knh, vnh, None)
    return merge(gates, ya, yb, yc, lp), (kbh, vbh, knh, vnh, s_fwd, s_bwd)


def latent_mixer(h, lp, rw, ck_win, cv_win, ck_nb, cv_nb, s_fwd, s_bwd):
    bsz, N = h.shape[:2]
    ua, qb, kb, vb, qn, kn, vn, gates = split_in(h @ lp['w_in'])
    ya, _, _ = rwkv_branch(ua, rw, s_fwd, s_bwd)
    qbh = axial_rope(qb.reshape(bsz, N, B_HEADS, HEAD_DIM)).reshape(bsz, N, B_KV, B_GROUP, HEAD_DIM)
    kbh = axial_rope(kb.reshape(bsz, N, B_KV, HEAD_DIM))
    yb = window_attention(qbh, kbh, vb.reshape(bsz, N, B_KV, HEAD_DIM), ck_win, cv_win,
                          lp['sink'].reshape(B_KV, B_GROUP))
    yc = neighbourhood_attention(qn.reshape(bsz, N, C_HEADS, HEAD_DIM), kn.reshape(bsz, N, C_HEADS, HEAD_DIM),
                                 vn.reshape(bsz, N, C_HEADS, HEAD_DIM), ck_nb, cv_nb, lp['rpb'])
    return merge(gates, ya, yb, yc, lp)


def ec_moe(h, router_w, w1, w3, w2):
    shp = h.shape
    t = h.reshape(-1, D_MODEL)
    n = t.shape[0]
    cap = max(1, CAP_FACTOR * n // N_EXP)
    aff = jax.nn.softmax((t @ router_w).astype(jnp.float32), axis=-1)
    gval, gidx = lax.top_k(aff.T, cap)
    xe = jnp.take(t, gidx, axis=0)
    hid = jax.nn.silu(jnp.einsum('ecd,edf->ecf', xe, w1)) * jnp.einsum('ecd,edf->ecf', xe, w3)
    ye = jnp.einsum('ecf,efd->ecd', hid, w2) * gval[..., None].astype(t.dtype)
    y = jnp.zeros_like(t).at[gidx.reshape(-1)].add(ye.reshape(-1, D_MODEL))
    return y.reshape(shp)


def setup_inputs(seed: int = 0) -> dict:
    key = jax.random.key(seed)
    keys = jax.random.split(key, 40)
    f32 = jnp.float32

    def nrm(i, shape, scale=1.0, shift=0.0):
        return jax.random.normal(keys[i], shape, f32) * scale + shift

    L, D = DEPTH, D_MODEL
    return {
        'x_prompt': nrm(0, (BATCH, SEQ, D)),
        'x_sample': nrm(1, (DEC_BATCH, DEC_SEQ, D)),
        'cache_win_k': nrm(2, (DEC_BATCH, L, PAST_LEN, B_KV, HEAD_DIM)),
        'cache_win_v': nrm(3, (DEC_BATCH, L, PAST_LEN, B_KV, HEAD_DIM)),
        'cache_nb_k': nrm(4, (DEC_BATCH, L, PAST_LEN, C_HEADS, HEAD_DIM)),
        'cache_nb_v': nrm(5, (DEC_BATCH, L, PAST_LEN, C_HEADS, HEAD_DIM)),
        'state_rwkv_fwd': nrm(6, (DEC_BATCH, L, A_HEADS, A_HEAD, A_HEAD), 0.3),
        'state_rwkv_bwd': nrm(7, (DEC_BATCH, L, A_HEADS, A_HEAD, A_HEAD), 0.3),
        'c': nrm(8, (DEC_BATCH, D)),
        'c_ctx': nrm(9, (D,)),
        'ada_w': nrm(10, (L, D, 6 * D), D ** -0.5),
        'ada_b': nrm(11, (L, 6 * D), 0.02),
        'w_in': nrm(12, (L, D, IN_W), D ** -0.5),
        'rwkv_mu': jax.random.uniform(keys[13], (L, A_COLS), f32),
        'rwkv_w0': nrm(14, (L, 2, A_WIDTH), 0.5, -1.5),
        'rwkv_w2': nrm(15, (L, 2, DECAY_R, A_WIDTH), 0.5 * DECAY_R ** -0.5),
        'rwkv_a0': nrm(16, (L, 2, A_WIDTH), 0.5),
        'rwkv_a2': nrm(17, (L, 2, AAA_R, A_WIDTH), 0.5 * AAA_R ** -0.5),
        'rwkv_g2': nrm(18, (L, GATE_R, A_WIDTH), GATE_R ** -0.5),
        'rwkv_kk': nrm(19, (L, A_WIDTH), 0.05, 0.85),
        'rwkv_ka': nrm(20, (L, A_WIDTH), 0.05, 1.0),
        'rwkv_rk': nrm(21, (L, A_HEADS, A_HEAD), 0.1),
        'rwkv_lnx_w': nrm(22, (L, A_WIDTH), 0.05, 1.0),
        'rwkv_lnx_b': nrm(23, (L, A_WIDTH), 0.02),
        'win_sink': nrm(24, (L, B_HEADS), 0.5),
        'nb_rpb': nrm(25, (L, C_HEADS, 2 * NB_ROWS - 1, 2 * NB_COLS - 1), 0.1),
        'w_br_a': nrm(26, (L, A_WIDTH, D), A_WIDTH ** -0.5),
        'w_br_b': nrm(27, (L, B_WIDTH, D), B_WIDTH ** -0.5),
        'w_br_c': nrm(28, (L, C_WIDTH, D), C_WIDTH ** -0.5),
        'w_out': nrm(29, (L, D, D), BETA * D ** -0.5),
        'ln1_g': nrm(30, (L, D), 0.05, 1.0),
        'ln1_b': nrm(31, (L, D), 0.02),
        'ln2_g': nrm(32, (L, D), 0.05, 1.0),
        'ln2_b': nrm(33, (L, D), 0.02),
        'router_w': nrm(34, (L, D, N_EXP), D ** -0.5),
        'exp_w1': nrm(35, (L, N_EXP, D, EXP_FF), D ** -0.5),
        'exp_w3': nrm(36, (L, N_EXP, D, EXP_FF), D ** -0.5),
        'exp_w2': nrm(37, (L, N_EXP, EXP_FF, D), BETA * EXP_FF ** -0.5),
    }


def reference(x_prompt, x_sample, cache_win_k, cache_win_v, cache_nb_k, cache_nb_v,
              state_rwkv_fwd, state_rwkv_bwd, c, c_ctx, ada_w, ada_b, w_in, rwkv_mu,
              rwkv_w0, rwkv_w2, rwkv_a0, rwkv_a2, rwkv_g2, rwkv_kk, rwkv_ka, rwkv_rk,
              rwkv_lnx_w, rwkv_lnx_b, win_sink, nb_rpb, w_br_a, w_br_b, w_br_c, w_out,
              ln1_g, ln1_b, ln2_g, ln2_b, router_w, exp_w1, exp_w3, exp_w2):
    xp = x_prompt
    xs = x_sample
    win_k, win_v, nb_k, nb_v, st_f, st_b = [], [], [], [], [], []
    for l in range(DEPTH):
        lp = {'w_in': w_in[l], 'sink': win_sink[l], 'rpb': nb_rpb[l], 'w_br_a': w_br_a[l],
              'w_br_b': w_br_b[l], 'w_br_c': w_br_c[l], 'w_out': w_out[l]}
        rw = {'mu': rwkv_mu[l], 'w0': rwkv_w0[l], 'w2': rwkv_w2[l], 'a0': rwkv_a0[l],
              'a2': rwkv_a2[l], 'g2': rwkv_g2[l], 'k_k': rwkv_kk[l], 'k_a': rwkv_ka[l],
              'r_k': rwkv_rk[l], 'ln_w': rwkv_lnx_w[l], 'ln_b': rwkv_lnx_b[l]}
        mc = ada_mods(c_ctx, ada_w[l], ada_b[l])[None, None]
        mix, ctx_tensors = context_mixer(modulate(xp, mc, 0), lp, rw)
        xp = layer_norm(ALPHA * xp + mc[..., 2, :] * mix, ln1_g[l], ln1_b[l])
        ff = ec_moe(modulate(xp, mc, 3), router_w[l], exp_w1[l], exp_w3[l], exp_w2[l])
        xp = layer_norm(ALPHA * xp + mc[..., 5, :] * ff, ln2_g[l], ln2_b[l])
        win_k.append(ctx_tensors[0]); win_v.append(ctx_tensors[1])
        nb_k.append(ctx_tensors[2]); nb_v.append(ctx_tensors[3])
        st_f.append(ctx_tensors[4]); st_b.append(ctx_tensors[5])
        ms = ada_mods(c, ada_w[l], ada_b[l])[:, None]
        mix = latent_mixer(modulate(xs, ms, 0), lp, rw, cache_win_k[:, l], cache_win_v[:, l],
                           cache_nb_k[:, l], cache_nb_v[:, l], state_rwkv_fwd[:, l], state_rwkv_bwd[:, l])
        xs = layer_norm(ALPHA * xs + ms[..., 2, :] * mix, ln1_g[l], ln1_b[l])
        ff = ec_moe(modulate(xs, ms, 3), router_w[l], exp_w1[l], exp_w3[l], exp_w2[l])
        xs = layer_norm(ALPHA * xs + ms[..., 5, :] * ff, ln2_g[l], ln2_b[l])
    new_win_k = jnp.stack(win_k, axis=1)
    new_win_v = jnp.stack(win_v, axis=1)
    new_nb_k = jnp.stack(nb_k, axis=1)
    new_nb_v = jnp.stack(nb_v, axis=1)
    new_state_fwd = jnp.stack(st_f, axis=1)
    new_state_bwd = jnp.stack(st_b, axis=1)
    return (xp, xs, new_win_k, new_win_v, new_nb_k, new_nb_v, new_state_fwd, new_state_bwd)
```

```python
import functools

import numpy as np
import jax
import jax.numpy as jnp
from jax import lax
from jax.experimental import pallas as pl
from jax.experimental.pallas import tpu as pltpu

F32 = jnp.float32
BF16 = jnp.bfloat16

HEAD_DIM = 128
A_HEAD = 64
GRID_W = 64
WIN = 128
QBLK = 128
NB_ROWS = 8
NB_COLS = 16
CAP_FACTOR = 2
ROPE_THETA = 10000.0
LN_EPS = 1e-6
GN_EPS = 64e-5
NEG = -1e30

LANE = 128
SUBLANE = 8
VMEM_LIMIT_BYTES = 56 * 1024 * 1024

SCAN_CHUNK = 64
SLOT_BLK = 128
TOK_TILE = 256


def _cparams(*sem):
    return pltpu.CompilerParams(dimension_semantics=sem, vmem_limit_bytes=VMEM_LIMIT_BYTES)


def _dot(a, b):
    return jnp.dot(a, b, preferred_element_type=F32)


def _dot_nt(a, b):
    return lax.dot_general(a, b, (((1,), (1,)), ((), ())), preferred_element_type=F32)


def _split2(x):
    hi = x.astype(BF16)
    lo = (x - hi.astype(F32)).astype(BF16)
    return hi, lo


def _split3(x):
    hi = x.astype(BF16)
    r1 = x - hi.astype(F32)
    mid = r1.astype(BF16)
    lo = (r1 - mid.astype(F32)).astype(BF16)
    return hi, mid, lo


def _dot3(a, b):
    ah, al = _split2(a)
    bh, bl = _split2(b)
    return _dot(ah, bh) + (_dot(ah, bl) + _dot(al, bh))


def _dot3_nt(a, b):
    ah, al = _split2(a)
    bh, bl = _split2(b)
    return _dot_nt(ah, bh) + (_dot_nt(ah, bl) + _dot_nt(al, bh))


def _dot_exact_lhs(a_bf16, b):
    bh, bm, bl = _split3(b)
    return _dot(a_bf16, bh) + (_dot(a_bf16, bm) + _dot(a_bf16, bl))


def _sigmoid(x):
    return 1.0 / (1.0 + jnp.exp(-x))


def _iota(shape, dim):
    return lax.broadcasted_iota(jnp.int32, shape, dim)


def _ada_kernel(c_ref, w_ref, b_ref, o_ref):
    c = c_ref[...]
    a = c * _sigmoid(c)
    o_ref[...] = _dot3(a, w_ref[...]) + b_ref[...]


def ada_mods(cvecs, ada_w, ada_b, tn=512):
    depth, d, n6 = ada_w.shape
    rows = cvecs.shape[0]
    return pl.pallas_call(
        _ada_kernel,
        grid=(depth, n6 // tn),
        in_specs=[
            pl.BlockSpec((rows, d), lambda l, j: (0, 0)),
            pl.BlockSpec((None, d, tn), lambda l, j: (l, 0, j)),
            pl.BlockSpec((None, 1, tn), lambda l, j: (l, 0, j)),
        ],
        out_specs=pl.BlockSpec((None, rows, tn), lambda l, j: (l, 0, j)),
        out_shape=jax.ShapeDtypeStruct((depth, rows, n6), F32),
        compiler_params=_cparams("parallel", "parallel"),
        name="ada_mods",
    )(cvecs, ada_w, ada_b.reshape(depth, 1, n6))


def _ln_rows(x):
    mu = jnp.mean(x, axis=-1, keepdims=True)
    xc = x - mu
    var = jnp.mean(xc * xc, axis=-1, keepdims=True)
    return xc * lax.rsqrt(var + LN_EPS)


def _lnmod_kernel(x_ref, sh_ref, sc_ref, o_ref):
    y = _ln_rows(x_ref[...])
    o_ref[...] = (y * (1.0 + sc_ref[...]) + sh_ref[...]).astype(o_ref.dtype)


def _mod_spec(mod, d):
    if mod.shape[0] == 1:
        return pl.BlockSpec((None, 1, d), lambda b, i: (0, 0, 0))
    return pl.BlockSpec((None, 1, d), lambda b, i: (b, 0, 0))


def ln_modulate(x, shift, scale, tt=256):
    bsz, t, d = x.shape
    return pl.pallas_call(
        _lnmod_kernel,
        grid=(bsz, t // tt),
        in_specs=[
            pl.BlockSpec((None, tt, d), lambda b, i: (b, i, 0)),
            _mod_spec(shift, d),
            _mod_spec(scale, d),
        ],
        out_specs=pl.BlockSpec((None, tt, d), lambda b, i: (b, i, 0)),
        out_shape=jax.ShapeDtypeStruct((bsz, t, d), BF16),
        compiler_params=_cparams("parallel", "parallel"),
        name="ln_modulate",
    )(x, shift, scale)


def _lnmod_router_kernel(n_exp, x_ref, sh_ref, sc_ref, rw_ref, h_ref, aff_ref):
    y = _ln_rows(x_ref[...])
    h = y * (1.0 + sc_ref[...]) + sh_ref[...]
    h_ref[...] = h.astype(h_ref.dtype)
    logits = _dot3(h, rw_ref[...])
    lane = _iota(logits.shape, 1)
    logits = jnp.where(lane < n_exp, logits, NEG)
    m = jnp.max(logits, axis=-1, keepdims=True)
    e = jnp.exp(logits - m)
    aff_ref[...] = e / jnp.sum(e, axis=-1, keepdims=True)


def ln_modulate_router(x, shift, scale, router_w, tt=256):
    bsz, t, d = x.shape
    n_exp = router_w.shape[1]
    rw = jnp.pad(router_w, ((0, 0), (0, LANE - n_exp)))
    return pl.pallas_call(
        functools.partial(_lnmod_router_kernel, n_exp),
        grid=(bsz, t // tt),
        in_specs=[
            pl.BlockSpec((None, tt, d), lambda b, i: (b, i, 0)),
            _mod_spec(shift, d),
            _mod_spec(scale, d),
            pl.BlockSpec((d, LANE), lambda b, i: (0, 0)),
        ],
        out_specs=[
            pl.BlockSpec((None, tt, d), lambda b, i: (b, i, 0)),
            pl.BlockSpec((None, tt, LANE), lambda b, i: (b, i, 0)),
        ],
        out_shape=[
            jax.ShapeDtypeStruct((bsz, t, d), BF16),
            jax.ShapeDtypeStruct((bsz, t, LANE), F32),
        ],
        compiler_params=_cparams("parallel", "parallel"),
        name="ln_modulate_router",
    )(x, shift, scale, rw)


def _mm_kernel(a_ref, b_ref, o_ref):
    o_ref[...] = _dot(a_ref[...], b_ref[...]).astype(o_ref.dtype)


def _pick(n, pref):
    for c in pref:
        if n % c == 0:
            return c
    return n


def matmul(a, b, out_dtype=F32, tm=None, tn=None, name="matmul"):
    m, k = a.shape
    _, n = b.shape
    tm = tm or _pick(m, (1024, 512, 256, 128))
    tn = tn or _pick(n, (768, 512, 256, 128))
    return pl.pallas_call(
        _mm_kernel,
        grid=(m // tm, n // tn),
        in_specs=[
            pl.BlockSpec((tm, k), lambda i, j: (i, 0)),
            pl.BlockSpec((k, tn), lambda i, j: (0, j)),
        ],
        out_specs=pl.BlockSpec((tm, tn), lambda i, j: (i, j)),
        out_shape=jax.ShapeDtypeStruct((m, n), out_dtype),
        compiler_params=_cparams("parallel", "parallel"),
        name=name,
    )(a, b)


HEAD_SHIFT = 6
assert (1 << HEAD_SHIFT) == A_HEAD and 2 * A_HEAD == LANE and SCAN_CHUNK == A_HEAD


def _head_block_ones():
    i = _iota((LANE, LANE), 0) >> HEAD_SHIFT
    j = _iota((LANE, LANE), 1) >> HEAD_SHIFT
    return (i == j).astype(BF16)


def _seg_sum(x):
    bd = _head_block_ones()
    outs = []
    for c in range(x.shape[1] // LANE):
        hi, mid, lo = _split3(x[:, c * LANE:(c + 1) * LANE])
        outs.append(_dot(hi, bd) + (_dot(mid, bd) + _dot(lo, bd)))
    return jnp.concatenate(outs, axis=1)


def _rwkv_prep_kernel(aw, gp, tiles_per_seq,
                      x_ref, xp_ref, xn_ref, mu_ref, w0_ref, w2_ref, a0_ref, a2_ref, g2_ref,
                      kkp_ref, kap_ref, rkp_ref,
                      r_o, v_o, kk_o, lwf_o, lwb_o, bf_o, bb_o, kdf_o, kdb_o, bonus_o, g_o):
    i = pl.program_id(0)
    x = x_ref[...]
    tt = x.shape[0]
    row = _iota(x.shape, 0)
    pos = i % tiles_per_seq
    prev_row = jnp.where(pos == 0, 0.0, xp_ref[SUBLANE - 1:SUBLANE, :])
    next_row = jnp.where(pos == tiles_per_seq - 1, 0.0, xn_ref[0:1, :])
    prev = jnp.where(row == 0, prev_row, pltpu.roll(x, 1, 0))
    nxt = jnp.where(row == tt - 1, next_row, pltpu.roll(x, tt - 1, 0))
    xs = x + mu_ref[...] * (0.5 * (prev + nxt) - x)

    r = xs[:, 0:aw]
    k = xs[:, aw:2 * aw]
    v = xs[:, 2 * aw:3 * aw]
    o = 3 * aw
    wlo = jnp.tanh(xs[:, o:o + LANE])
    alo = xs[:, o + LANE:o + 2 * LANE]
    glo = _sigmoid(xs[:, o + 2 * LANE:o + 2 * LANE + gp])
    lane = _iota((1, LANE), 1)
    exp_mhalf = float(np.exp(-0.5))

    r_o[...] = r
    v_o[...] = v
    g_o[...] = _dot3(glo, g2_ref[...])

    kk = k * kkp_ref[...]
    nrm = jnp.sqrt(_seg_sum(kk * kk))
    kk = kk / jnp.maximum(nrm, 1e-12)
    kk_o[...] = kk
    bonus_o[...] = _seg_sum(r * k * rkp_ref[...]) * v

    for d, (lw_o, b_o, kd_o) in enumerate(((lwf_o, bf_o, kdf_o), (lwb_o, bb_o, kdb_o))):
        sel = ((lane >> HEAD_SHIFT) == d).astype(F32)
        wl = w0_ref[d:d + 1, :] + _dot3(wlo * sel, w2_ref[...])
        lw_o[...] = -_sigmoid(wl) * exp_mhalf
        a = _sigmoid(a0_ref[d:d + 1, :] + _dot3(alo * sel, a2_ref[...]))
        b_o[...] = kk * a
        kd_o[...] = k * (1.0 + (a - 1.0) * kap_ref[...])


def rwkv_prep(u, seq_len, rw, aw, ap, tt=128):
    n = u.shape[0]
    gp = ap - 3 * aw - 2 * LANE
    n8 = n // SUBLANE
    tpb = tt // SUBLANE
    full = lambda shape: pl.BlockSpec(shape, lambda i: (0,) * len(shape))
    tok = pl.BlockSpec((tt, aw), lambda i: (i, 0))
    outs = pl.pallas_call(
        functools.partial(_rwkv_prep_kernel, aw, gp, seq_len // tt),
        grid=(n // tt,),
        in_specs=[
            pl.BlockSpec((tt, ap), lambda i: (i, 0)),
            pl.BlockSpec((SUBLANE, ap), lambda i: (jnp.maximum(i * tpb - 1, 0), 0)),
            pl.BlockSpec((SUBLANE, ap), lambda i: (jnp.minimum((i + 1) * tpb, n8 - 1), 0)),
            full((1, ap)), full((2, aw)), full((LANE, aw)), full((2, aw)), full((LANE, aw)),
            full((gp, aw)), full((1, aw)), full((1, aw)), full((1, aw)),
        ],
        out_specs=[tok] * 11,
        out_shape=[jax.ShapeDtypeStruct((n, aw), F32)] * 11,
        compiler_params=_cparams("parallel"),
        name="rwkv_prep",
    )(u, u, u, rw["mu"], rw["w0"], rw["w2"], rw["a0"], rw["a2"], rw["g2"],
      rw["k_k"], rw["k_a"], rw["r_k"])
    return outs


def _scan_chunk(rev, lw, kk, bb, kd, r, v, st, consts):
    tri, strict, incl, blk, eye, m0, m1 = consts
    c = SCAN_CHUNK
    big_l = _dot_exact_lhs(tri, lw)
    l_tot = big_l[0:1, :] if rev else big_l[c - 1:c, :]
    at = -kk * jnp.exp(big_l - lw)
    rt = r * jnp.exp(big_l)
    gi = jnp.exp(-big_l)
    bt = bb * gi
    kt = kd * gi
    gr = jnp.exp(l_tot - big_l)

    def bd(x):
        return jnp.where(blk, jnp.concatenate([x, x], axis=0), 0.0)

    lhs = jnp.concatenate([at, rt], axis=0)
    rhs = jnp.concatenate([bt * m0, bt * m1, kt * m0, kt * m1], axis=0)
    p = _dot3_nt(lhs, rhs)
    nab = jnp.where(strict, p[0:c, 0:LANE], 0.0)
    nak = jnp.where(strict, p[0:c, LANE:2 * LANE], 0.0)
    nrb = jnp.where(incl, p[c:2 * c, 0:LANE], 0.0)
    nrk = jnp.where(incl, p[c:2 * c, LANE:2 * LANE], 0.0)
    z = _dot3(lhs, st)
    w = _dot3(jnp.concatenate([nak, nrk], axis=0), bd(v))
    x = z[0:c] + w[0:c]
    n = nab
    rounds = int(np.log2(c))
    for j in range(rounds):
        x = x + _dot3(n, bd(x))
        if j < rounds - 1:
            n = _dot3(n, bd(n))
    y = z[c:2 * c] + w[c:2 * c] + _dot3(nrb, bd(x))
    g_col = jnp.sum(jnp.where(eye, jnp.exp(l_tot), 0.0), axis=1, keepdims=True)
    upd = _dot3(jnp.concatenate([bb * gr, kd * gr], axis=0).T, jnp.concatenate([x, v], axis=0))
    st_new = st * g_col + jnp.where(blk, upd, 0.0)
    return y, st_new


def _scan_kernel(rev, npar, n_chunks, lw_ref, kk_ref, bb_ref, kd_ref, r_ref, v_ref, s0_ref,
                 y_ref, st_out_ref, st_scr):
    ci = pl.program_id(2)

    @pl.when(ci == 0)
    def _():
        st_scr[...] = s0_ref[...]

    c = SCAN_CHUNK
    ti = _iota((c, c), 0)
    si = _iota((c, c), 1)
    tri = ((si >= ti) if rev else (si <= ti)).astype(BF16)
    t2 = _iota((c, LANE), 0)
    s2 = _iota((c, LANE), 1) & (c - 1)
    strict = (s2 > t2) if rev else (s2 < t2)
    incl = (s2 >= t2) if rev else (s2 <= t2)
    bi = _iota((LANE, LANE), 0)
    bj = _iota((LANE, LANE), 1)
    blk = (bi >> HEAD_SHIFT) == (bj >> HEAD_SHIFT)
    eye = bi == bj
    m0 = ((_iota((1, LANE), 1) >> HEAD_SHIFT) == 0).astype(F32)
    consts = (tri, strict, incl, blk, eye, m0, 1.0 - m0)
    for p in range(npar):
        sl = slice(p * LANE, (p + 1) * LANE)
        y, st_new = _scan_chunk(rev, lw_ref[:, sl], kk_ref[:, sl], bb_ref[:, sl], kd_ref[:, sl],
                                r_ref[:, sl], v_ref[:, sl], st_scr[p], consts)
        y_ref[:, sl] = y
        st_scr[p] = st_new

    @pl.when(ci == n_chunks - 1)
    def _():
        st_out_ref[...] = st_scr[...]


def rwkv_scan(lw, kk, bb, kd, r, v, s0, rev, npar=2):
    bsz, t, aw = lw.shape
    n_pairs = aw // LANE
    npar = npar if n_pairs % npar == 0 else 1
    n_chunks = t // SCAN_CHUNK
    if rev:
        tmap = lambda b, g, c: (b, n_chunks - 1 - c, g)
    else:
        tmap = lambda b, g, c: (b, c, g)
    tok = pl.BlockSpec((None, SCAN_CHUNK, npar * LANE), tmap)
    st = pl.BlockSpec((None, npar, LANE, LANE), lambda b, g, c: (b, g, 0, 0))
    return pl.pallas_call(
        functools.partial(_scan_kernel, rev, npar, n_chunks),
        grid=(bsz, n_pairs // npar, n_chunks),
        in_specs=[tok] * 6 + [st],
        out_specs=[tok, st],
        out_shape=[jax.ShapeDtypeStruct((bsz, t, aw), F32),
                   jax.ShapeDtypeStruct(s0.shape, F32)],
        scratch_shapes=[pltpu.VMEM((npar, LANE, LANE), F32)],
        compiler_params=_cparams("parallel", "parallel", "arbitrary"),
        name="rwkv_scan_bwd" if rev else "rwkv_scan_fwd",
    )(lw, kk, bb, kd, r, v, s0)


def _rwkv_post_kernel(yf_ref, yb_ref, bonus_ref, g_ref, lnw_ref, lnb_ref, o_ref):
    y = yf_ref[...] + yb_ref[...]
    inv = 1.0 / A_HEAD
    mu = _seg_sum(y) * inv
    yc = y - mu
    var = _seg_sum(yc * yc) * inv
    yn = yc * lax.rsqrt(var + GN_EPS) * lnw_ref[...] + lnb_ref[...]
    o_ref[...] = ((yn + bonus_ref[...]) * g_ref[...]).astype(o_ref.dtype)


def rwkv_post(yf, yb, bonus, g, ln_w, ln_b, tt=256):
    n, aw = yf.shape
    tok = pl.BlockSpec((tt, aw), lambda i: (i, 0))
    par = pl.BlockSpec((1, aw), lambda i: (0, 0))
    return pl.pallas_call(
        _rwkv_post_kernel,
        grid=(n // tt,),
        in_specs=[tok, tok, tok, tok, par, par],
        out_specs=tok,
        out_shape=jax.ShapeDtypeStruct((n, aw), BF16),
        compiler_params=_cparams("parallel"),
        name="rwkv_post",
    )(yf, yb, bonus, g, ln_w, ln_b)


def round_up(x, m):
    return (x + m - 1) // m * m


ATT_SCALE = HEAD_DIM ** -0.5


def _ctx_attn_kernel(q_ref, k_ref, v_ref, sink_ref, o_ref):
    s = _dot3_nt(q_ref[...] * ATT_SCALE, k_ref[...])
    sk = sink_ref[0:1, 0:1]
    m = jnp.maximum(jnp.max(s, axis=-1, keepdims=True), sk)
    e = jnp.exp(s - m)
    den = jnp.sum(e, axis=-1, keepdims=True) + jnp.exp(sk - m)
    o = _dot(e.astype(BF16), v_ref[...].astype(BF16))
    o_ref[...] = (o / den).astype(o_ref.dtype)


def ctx_attention(u, bsz, t, cols, sink, n_b, group, n_c):
    qb0, kb0, vb0, qn0, kn0, vn0 = cols
    nh = n_b + n_c
    sink_all = jnp.concatenate([sink.astype(F32), jnp.full((n_c,), NEG, F32)])
    sink_all = jnp.broadcast_to(sink_all[:, None, None], (nh, 1, LANE))
    qcol = lambda h: jnp.where(h < n_b, qb0 + h, qn0 + h - n_b)
    kcol = lambda h: jnp.where(h < n_b, kb0 + h // group, kn0 + h - n_b)
    vcol = lambda h: jnp.where(h < n_b, vb0 + h // group, vn0 + h - n_b)
    return pl.pallas_call(
        _ctx_attn_kernel,
        grid=(bsz, nh),
        in_specs=[
            pl.BlockSpec((t, HEAD_DIM), lambda b, h: (b, qcol(h))),
            pl.BlockSpec((t, HEAD_DIM), lambda b, h: (b, kcol(h))),
            pl.BlockSpec((t, HEAD_DIM), lambda b, h: (b, vcol(h))),
            pl.BlockSpec((None, 1, LANE), lambda b, h: (h, 0, 0)),
        ],
        out_specs=pl.BlockSpec((t, HEAD_DIM), lambda b, h: (b, h)),
        out_shape=jax.ShapeDtypeStruct((bsz * t, nh * HEAD_DIM), BF16),
        compiler_params=_cparams("parallel", "parallel"),
        name="ctx_attention",
    )(u, u, u, sink_all)


def rope_tables(n_tok):
    half = HEAD_DIM // 2
    quarter = half // 2
    tok = jnp.arange(n_tok)
    row = (tok // GRID_W).astype(F32)
    col = (tok % GRID_W).astype(F32)
    inv = ROPE_THETA ** (-jnp.arange(quarter, dtype=F32) / quarter)
    ang_r = row[:, None] * inv[None]
    ang_c = col[:, None] * inv[None]
    cos = jnp.concatenate([jnp.cos(ang_r)] * 2 + [jnp.cos(ang_c)] * 2, axis=1)
    sr, sc = jnp.sin(ang_r), jnp.sin(ang_c)
    z = jnp.zeros_like(sr)
    sin_a = jnp.concatenate([-sr, z, -sc, z], axis=1)
    sin_b = jnp.concatenate([z, sr, z, sc], axis=1)
    return cos, sin_a, sin_b


def _rope_kernel(x_ref, cos_ref, sa_ref, sb_ref, o_ref):
    x = x_ref[...]
    q = HEAD_DIM // 4
    o_ref[...] = (x * cos_ref[...] + pltpu.roll(x, HEAD_DIM - q, 1) * sa_ref[...]
                  + pltpu.roll(x, q, 1) * sb_ref[...])


def rope(u, bsz, n_tok, col0, n_heads, tt=256):
    cos, sin_a, sin_b = rope_tables(n_tok)
    nt = n_tok // tt
    tab = pl.BlockSpec((tt, HEAD_DIM), lambda b, i, h: (i, 0))
    return pl.pallas_call(
        _rope_kernel,
        grid=(bsz, nt, n_heads),
        in_specs=[pl.BlockSpec((tt, HEAD_DIM), lambda b, i, h: (b * nt + i, col0 + h)), tab, tab, tab],
        out_specs=pl.BlockSpec((tt, HEAD_DIM), lambda b, i, h: (b * nt + i, h)),
        out_shape=jax.ShapeDtypeStruct((bsz * n_tok, n_heads * HEAD_DIM), F32),
        compiler_params=_cparams("parallel", "parallel", "parallel"),
        name="rope",
    )(u, cos, sin_a, sin_b)


def _win_attn_kernel(group, n_tok, sink_ref, q_ref, kp_ref, kc_ref, kn_ref, vp_ref, vc_ref, vn_ref,
                     ck_ref, cv_ref, o_ref):
    kv = pl.program_id(1)
    i = pl.program_id(2)
    rows = group * QBLK
    q = jnp.concatenate([q_ref[:, g * HEAD_DIM:(g + 1) * HEAD_DIM] for g in range(group)], axis=0)
    q = q * ATT_SCALE
    k_loc = jnp.concatenate([kp_ref[...], kc_ref[...], kn_ref[...]], axis=0)
    v_loc = jnp.concatenate([vp_ref[...], vc_ref[...], vn_ref[...]], axis=0).astype(BF16)
    s_loc = _dot3_nt(q, k_loc)
    qi = _iota(s_loc.shape, 0) & (QBLK - 1)
    kj = _iota(s_loc.shape, 1)
    pos = i * QBLK - WIN + kj
    ok = (kj - qi >= 0) & (kj - qi <= 2 * WIN) & (pos >= 0) & (pos < n_tok)
    s_loc = jnp.where(ok, s_loc, NEG)
    s_ctx = _dot3_nt(q, ck_ref[...])
    rg = _iota((rows, 1), 0) >> int(np.log2(QBLK))
    sk = jnp.zeros((rows, 1), F32)
    for g in range(group):
        sk = jnp.where(rg == g, sink_ref[kv * group + g], sk)
    m = jnp.maximum(jnp.maximum(jnp.max(s_loc, axis=-1, keepdims=True),
                                jnp.max(s_ctx, axis=-1, keepdims=True)), sk)
    e_loc = jnp.exp(s_loc - m)
    e_ctx = jnp.exp(s_ctx - m)
    den = (jnp.sum(e_loc, axis=-1, keepdims=True) + jnp.sum(e_ctx, axis=-1, keepdims=True)
           + jnp.exp(sk - m))
    o = (_dot(e_loc.astype(BF16), v_loc) + _dot(e_ctx.astype(BF16), cv_ref[...].astype(BF16))) / den
    for g in range(group):
        o_ref[:, g * HEAD_DIM:(g + 1) * HEAD_DIM] = o[g * QBLK:(g + 1) * QBLK].astype(o_ref.dtype)


def window_attention(qk_rot, u, vb0, bsz, n_tok, n_kv, group, ck, cv, sink):
    assert WIN == QBLK
    nb = n_tok // QBLK
    nq = n_kv * group
    past = ck.shape[2]
    prv = lambda i: jnp.maximum(i - 1, 0)
    nxt = lambda i: jnp.minimum(i + 1, nb - 1)
    blk = lambda f, c0: pl.BlockSpec((QBLK, HEAD_DIM), lambda b, kv, i, s: (b * nb + f(i), c0 + kv))
    same = lambda i: i
    cache = pl.BlockSpec((None, None, past, HEAD_DIM), lambda b, kv, i, s: (b, kv, 0, 0))
    grid_spec = pltpu.PrefetchScalarGridSpec(
        num_scalar_prefetch=1,
        grid=(bsz, n_kv, nb),
        in_specs=[
            pl.BlockSpec((QBLK, group * HEAD_DIM), lambda b, kv, i, s: (b * nb + i, kv)),
            blk(prv, nq), blk(same, nq), blk(nxt, nq),
            blk(prv, vb0), blk(same, vb0), blk(nxt, vb0),
            cache, cache,
        ],
        out_specs=pl.BlockSpec((QBLK, group * HEAD_DIM), lambda b, kv, i, s: (b * nb + i, kv)),
    )
    return pl.pallas_call(
        functools.partial(_win_attn_kernel, group, n_tok),
        grid_spec=grid_spec,
        out_shape=jax.ShapeDtypeStruct((bsz * n_tok, nq * HEAD_DIM), BF16),
        compiler_params=_cparams("parallel", "parallel", "parallel"),
        name="window_attention",
    )(sink.astype(F32), qk_rot, qk_rot, qk_rot, qk_rot, u, u, u, ck, cv)


NB_QROWS = 8
NB_QTOK = NB_QROWS * GRID_W
NB_KTOK = 3 * NB_QTOK


def nb_tables(rpb, rows):
    kh = min(NB_ROWS, rows)
    ql = np.arange(NB_QTOK)
    kl = np.arange(NB_KTOK)
    r_rel, c = ql // GRID_W, ql % GRID_W
    kr_rel, kc = kl // GRID_W - NB_QROWS, kl % GRID_W
    row_off = np.clip(kr_rel[None, :] - r_rel[:, None] + NB_ROWS - 1, 0, 2 * NB_ROWS - 2)
    col_off = np.clip(kc[None, :] - c[:, None] + NB_COLS - 1, 0, 2 * NB_COLS - 2)
    bias = rpb.astype(F32)[:, row_off, col_off]
    win_start = np.clip(c - NB_COLS // 2, 0, GRID_W - NB_COLS)
    col_ok = (kc[None, :] >= win_start[:, None]) & (kc[None, :] < win_start[:, None] + NB_COLS)
    masks = []
    for j in range(rows // NB_QROWS):
        r = j * NB_QROWS + r_rel
        kr = j * NB_QROWS + kr_rel
        row_start = np.clip(r - kh // 2, 0, rows - kh)
        row_ok = (kr[None, :] >= row_start[:, None]) & (kr[None, :] < row_start[:, None] + kh)
        masks.append(row_ok & col_ok)
    return bias, jnp.asarray(np.stack(masks).astype(np.float32))


def _nb_attn_kernel(q_ref, kp_ref, kc_ref, kn_ref, vp_ref, vc_ref, vn_ref, ck_ref, cv_ref,
                    bias_ref, mask_ref, o_ref):
    q = q_ref[...] * ATT_SCALE
    k_loc = jnp.concatenate([kp_ref[...], kc_ref[...], kn_ref[...]], axis=0)
    v_loc = jnp.concatenate([vp_ref[...], vc_ref[...], vn_ref[...]], axis=0).astype(BF16)
    s_loc = jnp.where(mask_ref[...] > 0.0, _dot3_nt(q, k_loc) + bias_ref[...], NEG)
    s_ctx = _dot3_nt(q, ck_ref[...])
    m = jnp.maximum(jnp.max(s_loc, axis=-1, keepdims=True), jnp.max(s_ctx, axis=-1, keepdims=True))
    e_loc = jnp.exp(s_loc - m)
    e_ctx = jnp.exp(s_ctx - m)
    den = jnp.sum(e_loc, axis=-1, keepdims=True) + jnp.sum(e_ctx, axis=-1, keepdims=True)
    o = _dot(e_loc.astype(BF16), v_loc) + _dot(e_ctx.astype(BF16), cv_ref[...].astype(BF16))
    o_ref[...] = (o / den).astype(o_ref.dtype)


def neighbourhood_attention(u, cols, bsz, n_tok, n_heads, ck, cv, rpb):
    q0, k0, v0 = cols
    rows = n_tok // GRID_W
    assert rows % NB_QROWS == 0 and rows >= NB_ROWS
    nj = rows // NB_QROWS
    past = ck.shape[2]
    bias, mask = nb_tables(rpb, rows)
    prv = lambda j: jnp.maximum(j - 1, 0)
    nxt = lambda j: jnp.minimum(j + 1, nj - 1)
    same = lambda j: j
    blk = lambda f, c0: pl.BlockSpec((NB_QTOK, HEAD_DIM), lambda b, h, j: (b * nj + f(j), c0 + h))
    cache = pl.BlockSpec((None, None, past, HEAD_DIM), lambda b, h, j: (b, h, 0, 0))
    return pl.pallas_call(
        _nb_attn_kernel,
        grid=(bsz, n_heads, nj),
        in_specs=[
            blk(same, q0),
            blk(prv, k0), blk(same, k0), blk(nxt, k0),
            blk(prv, v0), blk(same, v0), blk(nxt, v0),
            cache, cache,
            pl.BlockSpec((None, NB_QTOK, NB_KTOK), lambda b, h, j: (h, 0, 0)),
            pl.BlockSpec((None, NB_QTOK, NB_KTOK), lambda b, h, j: (j, 0, 0)),
        ],
        out_specs=pl.BlockSpec((NB_QTOK, HEAD_DIM), lambda b, h, j: (b * nj + j, h)),
        out_shape=jax.ShapeDtypeStruct((bsz * n_tok, n_heads * HEAD_DIM), BF16),
        compiler_params=_cparams("parallel", "parallel", "parallel"),
        name="neighbourhood_attention",
    )(u, u, u, u, u, u, u, ck, cv, bias, mask)


def _merge_kernel(ya_ref, yb_ref, yc_ref, wa_ref, wb_ref, wc_ref, ga_ref, gb_ref, gc_ref, o_ref):
    m = _sigmoid(ga_ref[...]) * _dot(ya_ref[...], wa_ref[...])
    m = m + _sigmoid(gb_ref[...]) * _dot(yb_ref[...], wb_ref[...])
    m = m + _sigmoid(gc_ref[...]) * _dot(yc_ref[...], wc_ref[...])
    o_ref[...] = m.astype(o_ref.dtype)


def merge_branches(ya, yb, yc, wa, wb, wc, u, gate_col0, tm=512, tn=512):
    n, aw = ya.shape
    bw, cw = wb.shape[0], wc.shape[0]
    d = wa.shape[1]
    tm = _pick(n, (tm, 256, 128))
    tn = _pick(d, (tn, 256, 128))
    assert gate_col0 % tn == 0
    g0 = gate_col0 // tn
    nd = d // tn
    gate = lambda gi: pl.BlockSpec((tm, tn), lambda i, j: (i, g0 + gi * nd + j))
    return pl.pallas_call(
        _merge_kernel,
        grid=(n // tm, nd),
        in_specs=[
            pl.BlockSpec((tm, aw), lambda i, j: (i, 0)),
            pl.BlockSpec((tm, bw), lambda i, j: (i, 0)),
            pl.BlockSpec((tm, cw), lambda i, j: (i, 0)),
            pl.BlockSpec((aw, tn), lambda i, j: (0, j)),
            pl.BlockSpec((bw, tn), lambda i, j: (0, j)),
            pl.BlockSpec((cw, tn), lambda i, j: (0, j)),
            gate(0), gate(1), gate(2),
        ],
        out_specs=pl.BlockSpec((tm, tn), lambda i, j: (i, j)),
        out_shape=jax.ShapeDtypeStruct((n, d), BF16),
        compiler_params=_cparams("parallel", "parallel"),
        name="merge_branches",
    )(ya, yb, yc, wa, wb, wc, u, u, u)


def _post_norm(alpha, x, gate, y, g, b):
    r = alpha * x + gate * y
    return _ln_rows(r) * g + b


def _outproj_ln_kernel(alpha, nk, m_ref, w_ref, x_ref, gate_ref, g_ref, b_ref, o_ref, acc_ref):
    k = pl.program_id(1)

    @pl.when(k == 0)
    def _():
        acc_ref[...] = jnp.zeros_like(acc_ref)

    acc_ref[...] += _dot(m_ref[...], w_ref[...])

    @pl.when(k == nk - 1)
    def _():
        o_ref[...] = _post_norm(alpha, x_ref[...], gate_ref[...], acc_ref[...], g_ref[...], b_ref[...])


def _tile_mod_spec(mod, d, tm, seq_len):
    if mod.shape[0] == 1:
        return pl.BlockSpec((None, 1, d), lambda i, *_: (0, 0, 0))
    return pl.BlockSpec((None, 1, d), lambda i, *_: (i * tm // seq_len, 0, 0))


def outproj_postnorm(m, w_out, x, gate, ln_g, ln_b, alpha, seq_len, tm=256, tk=512):
    n, d = x.shape
    kdim = m.shape[1]
    tm = _pick(seq_len, (tm, 128))
    tk = _pick(kdim, (tk, 256, 128))
    nk = kdim // tk
    vec = pl.BlockSpec((1, d), lambda i, k: (0, 0))
    return pl.pallas_call(
        functools.partial(_outproj_ln_kernel, alpha, nk),
        grid=(n // tm, nk),
        in_specs=[
            pl.BlockSpec((tm, tk), lambda i, k: (i, k)),
            pl.BlockSpec((tk, d), lambda i, k: (k, 0)),
            pl.BlockSpec((tm, d), lambda i, k: (i, 0)),
            _tile_mod_spec(gate, d, tm, seq_len),
            vec, vec,
        ],
        out_specs=pl.BlockSpec((tm, d), lambda i, k: (i, 0)),
        out_shape=jax.ShapeDtypeStruct((n, d), F32),
        scratch_shapes=[pltpu.VMEM((tm, d), F32)],
        compiler_params=_cparams("parallel", "arbitrary"),
        name="outproj_postnorm",
    )(m, w_out, x, gate, ln_g, ln_b)


def _select_kernel(cap, n, aff_ref, mask_ref):
    bits = lax.bitcast_convert_type(aff_ref[...], jnp.int32)
    n_exp = bits.shape[0]
    capf = jnp.float32(cap)

    def count(pred):
        return jnp.sum(pred.astype(F32), axis=1, keepdims=True)

    def value_step(_, carry):
        lo, hi = carry
        mid = lo + ((hi - lo + 1) >> 1)
        ok = count(bits >= mid) >= capf
        return jnp.where(ok, mid, lo), jnp.where(ok, hi, mid - 1)

    lo0 = jnp.zeros((n_exp, 1), jnp.int32)
    hi0 = jnp.full((n_exp, 1), 0x7F800000, jnp.int32)
    thr, _ = lax.fori_loop(0, 32, value_step, (lo0, hi0))
    gt = bits > thr
    eq = bits == thr
    need = capf - count(gt)
    idx = _iota(bits.shape, 1)

    def index_step(_, carry):
        lo, hi = carry
        mid = (lo + hi) >> 1
        ok = count(eq & (idx < mid)) >= need
        return jnp.where(ok, lo, mid), jnp.where(ok, mid, hi)

    _, bound = lax.fori_loop(0, int(np.ceil(np.log2(n))) + 1, index_step,
                             (jnp.zeros((n_exp, 1), jnp.int32), jnp.full((n_exp, 1), n, jnp.int32)))
    mask_ref[...] = (gt | (eq & (idx < bound))).astype(jnp.int32)


def expert_choice_mask(aff_t, cap):
    n_exp, n = aff_t.shape
    return pl.pallas_call(
        functools.partial(_select_kernel, cap, n),
        out_shape=jax.ShapeDtypeStruct((n_exp, n), jnp.int32),
        compiler_params=pltpu.CompilerParams(vmem_limit_bytes=VMEM_LIMIT_BYTES),
        name="expert_choice_mask",
    )(aff_t)


def _visit_list(nvis, first_blk, n_work):
    na, nb = nvis.shape
    flat = nvis.reshape(-1)
    off_end = jnp.cumsum(flat)
    off_start = off_end - flat
    total = off_end[-1]
    w = jnp.minimum(jnp.arange(n_work, dtype=jnp.int32), total - 1)
    idx = jnp.minimum(jnp.searchsorted(off_end, w, side="right"), na * nb - 1).astype(jnp.int32)
    blk = first_blk.reshape(-1)[idx] + (w - off_start[idx])
    valid = (jnp.arange(n_work) < total).astype(jnp.int32)
    return idx // nb, idx % nb, blk.astype(jnp.int32), valid


def routing_plan(mask, cap):
    n_exp, n = mask.shape
    nt = n // TOK_TILE
    nsb = cap // SLOT_BLK
    pos = jnp.cumsum(mask, axis=1) - mask
    posm = jnp.where(mask > 0, pos, -1).astype(jnp.int32)
    cnt = mask.reshape(n_exp, nt, TOK_TILE).sum(-1)
    cend = jnp.cumsum(cnt, axis=1)
    cstart = cend - cnt
    sb_lo = jnp.minimum(cstart // SLOT_BLK, nsb - 1)
    nvis = jnp.where(cnt > 0, (cend - 1) // SLOT_BLK - sb_lo + 1, 0)
    n_work = n_exp * (nsb + nt)
    e, t, sb, valid = _visit_list(nvis, sb_lo, n_work)
    key = e * nsb + sb
    first = jnp.concatenate([jnp.ones((1,), jnp.int32), (key[1:] != key[:-1]).astype(jnp.int32)]) * valid
    dispatch = (e, sb, t, first, valid)
    nvis_t = nvis.T.at[:, 0].max(1)
    t2, e2, sb2, valid2 = _visit_list(nvis_t, sb_lo.T, n_work + nt)
    first2 = jnp.concatenate([jnp.ones((1,), jnp.int32), (t2[1:] != t2[:-1]).astype(jnp.int32)]) * valid2
    nxt_valid = jnp.concatenate([valid2[1:], jnp.zeros((1,), jnp.int32)])
    last2 = jnp.concatenate([(t2[1:] != t2[:-1]).astype(jnp.int32), jnp.ones((1,), jnp.int32)])
    last2 = jnp.maximum(last2, 1 - nxt_valid) * valid2
    combine = (t2, e2, sb2, first2, last2, valid2)
    return posm, dispatch, combine


def _dispatch_kernel(e_ref, sb_ref, t_ref, first_ref, valid_ref, posm_ref, tok_ref, o_ref, acc_ref):
    w = pl.program_id(0)

    @pl.when(first_ref[w] == 1)
    def _():
        acc_ref[...] = jnp.zeros_like(acc_ref)

    @pl.when(valid_ref[w] == 1)
    def _():
        slot = _iota((SLOT_BLK, TOK_TILE), 0) + sb_ref[w] * SLOT_BLK
        onehot = (slot == posm_ref[...]).astype(BF16)
        acc_ref[...] += _dot(onehot, tok_ref[...])

    o_ref[...] = acc_ref[...].astype(o_ref.dtype)


def moe_dispatch(tokens, posm, plan, cap):
    n, d = tokens.shape
    n_exp = posm.shape[0]
    nsb = cap // SLOT_BLK
    n_work = plan[0].shape[0]
    grid_spec = pltpu.PrefetchScalarGridSpec(
        num_scalar_prefetch=5,
        grid=(n_work,),
        in_specs=[
            pl.BlockSpec((None, 1, TOK_TILE), lambda w, e, sb, t, f, v: (e[w], 0, t[w])),
            pl.BlockSpec((TOK_TILE, d), lambda w, e, sb, t, f, v: (t[w], 0)),
        ],
        out_specs=pl.BlockSpec((SLOT_BLK, d), lambda w, e, sb, t, f, v: (e[w] * nsb + sb[w], 0)),
        scratch_shapes=[pltpu.VMEM((SLOT_BLK, d), F32)],
    )
    return pl.pallas_call(
        _dispatch_kernel,
        grid_spec=grid_spec,
        out_shape=jax.ShapeDtypeStruct((n_exp * cap, d), BF16),
        compiler_params=_cparams("arbitrary"),
        name="moe_dispatch",
    )(*plan, posm.reshape(n_exp, 1, n), tokens)


def _ffn_up_kernel(x_ref, w1_ref, w3_ref, o_ref):
    x = x_ref[...]
    a = _dot(x, w1_ref[...])
    o_ref[...] = (a * _sigmoid(a) * _dot(x, w3_ref[...])).astype(o_ref.dtype)


def _ffn_down_kernel(h_ref, w2_ref, o_ref):
    o_ref[...] = _dot(h_ref[...], w2_ref[...]).astype(o_ref.dtype)


def expert_ffn(xe, w1, w3, w2, cap, tn=512):
    n_exp, d, ff = w1.shape
    tm = _pick(cap, (1024, 512, 256, 128))
    nm = cap // tm
    tf = _pick(ff, (tn, 256, 128))
    hid = pl.pallas_call(
        _ffn_up_kernel,
        grid=(n_exp, nm, ff // tf),
        in_specs=[
            pl.BlockSpec((tm, d), lambda e, i, j: (e * nm + i, 0)),
            pl.BlockSpec((None, d, tf), lambda e, i, j: (e, 0, j)),
            pl.BlockSpec((None, d, tf), lambda e, i, j: (e, 0, j)),
        ],
        out_specs=pl.BlockSpec((tm, tf), lambda e, i, j: (e * nm + i, j)),
        out_shape=jax.ShapeDtypeStruct((n_exp * cap, ff), BF16),
        compiler_params=_cparams("parallel", "parallel", "parallel"),
        name="expert_ffn_up",
    )(xe, w1, w3)
    td = _pick(d, (tn, 256, 128))
    return pl.pallas_call(
        _ffn_down_kernel,
        grid=(n_exp, nm, d // td),
        in_specs=[
            pl.BlockSpec((tm, ff), lambda e, i, j: (e * nm + i, 0)),
            pl.BlockSpec((None, ff, td), lambda e, i, j: (e, 0, j)),
        ],
        out_specs=pl.BlockSpec((tm, td), lambda e, i, j: (e * nm + i, j)),
        out_shape=jax.ShapeDtypeStruct((n_exp * cap, d), BF16),
        compiler_params=_cparams("parallel", "parallel", "parallel"),
        name="expert_ffn_down",
    )(hid, w2)


def _combine_kernel(alpha, t_ref, e_ref, sb_ref, first_ref, last_ref, valid_ref,
                    posm_ref, aff_ref, ye_ref, x_ref, gate_ref, g_ref, b_ref, o_ref, acc_ref):
    w = pl.program_id(0)

    @pl.when(first_ref[w] == 1)
    def _():
        acc_ref[...] = jnp.zeros_like(acc_ref)

    @pl.when(valid_ref[w] == 1)
    def _():
        e = e_ref[w]
        posm = posm_ref[...].astype(F32)
        sel_p = _iota(posm.shape, 1) == e
        col = jnp.sum(jnp.where(sel_p, posm, 0.0), axis=1, keepdims=True)
        aff = aff_ref[...]
        gval = jnp.sum(jnp.where(_iota(aff.shape, 1) == e, aff, 0.0), axis=1, keepdims=True)
        slot = (_iota((TOK_TILE, SLOT_BLK), 1) + sb_ref[w] * SLOT_BLK).astype(F32)
        onehot = (slot == col).astype(BF16)
        acc_ref[...] += gval * _dot(onehot, ye_ref[...])

    @pl.when(last_ref[w] == 1)
    def _():
        o_ref[...] = _post_norm(alpha, x_ref[...], gate_ref[...], acc_ref[...], g_ref[...], b_ref[...])


def moe_combine_postnorm(ye, posm_t, aff, plan, cap, x, gate, ln_g, ln_b, alpha, seq_len):
    n, d = x.shape
    n_exp = posm_t.shape[1]
    nsb = cap // SLOT_BLK
    n_work = plan[0].shape[0]
    assert seq_len % TOK_TILE == 0
    tile = lambda w, t, *_: (t[w], 0)
    if gate.shape[0] == 1:
        gate_map = lambda w, t, *_: (0, 0, 0)
    else:
        gate_map = lambda w, t, *_: (t[w] * TOK_TILE // seq_len, 0, 0)
    vec = pl.BlockSpec((1, d), lambda w, *_: (0, 0))
    grid_spec = pltpu.PrefetchScalarGridSpec(
        num_scalar_prefetch=6,
        grid=(n_work,),
        in_specs=[
            pl.BlockSpec((TOK_TILE, n_exp), tile),
            pl.BlockSpec((TOK_TILE, LANE), tile),
            pl.BlockSpec((SLOT_BLK, d), lambda w, t, e, sb, *_: (e[w] * nsb + sb[w], 0)),
            pl.BlockSpec((TOK_TILE, d), tile),
            pl.BlockSpec((None, 1, d), gate_map),
            vec, vec,
        ],
        out_specs=pl.BlockSpec((TOK_TILE, d), tile),
        scratch_shapes=[pltpu.VMEM((TOK_TILE, d), F32)],
    )
    return pl.pallas_call(
        functools.partial(_combine_kernel, alpha),
        grid_spec=grid_spec,
        out_shape=jax.ShapeDtypeStruct((n, d), F32),
        compiler_params=_cparams("arbitrary"),
        name="moe_combine_postnorm",
    )(*plan, posm_t, aff, ye, x, gate, ln_g, ln_b)


def ec_moe_postnorm(x, mods_shift, mods_scale, gate, router_w, w1, w3, w2, ln_g, ln_b, alpha):
    bsz, t, d = x.shape
    n = bsz * t
    n_exp = router_w.shape[1]
    cap = max(1, CAP_FACTOR * n // n_exp)
    assert cap % SLOT_BLK == 0 and n % TOK_TILE == 0
    h, aff = ln_modulate_router(x, mods_shift, mods_scale, router_w)
    h = h.reshape(n, d)
    aff = aff.reshape(n, LANE)
    mask = expert_choice_mask(aff[:, :n_exp].T, cap)
    posm, dispatch, combine = routing_plan(mask, cap)
    xe = moe_dispatch(h, posm, dispatch, cap)
    ye = expert_ffn(xe, w1, w3, w2, cap)
    out = moe_combine_postnorm(ye, posm.T, aff, combine, cap, x.reshape(n, d), gate, ln_g, ln_b,
                               alpha, t)
    return out.reshape(bsz, t, d)


def prep_rwkv_params(rw, aw):
    a_cols = rw["mu"].shape[0]
    ap = round_up(a_cols, LANE)
    gp = ap - 3 * aw - 2 * LANE
    assert rw["w2"].shape[:2] == (2, LANE // 2) and rw["a2"].shape[:2] == (2, LANE // 2)
    assert gp >= rw["g2"].shape[0] and aw % LANE == 0
    row = lambda a: a.reshape(1, aw)
    return dict(
        mu=jnp.pad(rw["mu"], (0, ap - a_cols)).reshape(1, ap),
        w0=rw["w0"], w2=rw["w2"].reshape(LANE, aw), a0=rw["a0"], a2=rw["a2"].reshape(LANE, aw),
        g2=jnp.pad(rw["g2"], ((0, gp - rw["g2"].shape[0]), (0, 0))),
        k_k=row(rw["k_k"]), k_a=row(rw["k_a"]), r_k=row(rw["r_k"]),
        ln_w=row(rw["ln_w"]), ln_b=row(rw["ln_b"]), ap=ap, aw=aw)


def states_to_pairs(s):
    bsz, h = s.shape[:2]
    st = jnp.swapaxes(s, -1, -2).reshape(bsz, h // 2, 2, A_HEAD, A_HEAD)
    z = jnp.zeros_like(st[:, :, 0])
    top = jnp.concatenate([st[:, :, 0], z], axis=-1)
    bot = jnp.concatenate([z, st[:, :, 1]], axis=-1)
    return jnp.concatenate([top, bot], axis=-2)


def pairs_to_states(sp):
    bsz = sp.shape[0]
    st = jnp.stack([sp[:, :, :A_HEAD, :A_HEAD], sp[:, :, A_HEAD:, A_HEAD:]], axis=2)
    return jnp.swapaxes(st.reshape(bsz, -1, A_HEAD, A_HEAD), -1, -2)


def rwkv_branch(u, bsz, t, rwp, s0f, s0b):
    aw, ap = rwp["aw"], rwp["ap"]
    n = bsz * t
    r_, v_, kk, lwf, lwb, bf, bb, kdf, kdb, bonus, g = rwkv_prep(u, t, rwp, aw, ap)
    sh = lambda a: a.reshape(bsz, t, aw)
    yf, sf = rwkv_scan(sh(lwf), sh(kk), sh(bf), sh(kdf), sh(r_), sh(v_), states_to_pairs(s0f), False)
    yb, sb = rwkv_scan(sh(lwb), sh(kk), sh(bb), sh(kdb), sh(r_), sh(v_), states_to_pairs(s0b), True)
    ya = rwkv_post(yf.reshape(n, aw), yb.reshape(n, aw), bonus, g, rwp["ln_w"], rwp["ln_b"])
    return ya, pairs_to_states(sf), pairs_to_states(sb)


def _mod_rows(m):
    return [m[:, i][:, None, :] for i in range(6)]


def kernel(x_prompt, x_sample, cache_win_k, cache_win_v, cache_nb_k, cache_nb_v, state_rwkv_fwd, state_rwkv_bwd, c, c_ctx, ada_w, ada_b, w_in, rwkv_mu, rwkv_w0, rwkv_w2, rwkv_a0, rwkv_a2, rwkv_g2, rwkv_kk, rwkv_ka, rwkv_rk, rwkv_lnx_w, rwkv_lnx_b, win_sink, nb_rpb, w_br_a, w_br_b, w_br_c, w_out, ln1_g, ln1_b, ln2_g, ln2_b, router_w, exp_w1, exp_w3, exp_w2):
    depth, d, in_w = w_in.shape
    bsz, seq, _ = x_prompt.shape
    dbsz, dseq, _ = x_sample.shape
    aw = rwkv_w0.shape[-1]
    a_heads = aw // A_HEAD
    a_cols = rwkv_mu.shape[-1]
    ap = round_up(a_cols, LANE)
    n_b = win_sink.shape[-1]
    n_kv = cache_win_k.shape[3]
    group = n_b // n_kv
    n_c = nb_rpb.shape[1]
    bw, cw = n_b * HEAD_DIM, n_c * HEAD_DIM
    alpha = float((2 * depth) ** 0.25)
    assert in_w == a_cols + bw + 2 * n_kv * HEAD_DIM + 3 * cw + 3 * d

    qb0 = ap // LANE
    kb0 = qb0 + n_b
    vb0 = kb0 + n_kv
    qn0 = vb0 + n_kv
    kn0 = qn0 + n_c
    vn0 = kn0 + n_c
    gate_col0 = (vn0 + n_c) * LANE

    rows = 1 + dbsz
    cvecs = jnp.pad(jnp.concatenate([c_ctx[None], c], axis=0), ((0, round_up(rows, 16) - rows), (0, 0)))
    mods = ada_mods(cvecs, ada_w, ada_b).reshape(depth, -1, 6, d)

    xp, xs = x_prompt, x_sample
    np_tok, ns_tok = bsz * seq, dbsz * dseq
    win_k, win_v, nb_k, nb_v, st_f, st_b = [], [], [], [], [], []
    zero_state = jnp.zeros((bsz, a_heads, A_HEAD, A_HEAD), F32)
    for l in range(depth):
        w_cat = jnp.concatenate(
            [w_in[l, :, :a_cols].astype(BF16), jnp.zeros((d, ap - a_cols), BF16),
             w_in[l, :, a_cols:].astype(BF16)], axis=1)
        rwp = prep_rwkv_params(
            {"mu": rwkv_mu[l], "w0": rwkv_w0[l], "w2": rwkv_w2[l], "a0": rwkv_a0[l], "a2": rwkv_a2[l],
             "g2": rwkv_g2[l], "k_k": rwkv_kk[l], "k_a": rwkv_ka[l], "r_k": rwkv_rk[l],
             "ln_w": rwkv_lnx_w[l], "ln_b": rwkv_lnx_b[l]}, aw)
        wa, wb, wc = w_br_a[l].astype(BF16), w_br_b[l].astype(BF16), w_br_c[l].astype(BF16)
        wo = w_out[l].astype(BF16)
        w1, w3, w2 = exp_w1[l].astype(BF16), exp_w3[l].astype(BF16), exp_w2[l].astype(BF16)
        g1, b1 = ln1_g[l].reshape(1, d), ln1_b[l].reshape(1, d)
        g2, b2 = ln2_g[l].reshape(1, d), ln2_b[l].reshape(1, d)

        sh1, sc1, gt1, sh2, sc2, gt2 = _mod_rows(mods[l, 0:1])
        h = ln_modulate(xp, sh1, sc1).reshape(np_tok, d)
        u = matmul(h, w_cat, name="in_proj")
        ya, s_f, s_b = rwkv_branch(u, bsz, seq, rwp, zero_state, zero_state)
        ybc = ctx_attention(u, bsz, seq, (qb0, kb0, vb0, qn0, kn0, vn0), win_sink[l], n_b, group, n_c)
        m = merge_branches(ya, ybc[:, :bw], ybc[:, bw:], wa, wb, wc, u, gate_col0)
        x1 = outproj_postnorm(m, wo, xp.reshape(np_tok, d), gt1, g1, b1, alpha, seq)
        xp = ec_moe_postnorm(x1.reshape(bsz, seq, d), sh2, sc2, gt2, router_w[l], w1, w3, w2, g2, b2, alpha)
        cols = lambda c0, nh: u[:, c0 * LANE:(c0 + nh) * LANE].reshape(bsz, seq, nh, HEAD_DIM)
        win_k.append(cols(kb0, n_kv))
        win_v.append(cols(vb0, n_kv))
        nb_k.append(cols(kn0, n_c))
        nb_v.append(cols(vn0, n_c))
        st_f.append(s_f)
        st_b.append(s_b)

        sh1, sc1, gt1, sh2, sc2, gt2 = _mod_rows(mods[l, 1:1 + dbsz])
        h = ln_modulate(xs, sh1, sc1).reshape(ns_tok, d)
        u = matmul(h, w_cat, name="in_proj")
        ya, _, _ = rwkv_branch(u, dbsz, dseq, rwp, state_rwkv_fwd[:, l], state_rwkv_bwd[:, l])
        qk_rot = rope(u, dbsz, dseq, qb0, n_b + n_kv)
        heads_first = lambda a: jnp.swapaxes(a[:, l], 1, 2)
        yb = window_attention(qk_rot, u, vb0, dbsz, dseq, n_kv, group,
                              heads_first(cache_win_k), heads_first(cache_win_v), win_sink[l])
        yc = neighbourhood_attention(u, (qn0, kn0, vn0), dbsz, dseq, n_c,
                                     heads_first(cache_nb_k), heads_first(cache_nb_v), nb_rpb[l])
        m = merge_branches(ya, yb, yc, wa, wb, wc, u, gate_col0)
        x1 = outproj_postnorm(m, wo, xs.reshape(ns_tok, d), gt1, g1, b1, alpha, dseq)
        xs = ec_moe_postnorm(x1.reshape(dbsz, dseq, d), sh2, sc2, gt2, router_w[l], w1, w3, w2, g2, b2, alpha)

    stack = lambda xs_: jnp.stack(xs_, axis=1)
    return (xp, xs, stack(win_k), stack(win_v), stack(nb_k), stack(nb_v), stack(st_f), stack(st_b))
```

```python
import functools

import numpy as np
import jax
import jax.numpy as jnp
from jax import lax
from jax.experimental import pallas as pl
from jax.experimental.pallas import tpu as pltpu

F32 = jnp.float32
BF16 = jnp.bfloat16

HEAD_DIM = 128
A_HEAD = 64
GRID_W = 64
WIN = 128
QBLK = 128
NB_ROWS = 8
NB_COLS = 16
CAP_FACTOR = 2
ROPE_THETA = 10000.0
LN_EPS = 1e-6
GN_EPS = 64e-5
NEG = -1e30

LANE = 128
SUBLANE = 8
VMEM_LIMIT_BYTES = 56 * 1024 * 1024

SCAN_CHUNK = 64
SLOT_BLK = 128
TOK_TILE = 256


def _cparams(*sem):
    return pltpu.CompilerParams(dimension_semantics=sem, vmem_limit_bytes=VMEM_LIMIT_BYTES)


def _dot(a, b):
    return jnp.dot(a, b, preferred_element_type=F32)


def _dot_nt(a, b):
    return lax.dot_general(a, b, (((1,), (1,)), ((), ())), preferred_element_type=F32)


def _split2(x):
    hi = x.astype(BF16)
    lo = (x - hi.astype(F32)).astype(BF16)
    return hi, lo


def _split3(x):
    hi = x.astype(BF16)
    r1 = x - hi.astype(F32)
    mid = r1.astype(BF16)
    lo = (r1 - mid.astype(F32)).astype(BF16)
    return hi, mid, lo


def _dot3(a, b):
    ah, al = _split2(a)
    bh, bl = _split2(b)
    return _dot(ah, bh) + (_dot(ah, bl) + _dot(al, bh))


def _dot3_nt(a, b):
    ah, al = _split2(a)
    bh, bl = _split2(b)
    return _dot_nt(ah, bh) + (_dot_nt(ah, bl) + _dot_nt(al, bh))


def _dot_exact_lhs(a_bf16, b):
    bh, bm, bl = _split3(b)
    return _dot(a_bf16, bh) + (_dot(a_bf16, bm) + _dot(a_bf16, bl))


def _sigmoid(x):
    return 1.0 / (1.0 + jnp.exp(-x))


def _iota(shape, dim):
    return lax.broadcasted_iota(jnp.int32, shape, dim)


def _ada_kernel(c_ref, w_ref, b_ref, o_ref):
    c = c_ref[...]
    a = c * _sigmoid(c)
    o_ref[...] = _dot3(a, w_ref[...]) + b_ref[...]


def ada_mods(cvecs, ada_w, ada_b, tn=512):
    depth, d, n6 = ada_w.shape
    rows = cvecs.shape[0]
    return pl.pallas_call(
        _ada_kernel,
        grid=(depth, n6 // tn),
        in_specs=[
            pl.BlockSpec((rows, d), lambda l, j: (0, 0)),
            pl.BlockSpec((None, d, tn), lambda l, j: (l, 0, j)),
            pl.BlockSpec((None, 1, tn), lambda l, j: (l, 0, j)),
        ],
        out_specs=pl.BlockSpec((None, rows, tn), lambda l, j: (l, 0, j)),
        out_shape=jax.ShapeDtypeStruct((depth, rows, n6), F32),
        compiler_params=_cparams("parallel", "parallel"),
        name="ada_mods",
    )(cvecs, ada_w, ada_b.reshape(depth, 1, n6))


def _ln_rows(x):
    mu = jnp.mean(x, axis=-1, keepdims=True)
    xc = x - mu
    var = jnp.mean(xc * xc, axis=-1, keepdims=True)
    return xc * lax.rsqrt(var + LN_EPS)


def _lnmod_kernel(x_ref, sh_ref, sc_ref, o_ref):
    y = _ln_rows(x_ref[...])
    o_ref[...] = (y * (1.0 + sc_ref[...]) + sh_ref[...]).astype(o_ref.dtype)


def _mod_spec(mod, d):
    if mod.shape[0] == 1:
        return pl.BlockSpec((None, 1, d), lambda b, i: (0, 0, 0))
    return pl.BlockSpec((None, 1, d), lambda b, i: (b, 0, 0))


def ln_modulate(x, shift, scale, tt=256):
    bsz, t, d = x.shape
    return pl.pallas_call(
        _lnmod_kernel,
        grid=(bsz, t // tt),
        in_specs=[
            pl.BlockSpec((None, tt, d), lambda b, i: (b, i, 0)),
            _mod_spec(shift, d),
            _mod_spec(scale, d),
        ],
        out_specs=pl.BlockSpec((None, tt, d), lambda b, i: (b, i, 0)),
        out_shape=jax.ShapeDtypeStruct((bsz, t, d), BF16),
        compiler_params=_cparams("parallel", "parallel"),
        name="ln_modulate",
    )(x, shift, scale)


def _lnmod_router_kernel(n_exp, x_ref, sh_ref, sc_ref, rw_ref, h_ref, aff_ref):
    y = _ln_rows(x_ref[...])
    h = y * (1.0 + sc_ref[...]) + sh_ref[...]
    h_ref[...] = h.astype(h_ref.dtype)
    logits = _dot3(h, rw_ref[...])
    lane = _iota(logits.shape, 1)
    logits = jnp.where(lane < n_exp, logits, NEG)
    m = jnp.max(logits, axis=-1, keepdims=True)
    e = jnp.exp(logits - m)
    aff_ref[...] = e / jnp.sum(e, axis=-1, keepdims=True)


def ln_modulate_router(x, shift, scale, router_w, tt=256):
    bsz, t, d = x.shape
    n_exp = router_w.shape[1]
    rw = jnp.pad(router_w, ((0, 0), (0, LANE - n_exp)))
    return pl.pallas_call(
        functools.partial(_lnmod_router_kernel, n_exp),
        grid=(bsz, t // tt),
        in_specs=[
            pl.BlockSpec((None, tt, d), lambda b, i: (b, i, 0)),
            _mod_spec(shift, d),
            _mod_spec(scale, d),
            pl.BlockSpec((d, LANE), lambda b, i: (0, 0)),
        ],
        out_specs=[
            pl.BlockSpec((None, tt, d), lambda b, i: (b, i, 0)),
            pl.BlockSpec((None, tt, LANE), lambda b, i: (b, i, 0)),
        ],
        out_shape=[
            jax.ShapeDtypeStruct((bsz, t, d), BF16),
            jax.ShapeDtypeStruct((bsz, t, LANE), F32),
        ],
        compiler_params=_cparams("parallel", "parallel"),
        name="ln_modulate_router",
    )(x, shift, scale, rw)


def _mm_kernel(a_ref, b_ref, o_ref):
    o_ref[...] = _dot(a_ref[...], b_ref[...]).astype(o_ref.dtype)


def _pick(n, pref):
    for c in pref:
        if n % c == 0:
            return c
    return n


def matmul(a, b, out_dtype=F32, tm=None, tn=None, name="matmul"):
    m, k = a.shape
    _, n = b.shape
    tm = tm or _pick(m, (1024, 512, 256, 128))
    tn = tn or _pick(n, (768, 512, 256, 128))
    return pl.pallas_call(
        _mm_kernel,
        grid=(m // tm, n // tn),
        in_specs=[
            pl.BlockSpec((tm, k), lambda i, j: (i, 0)),
            pl.BlockSpec((k, tn), lambda i, j: (0, j)),
        ],
        out_specs=pl.BlockSpec((tm, tn), lambda i, j: (i, j)),
        out_shape=jax.ShapeDtypeStruct((m, n), out_dtype),
        compiler_params=_cparams("parallel", "parallel"),
        name=name,
    )(a, b)


HEAD_SHIFT = 6
assert (1 << HEAD_SHIFT) == A_HEAD and 2 * A_HEAD == LANE and SCAN_CHUNK == A_HEAD


def _head_block_ones():
    i = _iota((LANE, LANE), 0) >> HEAD_SHIFT
    j = _iota((LANE, LANE), 1) >> HEAD_SHIFT
    return (i == j).astype(BF16)


def _seg_sum(x):
    bd = _head_block_ones()
    outs = []
    for c in range(x.shape[1] // LANE):
        hi, mid, lo = _split3(x[:, c * LANE:(c + 1) * LANE])
        outs.append(_dot(hi, bd) + (_dot(mid, bd) + _dot(lo, bd)))
    return jnp.concatenate(outs, axis=1)


def _rwkv_prep_kernel(aw, gp, tiles_per_seq,
                      x_ref, xp_ref, xn_ref, mu_ref, w0_ref, w2_ref, a0_ref, a2_ref, g2_ref,
                      kkp_ref, kap_ref, rkp_ref,
                      r_o, v_o, kk_o, lwf_o, lwb_o, bf_o, bb_o, kdf_o, kdb_o, bonus_o, g_o):
    i = pl.program_id(0)
    x = x_ref[...]
    tt = x.shape[0]
    row = _iota(x.shape, 0)
    pos = i % tiles_per_seq
    prev_row = jnp.where(pos == 0, 0.0, xp_ref[SUBLANE - 1:SUBLANE, :])
    next_row = jnp.where(pos == tiles_per_seq - 1, 0.0, xn_ref[0:1, :])
    prev = jnp.where(row == 0, prev_row, pltpu.roll(x, 1, 0))
    nxt = jnp.where(row == tt - 1, next_row, pltpu.roll(x, tt - 1, 0))
    xs = x + mu_ref[...] * (0.5 * (prev + nxt) - x)

    r = xs[:, 0:aw]
    k = xs[:, aw:2 * aw]
    v = xs[:, 2 * aw:3 * aw]
    o = 3 * aw
    wlo = jnp.tanh(xs[:, o:o + LANE])
    alo = xs[:, o + LANE:o + 2 * LANE]
    glo = _sigmoid(xs[:, o + 2 * LANE:o + 2 * LANE + gp])
    lane = _iota((1, LANE), 1)
    exp_mhalf = float(np.exp(-0.5))

    r_o[...] = r
    v_o[...] = v
    g_o[...] = _dot3(glo, g2_ref[...])

    kk = k * kkp_ref[...]
    nrm = jnp.sqrt(_seg_sum(kk * kk))
    kk = kk / jnp.maximum(nrm, 1e-12)
    kk_o[...] = kk
    bonus_o[...] = _seg_sum(r * k * rkp_ref[...]) * v

    for d, (lw_o, b_o, kd_o) in enumerate(((lwf_o, bf_o, kdf_o), (lwb_o, bb_o, kdb_o))):
        sel = ((lane >> HEAD_SHIFT) == d).astype(F32)
        wl = w0_ref[d:d + 1, :] + _dot3(wlo * sel, w2_ref[...])
        lw_o[...] = -_sigmoid(wl) * exp_mhalf
        a = _sigmoid(a0_ref[d:d + 1, :] + _dot3(alo * sel, a2_ref[...]))
        b_o[...] = kk * a
        kd_o[...] = k * (1.0 + (a - 1.0) * kap_ref[...])


def rwkv_prep(u, seq_len, rw, aw, ap, tt=128):
    n = u.shape[0]
    gp = ap - 3 * aw - 2 * LANE
    n8 = n // SUBLANE
    tpb = tt // SUBLANE
    full = lambda shape: pl.BlockSpec(shape, lambda i: (0,) * len(shape))
    tok = pl.BlockSpec((tt, aw), lambda i: (i, 0))
    outs = pl.pallas_call(
        functools.partial(_rwkv_prep_kernel, aw, gp, seq_len // tt),
        grid=(n // tt,),
        in_specs=[
            pl.BlockSpec((tt, ap), lambda i: (i, 0)),
            pl.BlockSpec((SUBLANE, ap), lambda i: (jnp.maximum(i * tpb - 1, 0), 0)),
            pl.BlockSpec((SUBLANE, ap), lambda i: (jnp.minimum((i + 1) * tpb, n8 - 1), 0)),
            full((1, ap)), full((2, aw)), full((LANE, aw)), full((2, aw)), full((LANE, aw)),
            full((gp, aw)), full((1, aw)), full((1, aw)), full((1, aw)),
        ],
        out_specs=[tok] * 11,
        out_shape=[jax.ShapeDtypeStruct((n, aw), F32)] * 11,
        compiler_params=_cparams("parallel"),
        name="rwkv_prep",
    )(u, u, u, rw["mu"], rw["w0"], rw["w2"], rw["a0"], rw["a2"], rw["g2"],
      rw["k_k"], rw["k_a"], rw["r_k"])
    return outs


def _scan_chunks(rev, toks, sts, consts):
    tri, strict, incl, blk, eye, head0, t2, s2 = consts
    c = SCAN_CHUNK
    ident = (t2 == s2).astype(F32)
    zero = jnp.zeros((), BF16)
    cat = lambda *xs: jnp.concatenate(xs, axis=0)

    def each(f, *lists):
        return [f(*args) for args in zip(*lists)]

    def bd(x):
        xb = x.astype(BF16)
        return jnp.where(blk, cat(xb, xb), zero)

    lw, kk, bb, kd, r, v = (list(t) for t in zip(*toks))
    big_l = each(lambda a: _dot_exact_lhs(tri, a), lw)
    l_tot = each(lambda a: a[0:1, :] if rev else a[c - 1:c, :], big_l)
    lhs = each(lambda k_, r_, l_, w_: cat(-k_ * jnp.exp(l_ - w_), r_ * jnp.exp(l_)).astype(BF16),
               kk, r, big_l, lw)
    gi = each(lambda l_: jnp.exp(-l_), big_l)
    bt = each(lambda b_, g_: (b_ * g_).astype(BF16), bb, gi)
    kt = each(lambda k_, g_: (k_ * g_).astype(BF16), kd, gi)
    rhs = each(lambda b_, k_: cat(jnp.where(head0, b_, zero), jnp.where(head0, zero, b_),
                                  jnp.where(head0, k_, zero), jnp.where(head0, zero, k_)), bt, kt)
    p = each(_dot_nt, lhs, rhs)
    nab = each(lambda p_: jnp.where(strict, p_[0:c, 0:LANE], 0.0).astype(BF16), p)
    nrb = each(lambda p_: jnp.where(incl, p_[c:2 * c, 0:LANE], 0.0).astype(BF16), p)
    nkk = each(lambda p_: cat(jnp.where(strict, p_[0:c, LANE:2 * LANE], 0.0),
                              jnp.where(incl, p_[c:2 * c, LANE:2 * LANE], 0.0)).astype(BF16), p)
    z = each(lambda l_, s_: _dot(l_, s_.astype(BF16)), lhs, sts)
    w = each(lambda n_, v_: _dot(n_, bd(v_)), nkk, v)
    x = each(lambda z_, w_: z_[0:c] + w_[0:c], z, w)
    blk8 = (t2 >> 3) == (s2 >> 3)
    n0 = each(lambda n_: jnp.where(blk8, n_, zero), nab)
    inv = each(lambda n_: ident + n_.astype(F32), n0)
    pw = each(lambda n_: _dot(n_, bd(n_)), n0)
    inv = each(lambda i_, p_: i_ + _dot(i_.astype(BF16), bd(p_)), inv, pw)
    pw = each(lambda p_: _dot(p_.astype(BF16), bd(p_)), pw)
    inv = each(lambda i_, p_: i_ + _dot(i_.astype(BF16), bd(p_)), inv, pw)
    for lb in range(3, HEAD_SHIFT):
        new = ((t2 >> (lb + 1)) == (s2 >> (lb + 1))) & ((t2 >> lb) != (s2 >> lb))
        half = each(lambda i_, n_: _dot(i_.astype(BF16), bd(jnp.where(new, n_, zero))), inv, nab)
        inv = each(lambda i_, h_: i_ + _dot(h_.astype(BF16), bd(i_)), inv, half)
    x = each(lambda i_, x_: _dot(i_.astype(BF16), bd(x_)), inv, x)
    y = each(lambda z_, w_, n_, x_: z_[c:2 * c] + w_[c:2 * c] + _dot(n_, bd(x_)), z, w, nrb, x)
    gr = each(lambda t_, l_: jnp.exp(t_ - l_), l_tot, big_l)
    upd = each(lambda b_, k_, g_, x_, v_: _dot3(cat(b_ * g_, k_ * g_).T, cat(x_, v_)), bb, kd, gr, x, v)
    g_col = each(lambda t_: jnp.sum(jnp.where(eye, jnp.exp(t_), 0.0), axis=1, keepdims=True), l_tot)
    st_new = each(lambda s_, g_, u_: s_ * g_ + jnp.where(blk, u_, 0.0), sts, g_col, upd)
    return y, st_new


def _scan_kernel(rev, npar, n_chunks, lw_ref, kk_ref, bb_ref, kd_ref, r_ref, v_ref, s0_ref,
                 y_ref, st_out_ref, st_scr):
    ci = pl.program_id(2)

    @pl.when(ci == 0)
    def _():
        st_scr[...] = s0_ref[...]

    c = SCAN_CHUNK
    ti = _iota((c, c), 0)
    si = _iota((c, c), 1)
    tri = ((si >= ti) if rev else (si <= ti)).astype(BF16)
    t2 = _iota((c, LANE), 0)
    s2 = _iota((c, LANE), 1) & (c - 1)
    strict = (s2 > t2) if rev else (s2 < t2)
    incl = (s2 >= t2) if rev else (s2 <= t2)
    bi = _iota((LANE, LANE), 0)
    bj = _iota((LANE, LANE), 1)
    blk = (bi >> HEAD_SHIFT) == (bj >> HEAD_SHIFT)
    eye = bi == bj
    head0 = _iota((c, LANE), 1) < A_HEAD
    consts = (tri, strict, incl, blk, eye, head0, t2, s2)
    lanes = [slice(p * LANE, (p + 1) * LANE) for p in range(npar)]
    toks = [tuple(ref[:, sl] for ref in (lw_ref, kk_ref, bb_ref, kd_ref, r_ref, v_ref)) for sl in lanes]
    ys, sts = _scan_chunks(rev, toks, [st_scr[p] for p in range(npar)], consts)
    for p, sl in enumerate(lanes):
        y_ref[:, sl] = ys[p]
        st_scr[p] = sts[p]

    @pl.when(ci == n_chunks - 1)
    def _():
        st_out_ref[...] = st_scr[...]


def rwkv_scan(lw, kk, bb, kd, r, v, s0, rev, npar=6):
    bsz, t, aw = lw.shape
    n_pairs = aw // LANE
    npar = max(p for p in range(1, npar + 1) if n_pairs % p == 0)
    n_chunks = t // SCAN_CHUNK
    if rev:
        tmap = lambda b, g, c: (b, n_chunks - 1 - c, g)
    else:
        tmap = lambda b, g, c: (b, c, g)
    tok = pl.BlockSpec((None, SCAN_CHUNK, npar * LANE), tmap)
    st = pl.BlockSpec((None, npar, LANE, LANE), lambda b, g, c: (b, g, 0, 0))
    return pl.pallas_call(
        functools.partial(_scan_kernel, rev, npar, n_chunks),
        grid=(bsz, n_pairs // npar, n_chunks),
        in_specs=[tok] * 6 + [st],
        out_specs=[tok, st],
        out_shape=[jax.ShapeDtypeStruct((bsz, t, aw), F32),
                   jax.ShapeDtypeStruct(s0.shape, F32)],
        scratch_shapes=[pltpu.VMEM((npar, LANE, LANE), F32)],
        compiler_params=_cparams("parallel", "parallel", "arbitrary"),
        name="rwkv_scan_bwd" if rev else "rwkv_scan_fwd",
    )(lw, kk, bb, kd, r, v, s0)


def _rwkv_post_kernel(yf_ref, yb_ref, bonus_ref, g_ref, lnw_ref, lnb_ref, o_ref):
    y = yf_ref[...] + yb_ref[...]
    inv = 1.0 / A_HEAD
    mu = _seg_sum(y) * inv
    yc = y - mu
    var = _seg_sum(yc * yc) * inv
    yn = yc * lax.rsqrt(var + GN_EPS) * lnw_ref[...] + lnb_ref[...]
    o_ref[...] = ((yn + bonus_ref[...]) * g_ref[...]).astype(o_ref.dtype)


def rwkv_post(yf, yb, bonus, g, ln_w, ln_b, tt=256):
    n, aw = yf.shape
    tok = pl.BlockSpec((tt, aw), lambda i: (i, 0))
    par = pl.BlockSpec((1, aw), lambda i: (0, 0))
    return pl.pallas_call(
        _rwkv_post_kernel,
        grid=(n // tt,),
        in_specs=[tok, tok, tok, tok, par, par],
        out_specs=tok,
        out_shape=jax.ShapeDtypeStruct((n, aw), BF16),
        compiler_params=_cparams("parallel"),
        name="rwkv_post",
    )(yf, yb, bonus, g, ln_w, ln_b)


def round_up(x, m):
    return (x + m - 1) // m * m


ATT_SCALE = HEAD_DIM ** -0.5


def _ctx_attn_kernel(q_ref, k_ref, v_ref, sink_ref, o_ref):
    s = _dot3_nt(q_ref[...] * ATT_SCALE, k_ref[...])
    sk = sink_ref[0:1, 0:1]
    m = jnp.maximum(jnp.max(s, axis=-1, keepdims=True), sk)
    e = jnp.exp(s - m)
    den = jnp.sum(e, axis=-1, keepdims=True) + jnp.exp(sk - m)
    o = _dot(e.astype(BF16), v_ref[...].astype(BF16))
    o_ref[...] = (o / den).astype(o_ref.dtype)


def ctx_attention(u, bsz, t, q0, k0, v0, n_heads, group, sink):
    sink = jnp.full((n_heads,), NEG, F32) if sink is None else sink.astype(F32)
    sink = jnp.broadcast_to(sink[:, None, None], (n_heads, 1, LANE))
    return pl.pallas_call(
        _ctx_attn_kernel,
        grid=(bsz, n_heads),
        in_specs=[
            pl.BlockSpec((t, HEAD_DIM), lambda b, h: (b, q0 + h)),
            pl.BlockSpec((t, HEAD_DIM), lambda b, h: (b, k0 + h // group)),
            pl.BlockSpec((t, HEAD_DIM), lambda b, h: (b, v0 + h // group)),
            pl.BlockSpec((None, 1, LANE), lambda b, h: (h, 0, 0)),
        ],
        out_specs=pl.BlockSpec((t, HEAD_DIM), lambda b, h: (b, h)),
        out_shape=jax.ShapeDtypeStruct((bsz * t, n_heads * HEAD_DIM), BF16),
        compiler_params=_cparams("parallel", "parallel"),
        name="ctx_attention",
    )(u, u, u, sink)


def rope_tables(n_tok):
    half = HEAD_DIM // 2
    quarter = half // 2
    tok = jnp.arange(n_tok)
    row = (tok // GRID_W).astype(F32)
    col = (tok % GRID_W).astype(F32)
    inv = ROPE_THETA ** (-jnp.arange(quarter, dtype=F32) / quarter)
    ang_r = row[:, None] * inv[None]
    ang_c = col[:, None] * inv[None]
    cos = jnp.concatenate([jnp.cos(ang_r)] * 2 + [jnp.cos(ang_c)] * 2, axis=1)
    sr, sc = jnp.sin(ang_r), jnp.sin(ang_c)
    z = jnp.zeros_like(sr)
    sin_a = jnp.concatenate([-sr, z, -sc, z], axis=1)
    sin_b = jnp.concatenate([z, sr, z, sc], axis=1)
    return cos, sin_a, sin_b


def _rope_kernel(n_heads, x_ref, cos_ref, sa_ref, sb_ref, o_ref):
    q = HEAD_DIM // 4
    cos, sa, sb = cos_ref[...], sa_ref[...], sb_ref[...]
    for h in range(n_heads):
        sl = slice(h * HEAD_DIM, (h + 1) * HEAD_DIM)
        x = x_ref[:, sl]
        o_ref[:, sl] = x * cos + pltpu.roll(x, HEAD_DIM - q, 1) * sa + pltpu.roll(x, q, 1) * sb


def rope(u, bsz, n_tok, col0, n_heads, tt=256):
    cos, sin_a, sin_b = rope_tables(n_tok)
    nt = n_tok // tt
    hb = max(k for k in range(1, n_heads + 1) if n_heads % k == 0 and col0 % k == 0)
    width = hb * HEAD_DIM
    cb = col0 // hb
    tab = pl.BlockSpec((tt, HEAD_DIM), lambda b, i, j: (i, 0))
    return pl.pallas_call(
        functools.partial(_rope_kernel, hb),
        grid=(bsz, nt, n_heads // hb),
        in_specs=[pl.BlockSpec((tt, width), lambda b, i, j: (b * nt + i, cb + j)), tab, tab, tab],
        out_specs=pl.BlockSpec((tt, width), lambda b, i, j: (b * nt + i, j)),
        out_shape=jax.ShapeDtypeStruct((bsz * n_tok, n_heads * HEAD_DIM), F32),
        compiler_params=_cparams("parallel", "parallel", "parallel"),
        name="rope",
    )(u, cos, sin_a, sin_b)


def _win_attn_kernel(group, n_tok, sink_ref, q_ref, kp_ref, kc_ref, kn_ref, vp_ref, vc_ref, vn_ref,
                     ck_ref, cv_ref, o_ref):
    kv = pl.program_id(1)
    i = pl.program_id(2)
    rows = group * QBLK
    q = jnp.concatenate([q_ref[:, g * HEAD_DIM:(g + 1) * HEAD_DIM] for g in range(group)], axis=0)
    q = q * ATT_SCALE
    k_loc = jnp.concatenate([kp_ref[...], kc_ref[...], kn_ref[...]], axis=0)
    v_loc = jnp.concatenate([vp_ref[...], vc_ref[...], vn_ref[...]], axis=0).astype(BF16)
    s_loc = _dot3_nt(q, k_loc)
    qi = _iota(s_loc.shape, 0) & (QBLK - 1)
    kj = _iota(s_loc.shape, 1)
    pos = i * QBLK - WIN + kj
    ok = (kj - qi >= 0) & (kj - qi <= 2 * WIN) & (pos >= 0) & (pos < n_tok)
    s_loc = jnp.where(ok, s_loc, NEG)
    s_ctx = _dot3_nt(q, ck_ref[...])
    rg = _iota((rows, 1), 0) >> int(np.log2(QBLK))
    sk = jnp.zeros((rows, 1), F32)
    for g in range(group):
        sk = jnp.where(rg == g, sink_ref[kv * group + g], sk)
    m = jnp.maximum(jnp.maximum(jnp.max(s_loc, axis=-1, keepdims=True),
                                jnp.max(s_ctx, axis=-1, keepdims=True)), sk)
    e_loc = jnp.exp(s_loc - m)
    e_ctx = jnp.exp(s_ctx - m)
    den = (jnp.sum(e_loc, axis=-1, keepdims=True) + jnp.sum(e_ctx, axis=-1, keepdims=True)
           + jnp.exp(sk - m))
    o = (_dot(e_loc.astype(BF16), v_loc) + _dot(e_ctx.astype(BF16), cv_ref[...].astype(BF16))) / den
    for g in range(group):
        o_ref[:, g * HEAD_DIM:(g + 1) * HEAD_DIM] = o[g * QBLK:(g + 1) * QBLK].astype(o_ref.dtype)


def window_attention(qk_rot, u, vb0, bsz, n_tok, n_kv, group, ck, cv, sink):
    assert WIN == QBLK
    nb = n_tok // QBLK
    nq = n_kv * group
    past = ck.shape[2]
    prv = lambda i: jnp.maximum(i - 1, 0)
    nxt = lambda i: jnp.minimum(i + 1, nb - 1)
    blk = lambda f, c0: pl.BlockSpec((QBLK, HEAD_DIM), lambda b, kv, i, s: (b * nb + f(i), c0 + kv))
    same = lambda i: i
    cache = pl.BlockSpec((None, None, past, HEAD_DIM), lambda b, kv, i, s: (b, kv, 0, 0))
    grid_spec = pltpu.PrefetchScalarGridSpec(
        num_scalar_prefetch=1,
        grid=(bsz, n_kv, nb),
        in_specs=[
            pl.BlockSpec((QBLK, group * HEAD_DIM), lambda b, kv, i, s: (b * nb + i, kv)),
            blk(prv, nq), blk(same, nq), blk(nxt, nq),
            blk(prv, vb0), blk(same, vb0), blk(nxt, vb0),
            cache, cache,
        ],
        out_specs=pl.BlockSpec((QBLK, group * HEAD_DIM), lambda b, kv, i, s: (b * nb + i, kv)),
    )
    return pl.pallas_call(
        functools.partial(_win_attn_kernel, group, n_tok),
        grid_spec=grid_spec,
        out_shape=jax.ShapeDtypeStruct((bsz * n_tok, nq * HEAD_DIM), BF16),
        compiler_params=_cparams("parallel", "parallel", "parallel"),
        name="window_attention",
    )(sink.astype(F32), qk_rot, qk_rot, qk_rot, qk_rot, u, u, u, ck, cv)


NB_QROWS = 8
NB_QTOK = NB_QROWS * GRID_W
NB_KTOK = 3 * NB_QTOK


def nb_tables(rpb, rows):
    kh = min(NB_ROWS, rows)
    ql = np.arange(NB_QTOK)
    kl = np.arange(NB_KTOK)
    r_rel, c = ql // GRID_W, ql % GRID_W
    kr_rel, kc = kl // GRID_W - NB_QROWS, kl % GRID_W
    n_heads = rpb.shape[0]
    gcol = np.arange(GRID_W)
    col_idx = np.clip(gcol[None, :] - gcol[:, None] + NB_COLS - 1, 0, 2 * NB_COLS - 2)
    t_col = jnp.take(rpb.astype(F32), jnp.asarray(col_idx.reshape(-1)), axis=2)
    t_col = t_col.reshape(n_heads, 2 * NB_ROWS - 1, GRID_W * GRID_W)
    qr = np.arange(NB_QROWS)
    krr = np.arange(3 * NB_QROWS) - NB_QROWS
    row_idx = np.clip(krr[None, :] - qr[:, None] + NB_ROWS - 1, 0, 2 * NB_ROWS - 2)
    bias = jnp.take(t_col, jnp.asarray(row_idx.reshape(-1)), axis=1)
    bias = bias.reshape(n_heads, NB_QROWS, 3 * NB_QROWS, GRID_W, GRID_W)
    bias = jnp.transpose(bias, (0, 1, 3, 2, 4)).reshape(n_heads, NB_QTOK, NB_KTOK)
    win_start = np.clip(c - NB_COLS // 2, 0, GRID_W - NB_COLS)
    col_ok = (kc[None, :] >= win_start[:, None]) & (kc[None, :] < win_start[:, None] + NB_COLS)
    masks = []
    for j in range(rows // NB_QROWS):
        r = j * NB_QROWS + r_rel
        kr = j * NB_QROWS + kr_rel
        row_start = np.clip(r - kh // 2, 0, rows - kh)
        row_ok = (kr[None, :] >= row_start[:, None]) & (kr[None, :] < row_start[:, None] + kh)
        masks.append(row_ok & col_ok)
    return bias, jnp.asarray(np.stack(masks).astype(np.float32))


def _nb_attn_kernel(q_ref, kp_ref, kc_ref, kn_ref, vp_ref, vc_ref, vn_ref, ck_ref, cv_ref,
                    bias_ref, mask_ref, o_ref):
    q = q_ref[...] * ATT_SCALE
    k_loc = jnp.concatenate([kp_ref[...], kc_ref[...], kn_ref[...]], axis=0)
    v_loc = jnp.concatenate([vp_ref[...], vc_ref[...], vn_ref[...]], axis=0).astype(BF16)
    s_loc = jnp.where(mask_ref[...] > 0.0, _dot3_nt(q, k_loc) + bias_ref[...], NEG)
    s_ctx = _dot3_nt(q, ck_ref[...])
    m = jnp.maximum(jnp.max(s_loc, axis=-1, keepdims=True), jnp.max(s_ctx, axis=-1, keepdims=True))
    e_loc = jnp.exp(s_loc - m)
    e_ctx = jnp.exp(s_ctx - m)
    den = jnp.sum(e_loc, axis=-1, keepdims=True) + jnp.sum(e_ctx, axis=-1, keepdims=True)
    o = _dot(e_loc.astype(BF16), v_loc) + _dot(e_ctx.astype(BF16), cv_ref[...].astype(BF16))
    o_ref[...] = (o / den).astype(o_ref.dtype)


def neighbourhood_attention(u, cols, bsz, n_tok, n_heads, ck, cv, rpb):
    q0, k0, v0 = cols
    rows = n_tok // GRID_W
    assert rows % NB_QROWS == 0 and rows >= NB_ROWS
    nj = rows // NB_QROWS
    past = ck.shape[2]
    bias, mask = nb_tables(rpb, rows)
    prv = lambda j: jnp.maximum(j - 1, 0)
    nxt = lambda j: jnp.minimum(j + 1, nj - 1)
    same = lambda j: j
    blk = lambda f, c0: pl.BlockSpec((NB_QTOK, HEAD_DIM), lambda b, h, j: (b * nj + f(j), c0 + h))
    cache = pl.BlockSpec((None, None, past, HEAD_DIM), lambda b, h, j: (b, h, 0, 0))
    return pl.pallas_call(
        _nb_attn_kernel,
        grid=(bsz, n_heads, nj),
        in_specs=[
            blk(same, q0),
            blk(prv, k0), blk(same, k0), blk(nxt, k0),
            blk(prv, v0), blk(same, v0), blk(nxt, v0),
            cache, cache,
            pl.BlockSpec((None, NB_QTOK, NB_KTOK), lambda b, h, j: (h, 0, 0)),
            pl.BlockSpec((None, NB_QTOK, NB_KTOK), lambda b, h, j: (j, 0, 0)),
        ],
        out_specs=pl.BlockSpec((NB_QTOK, HEAD_DIM), lambda b, h, j: (b * nj + j, h)),
        out_shape=jax.ShapeDtypeStruct((bsz * n_tok, n_heads * HEAD_DIM), BF16),
        compiler_params=_cparams("parallel", "parallel", "parallel"),
        name="neighbourhood_attention",
    )(u, u, u, u, u, u, u, ck, cv, bias, mask)


def _merge_kernel(ya_ref, yb_ref, yc_ref, wa_ref, wb_ref, wc_ref, ga_ref, gb_ref, gc_ref, o_ref):
    m = _sigmoid(ga_ref[...]) * _dot(ya_ref[...], wa_ref[...])
    m = m + _sigmoid(gb_ref[...]) * _dot(yb_ref[...], wb_ref[...])
    m = m + _sigmoid(gc_ref[...]) * _dot(yc_ref[...], wc_ref[...])
    o_ref[...] = m.astype(o_ref.dtype)


def merge_branches(ya, yb, yc, wa, wb, wc, u, gate_col0, tm=512, tn=512):
    n, aw = ya.shape
    bw, cw = wb.shape[0], wc.shape[0]
    d = wa.shape[1]
    tm = _pick(n, (tm, 256, 128))
    tn = _pick(d, (tn, 256, 128))
    assert gate_col0 % tn == 0
    g0 = gate_col0 // tn
    nd = d // tn
    gate = lambda gi: pl.BlockSpec((tm, tn), lambda i, j: (i, g0 + gi * nd + j))
    return pl.pallas_call(
        _merge_kernel,
        grid=(n // tm, nd),
        in_specs=[
            pl.BlockSpec((tm, aw), lambda i, j: (i, 0)),
            pl.BlockSpec((tm, bw), lambda i, j: (i, 0)),
            pl.BlockSpec((tm, cw), lambda i, j: (i, 0)),
            pl.BlockSpec((aw, tn), lambda i, j: (0, j)),
            pl.BlockSpec((bw, tn), lambda i, j: (0, j)),
            pl.BlockSpec((cw, tn), lambda i, j: (0, j)),
            gate(0), gate(1), gate(2),
        ],
        out_specs=pl.BlockSpec((tm, tn), lambda i, j: (i, j)),
        out_shape=jax.ShapeDtypeStruct((n, d), BF16),
        compiler_params=_cparams("parallel", "parallel"),
        name="merge_branches",
    )(ya, yb, yc, wa, wb, wc, u, u, u)


def _post_norm(alpha, x, gate, y, g, b):
    r = alpha * x + gate * y
    return _ln_rows(r) * g + b


def _outproj_ln_kernel(alpha, nk, m_ref, w_ref, x_ref, gate_ref, g_ref, b_ref, o_ref, acc_ref):
    k = pl.program_id(1)

    @pl.when(k == 0)
    def _():
        acc_ref[...] = jnp.zeros_like(acc_ref)

    acc_ref[...] += _dot(m_ref[...], w_ref[...])

    @pl.when(k == nk - 1)
    def _():
        rows = acc_ref.shape[0]
        step = min(rows, 128)
        for r0 in range(0, rows, step):
            sl = slice(r0, r0 + step)
            o_ref[sl, :] = _post_norm(alpha, x_ref[sl, :], gate_ref[...], acc_ref[sl, :],
                                      g_ref[...], b_ref[...])


def _tile_mod_spec(mod, d, tm, seq_len):
    if mod.shape[0] == 1:
        return pl.BlockSpec((None, 1, d), lambda i, *_: (0, 0, 0))
    return pl.BlockSpec((None, 1, d), lambda i, *_: (i * tm // seq_len, 0, 0))


def outproj_postnorm(m, w_out, x, gate, ln_g, ln_b, alpha, seq_len, tm=512, tk=256):
    n, d = x.shape
    kdim = m.shape[1]
    tm = _pick(n if gate.shape[0] == 1 else seq_len, (tm, 256, 128))
    tk = _pick(kdim, (tk, 256, 128))
    nk = kdim // tk
    vec = pl.BlockSpec((1, d), lambda i, k: (0, 0))
    return pl.pallas_call(
        functools.partial(_outproj_ln_kernel, alpha, nk),
        grid=(n // tm, nk),
        in_specs=[
            pl.BlockSpec((tm, tk), lambda i, k: (i, k)),
            pl.BlockSpec((tk, d), lambda i, k: (k, 0)),
            pl.BlockSpec((tm, d), lambda i, k: (i, 0)),
            _tile_mod_spec(gate, d, tm, seq_len),
            vec, vec,
        ],
        out_specs=pl.BlockSpec((tm, d), lambda i, k: (i, 0)),
        out_shape=jax.ShapeDtypeStruct((n, d), F32),
        scratch_shapes=[pltpu.VMEM((tm, d), F32)],
        compiler_params=_cparams("parallel", "arbitrary"),
        name="outproj_postnorm",
    )(m, w_out, x, gate, ln_g, ln_b)


def _select_kernel(cap, n, aff_ref, mask_ref):
    bits = lax.bitcast_convert_type(aff_ref[...], jnp.int32)
    n_exp = bits.shape[0]
    capf = jnp.float32(cap)

    def count(pred):
        return jnp.sum(pred.astype(F32), axis=1, keepdims=True)

    def value_step(_, carry):
        lo, hi = carry
        mid = lo + ((hi - lo + 1) >> 1)
        ok = count(bits >= mid) >= capf
        return jnp.where(ok, mid, lo), jnp.where(ok, hi, mid - 1)

    lo0 = jnp.zeros((n_exp, 1), jnp.int32)
    hi0 = jnp.full((n_exp, 1), 0x7F800000, jnp.int32)
    thr, _ = lax.fori_loop(0, 32, value_step, (lo0, hi0))
    gt = bits > thr
    eq = bits == thr
    need = capf - count(gt)
    idx = _iota(bits.shape, 1)

    def index_step(_, carry):
        lo, hi = carry
        mid = (lo + hi) >> 1
        ok = count(eq & (idx < mid)) >= need
        return jnp.where(ok, lo, mid), jnp.where(ok, mid, hi)

    _, bound = lax.fori_loop(0, int(np.ceil(np.log2(n))) + 1, index_step,
                             (jnp.zeros((n_exp, 1), jnp.int32), jnp.full((n_exp, 1), n, jnp.int32)))
    mask_ref[...] = (gt | (eq & (idx < bound))).astype(jnp.int32)


def expert_choice_mask(aff_t, cap):
    n_exp, n = aff_t.shape
    return pl.pallas_call(
        functools.partial(_select_kernel, cap, n),
        out_shape=jax.ShapeDtypeStruct((n_exp, n), jnp.int32),
        compiler_params=pltpu.CompilerParams(vmem_limit_bytes=VMEM_LIMIT_BYTES),
        name="expert_choice_mask",
    )(aff_t)


def _visit_list(nvis, first_blk, n_work):
    na, nb = nvis.shape
    flat = nvis.reshape(-1)
    off_end = jnp.cumsum(flat)
    off_start = off_end - flat
    total = off_end[-1]
    w = jnp.minimum(jnp.arange(n_work, dtype=jnp.int32), total - 1)
    idx = jnp.minimum(jnp.searchsorted(off_end, w, side="right"), na * nb - 1).astype(jnp.int32)
    blk = first_blk.reshape(-1)[idx] + (w - off_start[idx])
    valid = (jnp.arange(n_work) < total).astype(jnp.int32)
    return idx // nb, idx % nb, blk.astype(jnp.int32), valid


def routing_plan(mask, cap):
    n_exp, n = mask.shape
    nt = n // TOK_TILE
    nsb = cap // SLOT_BLK
    pos = jnp.cumsum(mask, axis=1) - mask
    posm = jnp.where(mask > 0, pos, -1).astype(jnp.int32)
    cnt = mask.reshape(n_exp, nt, TOK_TILE).sum(-1)
    cend = jnp.cumsum(cnt, axis=1)
    cstart = cend - cnt
    sb_lo = jnp.minimum(cstart // SLOT_BLK, nsb - 1)
    nvis = jnp.where(cnt > 0, (cend - 1) // SLOT_BLK - sb_lo + 1, 0)
    n_work = n_exp * (nsb + nt)
    e, t, sb, valid = _visit_list(nvis, sb_lo, n_work)
    key = e * nsb + sb
    first = jnp.concatenate([jnp.ones((1,), jnp.int32), (key[1:] != key[:-1]).astype(jnp.int32)]) * valid
    dispatch = (e, sb, t, first, valid)
    nvis_t = nvis.T.at[:, 0].max(1)
    t2, e2, sb2, valid2 = _visit_list(nvis_t, sb_lo.T, n_work + nt)
    first2 = jnp.concatenate([jnp.ones((1,), jnp.int32), (t2[1:] != t2[:-1]).astype(jnp.int32)]) * valid2
    nxt_valid = jnp.concatenate([valid2[1:], jnp.zeros((1,), jnp.int32)])
    last2 = jnp.concatenate([(t2[1:] != t2[:-1]).astype(jnp.int32), jnp.ones((1,), jnp.int32)])
    last2 = jnp.maximum(last2, 1 - nxt_valid) * valid2
    combine = (t2, e2, sb2, first2, last2, valid2)
    return posm, dispatch, combine


def _dispatch_kernel(e_ref, sb_ref, t_ref, first_ref, valid_ref, posm_ref, tok_ref, o_ref, acc_ref):
    w = pl.program_id(0)

    @pl.when(first_ref[w] == 1)
    def _():
        acc_ref[...] = jnp.zeros_like(acc_ref)

    @pl.when(valid_ref[w] == 1)
    def _():
        slot = _iota((SLOT_BLK, TOK_TILE), 0) + sb_ref[w] * SLOT_BLK
        onehot = (slot == posm_ref[...]).astype(BF16)
        acc_ref[...] += _dot(onehot, tok_ref[...])

    o_ref[...] = acc_ref[...].astype(o_ref.dtype)


def moe_dispatch(tokens, posm, plan, cap):
    n, d = tokens.shape
    n_exp = posm.shape[0]
    nsb = cap // SLOT_BLK
    n_work = plan[0].shape[0]
    grid_spec = pltpu.PrefetchScalarGridSpec(
        num_scalar_prefetch=5,
        grid=(n_work,),
        in_specs=[
            pl.BlockSpec((None, 1, TOK_TILE), lambda w, e, sb, t, f, v: (e[w], 0, t[w])),
            pl.BlockSpec((TOK_TILE, d), lambda w, e, sb, t, f, v: (t[w], 0)),
        ],
        out_specs=pl.BlockSpec((SLOT_BLK, d), lambda w, e, sb, t, f, v: (e[w] * nsb + sb[w], 0)),
        scratch_shapes=[pltpu.VMEM((SLOT_BLK, d), F32)],
    )
    return pl.pallas_call(
        _dispatch_kernel,
        grid_spec=grid_spec,
        out_shape=jax.ShapeDtypeStruct((n_exp * cap, d), BF16),
        compiler_params=_cparams("arbitrary"),
        name="moe_dispatch",
    )(*plan, posm.reshape(n_exp, 1, n), tokens)


def _ffn_up_kernel(x_ref, w1_ref, w3_ref, o_ref):
    x = x_ref[...]
    a = _dot(x, w1_ref[...])
    o_ref[...] = (a * _sigmoid(a) * _dot(x, w3_ref[...])).astype(o_ref.dtype)


def _ffn_down_kernel(h_ref, w2_ref, o_ref):
    o_ref[...] = _dot(h_ref[...], w2_ref[...]).astype(o_ref.dtype)


def expert_ffn(xe, w1, w3, w2, cap, tn=512):
    n_exp, d, ff = w1.shape
    tm = _pick(cap, (1024, 512, 256, 128))
    nm = cap // tm
    tf = _pick(ff, (tn, 256, 128))
    hid = pl.pallas_call(
        _ffn_up_kernel,
        grid=(n_exp, nm, ff // tf),
        in_specs=[
            pl.BlockSpec((tm, d), lambda e, i, j: (e * nm + i, 0)),
            pl.BlockSpec((None, d, tf), lambda e, i, j: (e, 0, j)),
            pl.BlockSpec((None, d, tf), lambda e, i, j: (e, 0, j)),
        ],
        out_specs=pl.BlockSpec((tm, tf), lambda e, i, j: (e * nm + i, j)),
        out_shape=jax.ShapeDtypeStruct((n_exp * cap, ff), BF16),
        compiler_params=_cparams("parallel", "parallel", "parallel"),
        name="expert_ffn_up",
    )(xe, w1, w3)
    td = _pick(d, (tn, 256, 128))
    return pl.pallas_call(
        _ffn_down_kernel,
        grid=(n_exp, nm, d // td),
        in_specs=[
            pl.BlockSpec((tm, ff), lambda e, i, j: (e * nm + i, 0)),
            pl.BlockSpec((None, ff, td), lambda e, i, j: (e, 0, j)),
        ],
        out_specs=pl.BlockSpec((tm, td), lambda e, i, j: (e * nm + i, j)),
        out_shape=jax.ShapeDtypeStruct((n_exp * cap, d), BF16),
        compiler_params=_cparams("parallel", "parallel", "parallel"),
        name="expert_ffn_down",
    )(hid, w2)


def _combine_kernel(alpha, t_ref, e_ref, sb_ref, first_ref, last_ref, valid_ref,
                    posm_ref, aff_ref, ye_ref, x_ref, gate_ref, g_ref, b_ref, o_ref, acc_ref):
    w = pl.program_id(0)

    @pl.when(first_ref[w] == 1)
    def _():
        acc_ref[...] = jnp.zeros_like(acc_ref)

    @pl.when(valid_ref[w] == 1)
    def _():
        e = e_ref[w]
        posm = posm_ref[...].astype(F32)
        sel_p = _iota(posm.shape, 1) == e
        col = jnp.sum(jnp.where(sel_p, posm, 0.0), axis=1, keepdims=True)
        aff = aff_ref[...]
        gval = jnp.sum(jnp.where(_iota(aff.shape, 1) == e, aff, 0.0), axis=1, keepdims=True)
        slot = (_iota((TOK_TILE, SLOT_BLK), 1) + sb_ref[w] * SLOT_BLK).astype(F32)
        onehot = (slot == col).astype(BF16)
        acc_ref[...] += gval * _dot(onehot, ye_ref[...])

    @pl.when(last_ref[w] == 1)
    def _():
        o_ref[...] = _post_norm(alpha, x_ref[...], gate_ref[...], acc_ref[...], g_ref[...], b_ref[...])


def moe_combine_postnorm(ye, posm_t, aff, plan, cap, x, gate, ln_g, ln_b, alpha, seq_len):
    n, d = x.shape
    n_exp = posm_t.shape[1]
    nsb = cap // SLOT_BLK
    n_work = plan[0].shape[0]
    assert seq_len % TOK_TILE == 0
    tile = lambda w, t, *_: (t[w], 0)
    if gate.shape[0] == 1:
        gate_map = lambda w, t, *_: (0, 0, 0)
    else:
        gate_map = lambda w, t, *_: (t[w] * TOK_TILE // seq_len, 0, 0)
    vec = pl.BlockSpec((1, d), lambda w, *_: (0, 0))
    grid_spec = pltpu.PrefetchScalarGridSpec(
        num_scalar_prefetch=6,
        grid=(n_work,),
        in_specs=[
            pl.BlockSpec((TOK_TILE, n_exp), tile),
            pl.BlockSpec((TOK_TILE, LANE), tile),
            pl.BlockSpec((SLOT_BLK, d), lambda w, t, e, sb, *_: (e[w] * nsb + sb[w], 0)),
            pl.BlockSpec((TOK_TILE, d), tile),
            pl.BlockSpec((None, 1, d), gate_map),
            vec, vec,
        ],
        out_specs=pl.BlockSpec((TOK_TILE, d), tile),
        scratch_shapes=[pltpu.VMEM((TOK_TILE, d), F32)],
    )
    return pl.pallas_call(
        functools.partial(_combine_kernel, alpha),
        grid_spec=grid_spec,
        out_shape=jax.ShapeDtypeStruct((n, d), F32),
        compiler_params=_cparams("arbitrary"),
        name="moe_combine_postnorm",
    )(*plan, posm_t, aff, ye, x, gate, ln_g, ln_b)


def ec_moe_postnorm(x, mods_shift, mods_scale, gate, router_w, w1, w3, w2, ln_g, ln_b, alpha):
    bsz, t, d = x.shape
    n = bsz * t
    n_exp = router_w.shape[1]
    cap = max(1, CAP_FACTOR * n // n_exp)
    assert cap % SLOT_BLK == 0 and n % TOK_TILE == 0
    h, aff = ln_modulate_router(x, mods_shift, mods_scale, router_w)
    h = h.reshape(n, d)
    aff = aff.reshape(n, LANE)
    mask = expert_choice_mask(aff[:, :n_exp].T, cap)
    posm, dispatch, combine = routing_plan(mask, cap)
    xe = moe_dispatch(h, posm, dispatch, cap)
    ye = expert_ffn(xe, w1, w3, w2, cap)
    out = moe_combine_postnorm(ye, posm.T, aff, combine, cap, x.reshape(n, d), gate, ln_g, ln_b,
                               alpha, t)
    return out.reshape(bsz, t, d)


def prep_rwkv_params(rw, aw):
    a_cols = rw["mu"].shape[0]
    ap = round_up(a_cols, LANE)
    gp = ap - 3 * aw - 2 * LANE
    assert rw["w2"].shape[:2] == (2, LANE // 2) and rw["a2"].shape[:2] == (2, LANE // 2)
    assert gp >= rw["g2"].shape[0] and aw % LANE == 0
    row = lambda a: a.reshape(1, aw)
    return dict(
        mu=jnp.pad(rw["mu"], (0, ap - a_cols)).reshape(1, ap),
        w0=rw["w0"], w2=rw["w2"].reshape(LANE, aw), a0=rw["a0"], a2=rw["a2"].reshape(LANE, aw),
        g2=jnp.pad(rw["g2"], ((0, gp - rw["g2"].shape[0]), (0, 0))),
        k_k=row(rw["k_k"]), k_a=row(rw["k_a"]), r_k=row(rw["r_k"]),
        ln_w=row(rw["ln_w"]), ln_b=row(rw["ln_b"]), ap=ap, aw=aw)


def states_to_pairs(s):
    bsz, h = s.shape[:2]
    st = jnp.swapaxes(s, -1, -2).reshape(bsz, h // 2, 2, A_HEAD, A_HEAD)
    z = jnp.zeros_like(st[:, :, 0])
    top = jnp.concatenate([st[:, :, 0], z], axis=-1)
    bot = jnp.concatenate([z, st[:, :, 1]], axis=-1)
    return jnp.concatenate([top, bot], axis=-2)


def pairs_to_states(sp):
    bsz = sp.shape[0]
    st = jnp.stack([sp[:, :, :A_HEAD, :A_HEAD], sp[:, :, A_HEAD:, A_HEAD:]], axis=2)
    return jnp.swapaxes(st.reshape(bsz, -1, A_HEAD, A_HEAD), -1, -2)


def rwkv_branch(u, bsz, t, rwp, s0f, s0b):
    aw, ap = rwp["aw"], rwp["ap"]
    n = bsz * t
    r_, v_, kk, lwf, lwb, bf, bb, kdf, kdb, bonus, g = rwkv_prep(u, t, rwp, aw, ap)
    sh = lambda a: a.reshape(bsz, t, aw)
    yf, sf = rwkv_scan(sh(lwf), sh(kk), sh(bf), sh(kdf), sh(r_), sh(v_), states_to_pairs(s0f), False)
    yb, sb = rwkv_scan(sh(lwb), sh(kk), sh(bb), sh(kdb), sh(r_), sh(v_), states_to_pairs(s0b), True)
    ya = rwkv_post(yf.reshape(n, aw), yb.reshape(n, aw), bonus, g, rwp["ln_w"], rwp["ln_b"])
    return ya, pairs_to_states(sf), pairs_to_states(sb)


def _mod_rows(m):
    return [m[:, i][:, None, :] for i in range(6)]


def kernel(x_prompt, x_sample, cache_win_k, cache_win_v, cache_nb_k, cache_nb_v, state_rwkv_fwd, state_rwkv_bwd, c, c_ctx, ada_w, ada_b, w_in, rwkv_mu, rwkv_w0, rwkv_w2, rwkv_a0, rwkv_a2, rwkv_g2, rwkv_kk, rwkv_ka, rwkv_rk, rwkv_lnx_w, rwkv_lnx_b, win_sink, nb_rpb, w_br_a, w_br_b, w_br_c, w_out, ln1_g, ln1_b, ln2_g, ln2_b, router_w, exp_w1, exp_w3, exp_w2):
    depth, d, in_w = w_in.shape
    bsz, seq, _ = x_prompt.shape
    dbsz, dseq, _ = x_sample.shape
    aw = rwkv_w0.shape[-1]
    a_heads = aw // A_HEAD
    a_cols = rwkv_mu.shape[-1]
    ap = round_up(a_cols, LANE)
    n_b = win_sink.shape[-1]
    n_kv = cache_win_k.shape[3]
    group = n_b // n_kv
    n_c = nb_rpb.shape[1]
    bw, cw = n_b * HEAD_DIM, n_c * HEAD_DIM
    alpha = float((2 * depth) ** 0.25)
    assert in_w == a_cols + bw + 2 * n_kv * HEAD_DIM + 3 * cw + 3 * d

    qb0 = ap // LANE
    kb0 = qb0 + n_b
    vb0 = kb0 + n_kv
    qn0 = vb0 + n_kv
    kn0 = qn0 + n_c
    vn0 = kn0 + n_c
    gate_col0 = (vn0 + n_c) * LANE

    rows = 1 + dbsz
    cvecs = jnp.pad(jnp.concatenate([c_ctx[None], c], axis=0), ((0, round_up(rows, 16) - rows), (0, 0)))
    mods = ada_mods(cvecs, ada_w, ada_b).reshape(depth, -1, 6, d)

    xp, xs = x_prompt, x_sample
    np_tok, ns_tok = bsz * seq, dbsz * dseq
    win_k, win_v, nb_k, nb_v, st_f, st_b = [], [], [], [], [], []
    zero_state = jnp.zeros((bsz, a_heads, A_HEAD, A_HEAD), F32)
    for l in range(depth):
        w_cat = jnp.concatenate(
            [w_in[l, :, :a_cols].astype(BF16), jnp.zeros((d, ap - a_cols), BF16),
             w_in[l, :, a_cols:].astype(BF16)], axis=1)
        rwp = prep_rwkv_params(
            {"mu": rwkv_mu[l], "w0": rwkv_w0[l], "w2": rwkv_w2[l], "a0": rwkv_a0[l], "a2": rwkv_a2[l],
             "g2": rwkv_g2[l], "k_k": rwkv_kk[l], "k_a": rwkv_ka[l], "r_k": rwkv_rk[l],
             "ln_w": rwkv_lnx_w[l], "ln_b": rwkv_lnx_b[l]}, aw)
        wa, wb, wc = w_br_a[l].astype(BF16), w_br_b[l].astype(BF16), w_br_c[l].astype(BF16)
        wo = w_out[l].astype(BF16)
        w1, w3, w2 = exp_w1[l].astype(BF16), exp_w3[l].astype(BF16), exp_w2[l].astype(BF16)
        g1, b1 = ln1_g[l].reshape(1, d), ln1_b[l].reshape(1, d)
        g2, b2 = ln2_g[l].reshape(1, d), ln2_b[l].reshape(1, d)

        sh1, sc1, gt1, sh2, sc2, gt2 = _mod_rows(mods[l, 0:1])
        h = ln_modulate(xp, sh1, sc1).reshape(np_tok, d)
        u = matmul(h, w_cat, name="in_proj")
        ya, s_f, s_b = rwkv_branch(u, bsz, seq, rwp, zero_state, zero_state)
        yb = ctx_attention(u, bsz, seq, qb0, kb0, vb0, n_b, group, win_sink[l])
        yc = ctx_attention(u, bsz, seq, qn0, kn0, vn0, n_c, 1, None)
        m = merge_branches(ya, yb, yc, wa, wb, wc, u, gate_col0)
        x1 = outproj_postnorm(m, wo, xp.reshape(np_tok, d), gt1, g1, b1, alpha, seq)
        xp = ec_moe_postnorm(x1.reshape(bsz, seq, d), sh2, sc2, gt2, router_w[l], w1, w3, w2, g2, b2, alpha)
        cols = lambda c0, nh: u[:, c0 * LANE:(c0 + nh) * LANE].reshape(bsz, seq, nh, HEAD_DIM)
        win_k.append(cols(kb0, n_kv))
        win_v.append(cols(vb0, n_kv))
        nb_k.append(cols(kn0, n_c))
        nb_v.append(cols(vn0, n_c))
        st_f.append(s_f)
        st_b.append(s_b)

        sh1, sc1, gt1, sh2, sc2, gt2 = _mod_rows(mods[l, 1:1 + dbsz])
        h = ln_modulate(xs, sh1, sc1).reshape(ns_tok, d)
        u = matmul(h, w_cat, name="in_proj")
        ya, _, _ = rwkv_branch(u, dbsz, dseq, rwp, state_rwkv_fwd[:, l], state_rwkv_bwd[:, l])
        qk_rot = rope(u, dbsz, dseq, qb0, n_b + n_kv)
        heads_first = lambda a: jnp.swapaxes(a[:, l], 1, 2)
        yb = window_attention(qk_rot, u, vb0, dbsz, dseq, n_kv, group,
                              heads_first(cache_win_k), heads_first(cache_win_v), win_sink[l])
        yc = neighbourhood_attention(u, (qn0, kn0, vn0), dbsz, dseq, n_c,
                                     heads_first(cache_nb_k), heads_first(cache_nb_v), nb_rpb[l])
        m = merge_branches(ya, yb, yc, wa, wb, wc, u, gate_col0)
        x1 = outproj_postnorm(m, wo, xs.reshape(ns_tok, d), gt1, g1, b1, alpha, dseq)
        xs = ec_moe_postnorm(x1.reshape(dbsz, dseq, d), sh2, sc2, gt2, router_w[l], w1, w3, w2, g2, b2, alpha)

    stack = lambda xs_: jnp.stack(xs_, axis=1)
    return (xp, xs, stack(win_k), stack(win_v), stack(nb_k), stack(nb_v), stack(st_f), stack(st_b))
```

```python
import functools

import numpy as np
import jax
import jax.numpy as jnp
from jax import lax
from jax.experimental import pallas as pl
from jax.experimental.pallas import tpu as pltpu

F32 = jnp.float32
BF16 = jnp.bfloat16

HEAD_DIM = 128
A_HEAD = 64
GRID_W = 64
WIN = 128
QBLK = 128
NB_ROWS = 8
NB_COLS = 16
CAP_FACTOR = 2
ROPE_THETA = 10000.0
LN_EPS = 1e-6
GN_EPS = 64e-5
NEG = -1e30

LANE = 128
SUBLANE = 8
VMEM_LIMIT_BYTES = 56 * 1024 * 1024

SCAN_CHUNK = 64
SLOT_BLK = 128
TOK_TILE = 256


def _cparams(*sem):
    return pltpu.CompilerParams(dimension_semantics=sem, vmem_limit_bytes=VMEM_LIMIT_BYTES)


def _dot(a, b):
    return jnp.dot(a, b, preferred_element_type=F32)


def _dot_nt(a, b):
    return lax.dot_general(a, b, (((1,), (1,)), ((), ())), preferred_element_type=F32)


def _split2(x):
    hi = x.astype(BF16)
    lo = (x - hi.astype(F32)).astype(BF16)
    return hi, lo


def _split3(x):
    hi = x.astype(BF16)
    r1 = x - hi.astype(F32)
    mid = r1.astype(BF16)
    lo = (r1 - mid.astype(F32)).astype(BF16)
    return hi, mid, lo


def _dot3(a, b):
    ah, al = _split2(a)
    bh, bl = _split2(b)
    return _dot(ah, bh) + (_dot(ah, bl) + _dot(al, bh))


def _dot3_nt(a, b):
    ah, al = _split2(a)
    bh, bl = _split2(b)
    return _dot_nt(ah, bh) + (_dot_nt(ah, bl) + _dot_nt(al, bh))


def _dot_exact_lhs(a_bf16, b):
    bh, bm, bl = _split3(b)
    return _dot(a_bf16, bh) + (_dot(a_bf16, bm) + _dot(a_bf16, bl))


def _sigmoid(x):
    return 1.0 / (1.0 + jnp.exp(-x))


def _iota(shape, dim):
    return lax.broadcasted_iota(jnp.int32, shape, dim)


def _ada_kernel(c_ref, w_ref, b_ref, o_ref):
    c = c_ref[...]
    a = c * _sigmoid(c)
    o_ref[...] = _dot3(a, w_ref[...]) + b_ref[...]


def ada_mods(cvecs, ada_w, ada_b, tn=512):
    depth, d, n6 = ada_w.shape
    rows = cvecs.shape[0]
    return pl.pallas_call(
        _ada_kernel,
        grid=(depth, n6 // tn),
        in_specs=[
            pl.BlockSpec((rows, d), lambda l, j: (0, 0)),
            pl.BlockSpec((None, d, tn), lambda l, j: (l, 0, j)),
            pl.BlockSpec((None, 1, tn), lambda l, j: (l, 0, j)),
        ],
        out_specs=pl.BlockSpec((None, rows, tn), lambda l, j: (l, 0, j)),
        out_shape=jax.ShapeDtypeStruct((depth, rows, n6), F32),
        compiler_params=_cparams("parallel", "parallel"),
        name="ada_mods",
    )(cvecs, ada_w, ada_b.reshape(depth, 1, n6))


def _ln_rows(x):
    mu = jnp.mean(x, axis=-1, keepdims=True)
    xc = x - mu
    var = jnp.mean(xc * xc, axis=-1, keepdims=True)
    return xc * lax.rsqrt(var + LN_EPS)


def _lnmod_kernel(x_ref, sh_ref, sc_ref, o_ref):
    y = _ln_rows(x_ref[...])
    o_ref[...] = (y * (1.0 + sc_ref[...]) + sh_ref[...]).astype(o_ref.dtype)


def _mod_spec(mod, d):
    if mod.shape[0] == 1:
        return pl.BlockSpec((None, 1, d), lambda b, i: (0, 0, 0))
    return pl.BlockSpec((None, 1, d), lambda b, i: (b, 0, 0))


def ln_modulate(x, shift, scale, tt=256):
    bsz, t, d = x.shape
    return pl.pallas_call(
        _lnmod_kernel,
        grid=(bsz, t // tt),
        in_specs=[
            pl.BlockSpec((None, tt, d), lambda b, i: (b, i, 0)),
            _mod_spec(shift, d),
            _mod_spec(scale, d),
        ],
        out_specs=pl.BlockSpec((None, tt, d), lambda b, i: (b, i, 0)),
        out_shape=jax.ShapeDtypeStruct((bsz, t, d), BF16),
        compiler_params=_cparams("parallel", "parallel"),
        name="ln_modulate",
    )(x, shift, scale)


def _lnmod_router_kernel(n_exp, x_ref, sh_ref, sc_ref, rw_ref, h_ref, aff_ref):
    y = _ln_rows(x_ref[...])
    h = y * (1.0 + sc_ref[...]) + sh_ref[...]
    h_ref[...] = h.astype(h_ref.dtype)
    logits = _dot3(h, rw_ref[...])
    lane = _iota(logits.shape, 1)
    logits = jnp.where(lane < n_exp, logits, NEG)
    m = jnp.max(logits, axis=-1, keepdims=True)
    e = jnp.exp(logits - m)
    aff_ref[...] = e / jnp.sum(e, axis=-1, keepdims=True)


def ln_modulate_router(x, shift, scale, router_w, tt=256):
    bsz, t, d = x.shape
    n_exp = router_w.shape[1]
    rw = jnp.pad(router_w, ((0, 0), (0, LANE - n_exp)))
    return pl.pallas_call(
        functools.partial(_lnmod_router_kernel, n_exp),
        grid=(bsz, t // tt),
        in_specs=[
            pl.BlockSpec((None, tt, d), lambda b, i: (b, i, 0)),
            _mod_spec(shift, d),
            _mod_spec(scale, d),
            pl.BlockSpec((d, LANE), lambda b, i: (0, 0)),
        ],
        out_specs=[
            pl.BlockSpec((None, tt, d), lambda b, i: (b, i, 0)),
            pl.BlockSpec((None, tt, LANE), lambda b, i: (b, i, 0)),
        ],
        out_shape=[
            jax.ShapeDtypeStruct((bsz, t, d), BF16),
            jax.ShapeDtypeStruct((bsz, t, LANE), F32),
        ],
        compiler_params=_cparams("parallel", "parallel"),
        name="ln_modulate_router",
    )(x, shift, scale, rw)


def _mm_kernel(a_ref, b_ref, o_ref):
    o_ref[...] = _dot(a_ref[...], b_ref[...]).astype(o_ref.dtype)


def _pick(n, pref):
    for c in pref:
        if n % c == 0:
            return c
    return n


def matmul(a, b, layer, out_dtype=F32, tm=None, tn=None, name="matmul"):
    m, k = a.shape
    n = b.shape[2]
    tm = tm or _pick(m, (1024, 512, 256, 128))
    tn = tn or _pick(n, (768, 512, 256, 128))
    return pl.pallas_call(
        _mm_kernel,
        grid=(m // tm, n // tn),
        in_specs=[
            pl.BlockSpec((tm, k), lambda i, j: (i, 0)),
            pl.BlockSpec((None, k, tn), lambda i, j: (layer, 0, j)),
        ],
        out_specs=pl.BlockSpec((tm, tn), lambda i, j: (i, j)),
        out_shape=jax.ShapeDtypeStruct((m, n), out_dtype),
        compiler_params=_cparams("parallel", "parallel"),
        name=name,
    )(a, b)


HEAD_SHIFT = 6
assert (1 << HEAD_SHIFT) == A_HEAD and 2 * A_HEAD == LANE and SCAN_CHUNK == A_HEAD


def _head_block_ones():
    i = _iota((LANE, LANE), 0) >> HEAD_SHIFT
    j = _iota((LANE, LANE), 1) >> HEAD_SHIFT
    return (i == j).astype(BF16)


def _seg_sum(x):
    bd = _head_block_ones()
    outs = []
    for c in range(x.shape[1] // LANE):
        hi, mid, lo = _split3(x[:, c * LANE:(c + 1) * LANE])
        outs.append(_dot(hi, bd) + (_dot(mid, bd) + _dot(lo, bd)))
    return jnp.concatenate(outs, axis=1)


def _rwkv_prep_kernel(aw, gp, tiles_per_seq,
                      x_ref, xp_ref, xn_ref, mu_ref, w0_ref, w2_ref, a0_ref, a2_ref, g2_ref,
                      kkp_ref, kap_ref, rkp_ref,
                      r_o, v_o, kk_o, lwf_o, lwb_o, bf_o, bb_o, kdf_o, kdb_o, bonus_o, g_o):
    i = pl.program_id(0)
    x = x_ref[...]
    tt = x.shape[0]
    row = _iota(x.shape, 0)
    pos = i % tiles_per_seq
    prev_row = jnp.where(pos == 0, 0.0, xp_ref[SUBLANE - 1:SUBLANE, :])
    next_row = jnp.where(pos == tiles_per_seq - 1, 0.0, xn_ref[0:1, :])
    prev = jnp.where(row == 0, prev_row, pltpu.roll(x, 1, 0))
    nxt = jnp.where(row == tt - 1, next_row, pltpu.roll(x, tt - 1, 0))
    xs = x + mu_ref[...] * (0.5 * (prev + nxt) - x)

    r = xs[:, 0:aw]
    k = xs[:, aw:2 * aw]
    v = xs[:, 2 * aw:3 * aw]
    o = 3 * aw
    wlo = jnp.tanh(xs[:, o:o + LANE])
    alo = xs[:, o + LANE:o + 2 * LANE]
    glo = _sigmoid(xs[:, o + 2 * LANE:o + 2 * LANE + gp])
    lane = _iota((1, LANE), 1)
    exp_mhalf = float(np.exp(-0.5))

    r_o[...] = r
    v_o[...] = v
    g_o[...] = _dot3(glo, g2_ref[...])

    kk = k * kkp_ref[...]
    nrm = jnp.sqrt(_seg_sum(kk * kk))
    kk = kk / jnp.maximum(nrm, 1e-12)
    kk_o[...] = kk
    bonus_o[...] = _seg_sum(r * k * rkp_ref[...]) * v

    for d, (lw_o, b_o, kd_o) in enumerate(((lwf_o, bf_o, kdf_o), (lwb_o, bb_o, kdb_o))):
        sel = ((lane >> HEAD_SHIFT) == d).astype(F32)
        wl = w0_ref[d:d + 1, :] + _dot3(wlo * sel, w2_ref[...])
        lw_o[...] = -_sigmoid(wl) * exp_mhalf
        a = _sigmoid(a0_ref[d:d + 1, :] + _dot3(alo * sel, a2_ref[...]))
        b_o[...] = kk * a
        kd_o[...] = k * (1.0 + (a - 1.0) * kap_ref[...])


def rwkv_prep(u, seq_len, rw, aw, ap, tt=128):
    n = u.shape[0]
    gp = ap - 3 * aw - 2 * LANE
    n8 = n // SUBLANE
    tpb = tt // SUBLANE
    full = lambda shape: pl.BlockSpec(shape, lambda i: (0,) * len(shape))
    tok = pl.BlockSpec((tt, aw), lambda i: (i, 0))
    outs = pl.pallas_call(
        functools.partial(_rwkv_prep_kernel, aw, gp, seq_len // tt),
        grid=(n // tt,),
        in_specs=[
            pl.BlockSpec((tt, ap), lambda i: (i, 0)),
            pl.BlockSpec((SUBLANE, ap), lambda i: (jnp.maximum(i * tpb - 1, 0), 0)),
            pl.BlockSpec((SUBLANE, ap), lambda i: (jnp.minimum((i + 1) * tpb, n8 - 1), 0)),
            full((1, ap)), full((2, aw)), full((LANE, aw)), full((2, aw)), full((LANE, aw)),
            full((gp, aw)), full((1, aw)), full((1, aw)), full((1, aw)),
        ],
        out_specs=[tok] * 11,
        out_shape=[jax.ShapeDtypeStruct((n, aw), F32)] * 11,
        compiler_params=_cparams("parallel"),
        name="rwkv_prep",
    )(u, u, u, rw["mu"], rw["w0"], rw["w2"], rw["a0"], rw["a2"], rw["g2"],
      rw["k_k"], rw["k_a"], rw["r_k"])
    return outs


def _scan_chunks(rev, toks, sts, consts):
    tri, strict, incl, blk, eye, head0, t2, s2 = consts
    c = SCAN_CHUNK
    ident = (t2 == s2).astype(F32)
    zero = jnp.zeros((), BF16)
    cat = lambda *xs: jnp.concatenate(xs, axis=0)

    def each(f, *lists):
        return [f(*args) for args in zip(*lists)]

    def bd(x):
        xb = x.astype(BF16)
        return jnp.where(blk, cat(xb, xb), zero)

    lw, kk, bb, kd, r, v = (list(t) for t in zip(*toks))
    big_l = each(lambda a: _dot_exact_lhs(tri, a), lw)
    l_tot = each(lambda a: a[0:1, :] if rev else a[c - 1:c, :], big_l)
    lhs = each(lambda k_, r_, l_, w_: cat(-k_ * jnp.exp(l_ - w_), r_ * jnp.exp(l_)).astype(BF16),
               kk, r, big_l, lw)
    gi = each(lambda l_: jnp.exp(-l_), big_l)
    bt = each(lambda b_, g_: (b_ * g_).astype(BF16), bb, gi)
    kt = each(lambda k_, g_: (k_ * g_).astype(BF16), kd, gi)
    rhs = each(lambda b_, k_: cat(jnp.where(head0, b_, zero), jnp.where(head0, zero, b_),
                                  jnp.where(head0, k_, zero), jnp.where(head0, zero, k_)), bt, kt)
    p = each(_dot_nt, lhs, rhs)
    nab = each(lambda p_: jnp.where(strict, p_[0:c, 0:LANE], 0.0).astype(BF16), p)
    nrb = each(lambda p_: jnp.where(incl, p_[c:2 * c, 0:LANE], 0.0).astype(BF16), p)
    nkk = each(lambda p_: cat(jnp.where(strict, p_[0:c, LANE:2 * LANE], 0.0),
                              jnp.where(incl, p_[c:2 * c, LANE:2 * LANE], 0.0)).astype(BF16), p)
    z = each(lambda l_, s_: _dot(l_, s_.astype(BF16)), lhs, sts)
    w = each(lambda n_, v_: _dot(n_, bd(v_)), nkk, v)
    x = each(lambda z_, w_: z_[0:c] + w_[0:c], z, w)
    blk8 = (t2 >> 3) == (s2 >> 3)
    n0 = each(lambda n_: jnp.where(blk8, n_, zero), nab)
    inv = each(lambda n_: ident + n_.astype(F32), n0)
    pw = each(lambda n_: _dot(n_, bd(n_)), n0)
    inv = each(lambda i_, p_: i_ + _dot(i_.astype(BF16), bd(p_)), inv, pw)
    pw = each(lambda p_: _dot(p_.astype(BF16), bd(p_)), pw)
    inv = each(lambda i_, p_: i_ + _dot(i_.astype(BF16), bd(p_)), inv, pw)
    for lb in range(3, HEAD_SHIFT):
        new = ((t2 >> (lb + 1)) == (s2 >> (lb + 1))) & ((t2 >> lb) != (s2 >> lb))
        half = each(lambda i_, n_: _dot(i_.astype(BF16), bd(jnp.where(new, n_, zero))), inv, nab)
        inv = each(lambda i_, h_: i_ + _dot(h_.astype(BF16), bd(i_)), inv, half)
    x = each(lambda i_, x_: _dot(i_.astype(BF16), bd(x_)), inv, x)
    y = each(lambda z_, w_, n_, x_: z_[c:2 * c] + w_[c:2 * c] + _dot(n_, bd(x_)), z, w, nrb, x)
    gr = each(lambda t_, l_: jnp.exp(t_ - l_), l_tot, big_l)
    upd = each(lambda b_, k_, g_, x_, v_: _dot3(cat(b_ * g_, k_ * g_).T, cat(x_, v_)), bb, kd, gr, x, v)
    g_col = each(lambda t_: jnp.sum(jnp.where(eye, jnp.exp(t_), 0.0), axis=1, keepdims=True), l_tot)
    st_new = each(lambda s_, g_, u_: s_ * g_ + jnp.where(blk, u_, 0.0), sts, g_col, upd)
    return y, st_new


def _scan_kernel(rev, npar, n_chunks, lw_ref, kk_ref, bb_ref, kd_ref, r_ref, v_ref, s0_ref,
                 y_ref, st_out_ref, st_scr):
    ci = pl.program_id(2)

    @pl.when(ci == 0)
    def _():
        st_scr[...] = s0_ref[...]

    c = SCAN_CHUNK
    ti = _iota((c, c), 0)
    si = _iota((c, c), 1)
    tri = ((si >= ti) if rev else (si <= ti)).astype(BF16)
    t2 = _iota((c, LANE), 0)
    s2 = _iota((c, LANE), 1) & (c - 1)
    strict = (s2 > t2) if rev else (s2 < t2)
    incl = (s2 >= t2) if rev else (s2 <= t2)
    bi = _iota((LANE, LANE), 0)
    bj = _iota((LANE, LANE), 1)
    blk = (bi >> HEAD_SHIFT) == (bj >> HEAD_SHIFT)
    eye = bi == bj
    head0 = _iota((c, LANE), 1) < A_HEAD
    consts = (tri, strict, incl, blk, eye, head0, t2, s2)
    lanes = [slice(p * LANE, (p + 1) * LANE) for p in range(npar)]
    toks = [tuple(ref[:, sl] for ref in (lw_ref, kk_ref, bb_ref, kd_ref, r_ref, v_ref)) for sl in lanes]
    ys, sts = _scan_chunks(rev, toks, [st_scr[p] for p in range(npar)], consts)
    for p, sl in enumerate(lanes):
        y_ref[:, sl] = ys[p]
        st_scr[p] = sts[p]

    @pl.when(ci == n_chunks - 1)
    def _():
        st_out_ref[...] = st_scr[...]


def rwkv_scan(lw, kk, bb, kd, r, v, s0, rev, npar=6):
    bsz, t, aw = lw.shape
    n_pairs = aw // LANE
    npar = max(p for p in range(1, npar + 1) if n_pairs % p == 0)
    n_chunks = t // SCAN_CHUNK
    if rev:
        tmap = lambda b, g, c: (b, n_chunks - 1 - c, g)
    else:
        tmap = lambda b, g, c: (b, c, g)
    tok = pl.BlockSpec((None, SCAN_CHUNK, npar * LANE), tmap)
    st = pl.BlockSpec((None, npar, LANE, LANE), lambda b, g, c: (b, g, 0, 0))
    return pl.pallas_call(
        functools.partial(_scan_kernel, rev, npar, n_chunks),
        grid=(bsz, n_pairs // npar, n_chunks),
        in_specs=[tok] * 6 + [st],
        out_specs=[tok, st],
        out_shape=[jax.ShapeDtypeStruct((bsz, t, aw), F32),
                   jax.ShapeDtypeStruct(s0.shape, F32)],
        scratch_shapes=[pltpu.VMEM((npar, LANE, LANE), F32)],
        compiler_params=_cparams("parallel", "parallel", "arbitrary"),
        name="rwkv_scan_bwd" if rev else "rwkv_scan_fwd",
    )(lw, kk, bb, kd, r, v, s0)


def _rwkv_post_kernel(yf_ref, yb_ref, bonus_ref, g_ref, lnw_ref, lnb_ref, o_ref):
    y = yf_ref[...] + yb_ref[...]
    inv = 1.0 / A_HEAD
    mu = _seg_sum(y) * inv
    yc = y - mu
    var = _seg_sum(yc * yc) * inv
    yn = yc * lax.rsqrt(var + GN_EPS) * lnw_ref[...] + lnb_ref[...]
    o_ref[...] = ((yn + bonus_ref[...]) * g_ref[...]).astype(o_ref.dtype)


def rwkv_post(yf, yb, bonus, g, ln_w, ln_b, tt=256):
    n, aw = yf.shape
    tok = pl.BlockSpec((tt, aw), lambda i: (i, 0))
    par = pl.BlockSpec((1, aw), lambda i: (0, 0))
    return pl.pallas_call(
        _rwkv_post_kernel,
        grid=(n // tt,),
        in_specs=[tok, tok, tok, tok, par, par],
        out_specs=tok,
        out_shape=jax.ShapeDtypeStruct((n, aw), BF16),
        compiler_params=_cparams("parallel"),
        name="rwkv_post",
    )(yf, yb, bonus, g, ln_w, ln_b)


def round_up(x, m):
    return (x + m - 1) // m * m


ATT_SCALE = HEAD_DIM ** -0.5


def _ctx_attn_kernel(q_ref, k_ref, v_ref, sink_ref, o_ref):
    s = _dot3_nt(q_ref[...] * ATT_SCALE, k_ref[...])
    sk = sink_ref[0:1, 0:1]
    m = jnp.maximum(jnp.max(s, axis=-1, keepdims=True), sk)
    e = jnp.exp(s - m)
    den = jnp.sum(e, axis=-1, keepdims=True) + jnp.exp(sk - m)
    o = _dot(e.astype(BF16), v_ref[...].astype(BF16))
    o_ref[...] = (o / den).astype(o_ref.dtype)


def ctx_attention(u, bsz, t, q0, k0, v0, n_heads, group, sink):
    sink = jnp.full((n_heads,), NEG, F32) if sink is None else sink.astype(F32)
    sink = jnp.broadcast_to(sink[:, None, None], (n_heads, 1, LANE))
    return pl.pallas_call(
        _ctx_attn_kernel,
        grid=(bsz, n_heads),
        in_specs=[
            pl.BlockSpec((t, HEAD_DIM), lambda b, h: (b, q0 + h)),
            pl.BlockSpec((t, HEAD_DIM), lambda b, h: (b, k0 + h // group)),
            pl.BlockSpec((t, HEAD_DIM), lambda b, h: (b, v0 + h // group)),
            pl.BlockSpec((None, 1, LANE), lambda b, h: (h, 0, 0)),
        ],
        out_specs=pl.BlockSpec((t, HEAD_DIM), lambda b, h: (b, h)),
        out_shape=jax.ShapeDtypeStruct((bsz * t, n_heads * HEAD_DIM), BF16),
        compiler_params=_cparams("parallel", "parallel"),
        name="ctx_attention",
    )(u, u, u, sink)


def rope_tables(n_tok):
    half = HEAD_DIM // 2
    quarter = half // 2
    tok = jnp.arange(n_tok)
    row = (tok // GRID_W).astype(F32)
    col = (tok % GRID_W).astype(F32)
    inv = ROPE_THETA ** (-jnp.arange(quarter, dtype=F32) / quarter)
    ang_r = row[:, None] * inv[None]
    ang_c = col[:, None] * inv[None]
    cos = jnp.concatenate([jnp.cos(ang_r)] * 2 + [jnp.cos(ang_c)] * 2, axis=1)
    sr, sc = jnp.sin(ang_r), jnp.sin(ang_c)
    z = jnp.zeros_like(sr)
    sin_a = jnp.concatenate([-sr, z, -sc, z], axis=1)
    sin_b = jnp.concatenate([z, sr, z, sc], axis=1)
    return cos, sin_a, sin_b


def _rope_kernel(n_heads, x_ref, cos_ref, sa_ref, sb_ref, o_ref):
    q = HEAD_DIM // 4
    cos, sa, sb = cos_ref[...], sa_ref[...], sb_ref[...]
    for h in range(n_heads):
        sl = slice(h * HEAD_DIM, (h + 1) * HEAD_DIM)
        x = x_ref[:, sl]
        o_ref[:, sl] = x * cos + pltpu.roll(x, HEAD_DIM - q, 1) * sa + pltpu.roll(x, q, 1) * sb


def rope(u, bsz, n_tok, col0, n_heads, tt=256):
    cos, sin_a, sin_b = rope_tables(n_tok)
    nt = n_tok // tt
    hb = max(k for k in range(1, n_heads + 1) if n_heads % k == 0 and col0 % k == 0)
    width = hb * HEAD_DIM
    cb = col0 // hb
    tab = pl.BlockSpec((tt, HEAD_DIM), lambda b, i, j: (i, 0))
    return pl.pallas_call(
        functools.partial(_rope_kernel, hb),
        grid=(bsz, nt, n_heads // hb),
        in_specs=[pl.BlockSpec((tt, width), lambda b, i, j: (b * nt + i, cb + j)), tab, tab, tab],
        out_specs=pl.BlockSpec((tt, width), lambda b, i, j: (b * nt + i, j)),
        out_shape=jax.ShapeDtypeStruct((bsz * n_tok, n_heads * HEAD_DIM), F32),
        compiler_params=_cparams("parallel", "parallel", "parallel"),
        name="rope",
    )(u, cos, sin_a, sin_b)


def _win_attn_kernel(group, n_tok, sink_ref, q_ref, kp_ref, kc_ref, kn_ref, vp_ref, vc_ref, vn_ref,
                     ck_ref, cv_ref, o_ref):
    kv = pl.program_id(1)
    i = pl.program_id(2)
    rows = group * QBLK
    q = jnp.concatenate([q_ref[:, g * HEAD_DIM:(g + 1) * HEAD_DIM] for g in range(group)], axis=0)
    q = q * ATT_SCALE
    k_loc = jnp.concatenate([kp_ref[...], kc_ref[...], kn_ref[...]], axis=0)
    v_loc = jnp.concatenate([vp_ref[...], vc_ref[...], vn_ref[...]], axis=0).astype(BF16)
    s_loc = _dot3_nt(q, k_loc)
    qi = _iota(s_loc.shape, 0) & (QBLK - 1)
    kj = _iota(s_loc.shape, 1)
    pos = i * QBLK - WIN + kj
    ok = (kj - qi >= 0) & (kj - qi <= 2 * WIN) & (pos >= 0) & (pos < n_tok)
    s_loc = jnp.where(ok, s_loc, NEG)
    s_ctx = _dot3_nt(q, ck_ref[...])
    rg = _iota((rows, 1), 0) >> int(np.log2(QBLK))
    sk = jnp.zeros((rows, 1), F32)
    for g in range(group):
        sk = jnp.where(rg == g, sink_ref[kv * group + g], sk)
    m = jnp.maximum(jnp.maximum(jnp.max(s_loc, axis=-1, keepdims=True),
                                jnp.max(s_ctx, axis=-1, keepdims=True)), sk)
    e_loc = jnp.exp(s_loc - m)
    e_ctx = jnp.exp(s_ctx - m)
    den = (jnp.sum(e_loc, axis=-1, keepdims=True) + jnp.sum(e_ctx, axis=-1, keepdims=True)
           + jnp.exp(sk - m))
    o = (_dot(e_loc.astype(BF16), v_loc) + _dot(e_ctx.astype(BF16), cv_ref[...].astype(BF16))) / den
    for g in range(group):
        o_ref[:, g * HEAD_DIM:(g + 1) * HEAD_DIM] = o[g * QBLK:(g + 1) * QBLK].astype(o_ref.dtype)


def window_attention(qk_rot, u, vb0, bsz, n_tok, n_kv, group, ck, cv, sink):
    assert WIN == QBLK
    nb = n_tok // QBLK
    nq = n_kv * group
    past = ck.shape[2]
    prv = lambda i: jnp.maximum(i - 1, 0)
    nxt = lambda i: jnp.minimum(i + 1, nb - 1)
    blk = lambda f, c0: pl.BlockSpec((QBLK, HEAD_DIM), lambda b, kv, i, s: (b * nb + f(i), c0 + kv))
    same = lambda i: i
    cache = pl.BlockSpec((None, None, past, HEAD_DIM), lambda b, kv, i, s: (b, kv, 0, 0))
    grid_spec = pltpu.PrefetchScalarGridSpec(
        num_scalar_prefetch=1,
        grid=(bsz, n_kv, nb),
        in_specs=[
            pl.BlockSpec((QBLK, group * HEAD_DIM), lambda b, kv, i, s: (b * nb + i, kv)),
            blk(prv, nq), blk(same, nq), blk(nxt, nq),
            blk(prv, vb0), blk(same, vb0), blk(nxt, vb0),
            cache, cache,
        ],
        out_specs=pl.BlockSpec((QBLK, group * HEAD_DIM), lambda b, kv, i, s: (b * nb + i, kv)),
    )
    return pl.pallas_call(
        functools.partial(_win_attn_kernel, group, n_tok),
        grid_spec=grid_spec,
        out_shape=jax.ShapeDtypeStruct((bsz * n_tok, nq * HEAD_DIM), BF16),
        compiler_params=_cparams("parallel", "parallel", "parallel"),
        name="window_attention",
    )(sink.astype(F32), qk_rot, qk_rot, qk_rot, qk_rot, u, u, u, ck, cv)


NB_QROWS = 8
NB_QTOK = NB_QROWS * GRID_W
NB_KTOK = 3 * NB_QTOK


def nb_tables(rpb, rows):
    kh = min(NB_ROWS, rows)
    ql = np.arange(NB_QTOK)
    kl = np.arange(NB_KTOK)
    r_rel, c = ql // GRID_W, ql % GRID_W
    kr_rel, kc = kl // GRID_W - NB_QROWS, kl % GRID_W
    n_heads = rpb.shape[0]
    gcol = np.arange(GRID_W)
    col_idx = np.clip(gcol[None, :] - gcol[:, None] + NB_COLS - 1, 0, 2 * NB_COLS - 2)
    t_col = jnp.take(rpb.astype(F32), jnp.asarray(col_idx.reshape(-1)), axis=2)
    t_col = t_col.reshape(n_heads, 2 * NB_ROWS - 1, GRID_W * GRID_W)
    qr = np.arange(NB_QROWS)
    krr = np.arange(3 * NB_QROWS) - NB_QROWS
    row_idx = np.clip(krr[None, :] - qr[:, None] + NB_ROWS - 1, 0, 2 * NB_ROWS - 2)
    bias = jnp.take(t_col, jnp.asarray(row_idx.reshape(-1)), axis=1)
    bias = bias.reshape(n_heads, NB_QROWS, 3 * NB_QROWS, GRID_W, GRID_W)
    bias = jnp.transpose(bias, (0, 1, 3, 2, 4)).reshape(n_heads, NB_QTOK, NB_KTOK)
    win_start = np.clip(c - NB_COLS // 2, 0, GRID_W - NB_COLS)
    col_ok = (kc[None, :] >= win_start[:, None]) & (kc[None, :] < win_start[:, None] + NB_COLS)
    masks = []
    for j in range(rows // NB_QROWS):
        r = j * NB_QROWS + r_rel
        kr = j * NB_QROWS + kr_rel
        row_start = np.clip(r - kh // 2, 0, rows - kh)
        row_ok = (kr[None, :] >= row_start[:, None]) & (kr[None, :] < row_start[:, None] + kh)
        masks.append(row_ok & col_ok)
    return bias, jnp.asarray(np.stack(masks).astype(np.float32))


def _nb_attn_kernel(q_ref, kp_ref, kc_ref, kn_ref, vp_ref, vc_ref, vn_ref, ck_ref, cv_ref,
                    bias_ref, mask_ref, o_ref):
    q = q_ref[...] * ATT_SCALE
    k_loc = jnp.concatenate([kp_ref[...], kc_ref[...], kn_ref[...]], axis=0)
    v_loc = jnp.concatenate([vp_ref[...], vc_ref[...], vn_ref[...]], axis=0).astype(BF16)
    s_loc = jnp.where(mask_ref[...] > 0.0, _dot3_nt(q, k_loc) + bias_ref[...], NEG)
    s_ctx = _dot3_nt(q, ck_ref[...])
    m = jnp.maximum(jnp.max(s_loc, axis=-1, keepdims=True), jnp.max(s_ctx, axis=-1, keepdims=True))
    e_loc = jnp.exp(s_loc - m)
    e_ctx = jnp.exp(s_ctx - m)
    den = jnp.sum(e_loc, axis=-1, keepdims=True) + jnp.sum(e_ctx, axis=-1, keepdims=True)
    o = _dot(e_loc.astype(BF16), v_loc) + _dot(e_ctx.astype(BF16), cv_ref[...].astype(BF16))
    o_ref[...] = (o / den).astype(o_ref.dtype)


def neighbourhood_attention(u, cols, bsz, n_tok, n_heads, ck, cv, rpb):
    q0, k0, v0 = cols
    rows = n_tok // GRID_W
    assert rows % NB_QROWS == 0 and rows >= NB_ROWS
    nj = rows // NB_QROWS
    past = ck.shape[2]
    bias, mask = nb_tables(rpb, rows)
    prv = lambda j: jnp.maximum(j - 1, 0)
    nxt = lambda j: jnp.minimum(j + 1, nj - 1)
    same = lambda j: j
    blk = lambda f, c0: pl.BlockSpec((NB_QTOK, HEAD_DIM), lambda b, h, j: (b * nj + f(j), c0 + h))
    cache = pl.BlockSpec((None, None, past, HEAD_DIM), lambda b, h, j: (b, h, 0, 0))
    return pl.pallas_call(
        _nb_attn_kernel,
        grid=(bsz, n_heads, nj),
        in_specs=[
            blk(same, q0),
            blk(prv, k0), blk(same, k0), blk(nxt, k0),
            blk(prv, v0), blk(same, v0), blk(nxt, v0),
            cache, cache,
            pl.BlockSpec((None, NB_QTOK, NB_KTOK), lambda b, h, j: (h, 0, 0)),
            pl.BlockSpec((None, NB_QTOK, NB_KTOK), lambda b, h, j: (j, 0, 0)),
        ],
        out_specs=pl.BlockSpec((NB_QTOK, HEAD_DIM), lambda b, h, j: (b * nj + j, h)),
        out_shape=jax.ShapeDtypeStruct((bsz * n_tok, n_heads * HEAD_DIM), BF16),
        compiler_params=_cparams("parallel", "parallel", "parallel"),
        name="neighbourhood_attention",
    )(u, u, u, u, u, u, u, ck, cv, bias, mask)


def _merge_kernel(ya_ref, yb_ref, yc_ref, wa_ref, wb_ref, wc_ref, ga_ref, gb_ref, gc_ref, o_ref):
    m = _sigmoid(ga_ref[...]) * _dot(ya_ref[...], wa_ref[...])
    m = m + _sigmoid(gb_ref[...]) * _dot(yb_ref[...], wb_ref[...])
    m = m + _sigmoid(gc_ref[...]) * _dot(yc_ref[...], wc_ref[...])
    o_ref[...] = m.astype(o_ref.dtype)


def merge_branches(ya, yb, yc, wa, wb, wc, u, gate_col0, tm=512, tn=512):
    n, aw = ya.shape
    bw, cw = wb.shape[0], wc.shape[0]
    d = wa.shape[1]
    tm = _pick(n, (tm, 256, 128))
    tn = _pick(d, (tn, 256, 128))
    assert gate_col0 % tn == 0
    g0 = gate_col0 // tn
    nd = d // tn
    gate = lambda gi: pl.BlockSpec((tm, tn), lambda i, j: (i, g0 + gi * nd + j))
    return pl.pallas_call(
        _merge_kernel,
        grid=(n // tm, nd),
        in_specs=[
            pl.BlockSpec((tm, aw), lambda i, j: (i, 0)),
            pl.BlockSpec((tm, bw), lambda i, j: (i, 0)),
            pl.BlockSpec((tm, cw), lambda i, j: (i, 0)),
            pl.BlockSpec((aw, tn), lambda i, j: (0, j)),
            pl.BlockSpec((bw, tn), lambda i, j: (0, j)),
            pl.BlockSpec((cw, tn), lambda i, j: (0, j)),
            gate(0), gate(1), gate(2),
        ],
        out_specs=pl.BlockSpec((tm, tn), lambda i, j: (i, j)),
        out_shape=jax.ShapeDtypeStruct((n, d), BF16),
        compiler_params=_cparams("parallel", "parallel"),
        name="merge_branches",
    )(ya, yb, yc, wa, wb, wc, u, u, u)


def _post_norm(alpha, x, gate, y, g, b):
    r = alpha * x + gate * y
    return _ln_rows(r) * g + b


def _outproj_ln_kernel(alpha, nk, m_ref, w_ref, x_ref, gate_ref, g_ref, b_ref, o_ref):
    k = pl.program_id(1)

    @pl.when(k == 0)
    def _():
        o_ref[...] = _dot(m_ref[...], w_ref[...])

    @pl.when(k > 0)
    def _():
        o_ref[...] += _dot(m_ref[...], w_ref[...])

    @pl.when(k == nk - 1)
    def _():
        rows = o_ref.shape[0]
        step = min(rows, 128)
        for r0 in range(0, rows, step):
            sl = slice(r0, r0 + step)
            o_ref[sl, :] = _post_norm(alpha, x_ref[sl, :], gate_ref[...], o_ref[sl, :],
                                      g_ref[...], b_ref[...])


def _tile_mod_spec(mod, d, tm, seq_len):
    if mod.shape[0] == 1:
        return pl.BlockSpec((None, 1, d), lambda i, *_: (0, 0, 0))
    return pl.BlockSpec((None, 1, d), lambda i, *_: (i * tm // seq_len, 0, 0))


def outproj_postnorm(m, w_out, x, gate, ln_g, ln_b, alpha, seq_len, tm=512, tk=512):
    n, d = x.shape
    kdim = m.shape[1]
    tm = _pick(n if gate.shape[0] == 1 else seq_len, (tm, 256, 128))
    tk = _pick(kdim, (tk, 256, 128))
    nk = kdim // tk
    vec = pl.BlockSpec((1, d), lambda i, k: (0, 0))
    return pl.pallas_call(
        functools.partial(_outproj_ln_kernel, alpha, nk),
        grid=(n // tm, nk),
        in_specs=[
            pl.BlockSpec((tm, tk), lambda i, k: (i, k)),
            pl.BlockSpec((tk, d), lambda i, k: (k, 0)),
            pl.BlockSpec((tm, d), lambda i, k: (i, 0), pipeline_mode=pl.Buffered(1)),
            _tile_mod_spec(gate, d, tm, seq_len),
            vec, vec,
        ],
        out_specs=pl.BlockSpec((tm, d), lambda i, k: (i, 0)),
        out_shape=jax.ShapeDtypeStruct((n, d), F32),
        compiler_params=_cparams("parallel", "arbitrary"),
        name="outproj_postnorm",
    )(m, w_out, x, gate, ln_g, ln_b)


def _select_kernel(cap, n, aff_ref, mask_ref):
    bits = lax.bitcast_convert_type(aff_ref[...], jnp.int32)
    n_exp = bits.shape[0]
    capf = jnp.float32(cap)

    def count(pred):
        return jnp.sum(pred.astype(F32), axis=1, keepdims=True)

    def value_step(_, carry):
        lo, hi = carry
        mid = lo + ((hi - lo + 1) >> 1)
        ok = count(bits >= mid) >= capf
        return jnp.where(ok, mid, lo), jnp.where(ok, hi, mid - 1)

    lo0 = jnp.zeros((n_exp, 1), jnp.int32)
    hi0 = jnp.full((n_exp, 1), 0x7F800000, jnp.int32)
    thr, _ = lax.fori_loop(0, 32, value_step, (lo0, hi0))
    gt = bits > thr
    eq = bits == thr
    need = capf - count(gt)
    idx = _iota(bits.shape, 1)

    def index_step(_, carry):
        lo, hi = carry
        mid = (lo + hi) >> 1
        ok = count(eq & (idx < mid)) >= need
        return jnp.where(ok, lo, mid), jnp.where(ok, mid, hi)

    _, bound = lax.fori_loop(0, int(np.ceil(np.log2(n))) + 1, index_step,
                             (jnp.zeros((n_exp, 1), jnp.int32), jnp.full((n_exp, 1), n, jnp.int32)))
    mask_ref[...] = (gt | (eq & (idx < bound))).astype(jnp.int32)


def expert_choice_mask(aff_t, cap):
    n_exp, n = aff_t.shape
    return pl.pallas_call(
        functools.partial(_select_kernel, cap, n),
        out_shape=jax.ShapeDtypeStruct((n_exp, n), jnp.int32),
        compiler_params=pltpu.CompilerParams(vmem_limit_bytes=VMEM_LIMIT_BYTES),
        name="expert_choice_mask",
    )(aff_t)


def _count_le(sorted_vals, x):
    return jnp.sum((sorted_vals[None, :] <= x[:, None]).astype(jnp.int32), axis=1)


def _visit_list(nvis, first_blk, n_work):
    na, nb = nvis.shape
    flat = nvis.reshape(-1)
    off_end = jnp.cumsum(flat)
    off_start = off_end - flat
    total = off_end[-1]
    w = jnp.minimum(jnp.arange(n_work, dtype=jnp.int32), total - 1)
    idx = jnp.minimum(_count_le(off_end, w), na * nb - 1)
    blk = first_blk.reshape(-1)[idx] + (w - off_start[idx])
    valid = (jnp.arange(n_work) < total).astype(jnp.int32)
    return idx // nb, idx % nb, blk.astype(jnp.int32), valid


MOE_GROUP = 4
NO_SLOT = 1 << 28


def _group_items(key, valid, n_keys, group, n_steps):
    n_items = key.shape[0]
    cnt = jnp.zeros((n_keys,), jnp.int32).at[key].add(valid)
    start = jnp.cumsum(cnt) - cnt
    per_key = (cnt + group - 1) // group
    s_end = jnp.cumsum(per_key)
    s_start = s_end - per_key
    total = s_end[-1]
    step = jnp.arange(n_steps, dtype=jnp.int32)
    real = step < total
    sc = jnp.minimum(step, total - 1)
    k = jnp.minimum(_count_le(s_end, sc), n_keys - 1)
    j = sc - s_start[k]
    within = j[None, :] * group + jnp.arange(group, dtype=jnp.int32)[:, None]
    ok = (within < cnt[k][None, :]) & real[None, :]
    item = jnp.clip(start[k][None, :] + jnp.minimum(within, cnt[k][None, :] - 1), 0, n_items - 1)
    first = ((j == 0) & real).astype(jnp.int32)
    last = ((j == per_key[k] - 1) & real).astype(jnp.int32)
    return item, ok.astype(jnp.int32), first, last


def routing_plan(mask, cap):
    n_exp, n = mask.shape
    nt = n // TOK_TILE
    nsb = cap // SLOT_BLK
    pos = jnp.cumsum(mask, axis=1) - mask
    posm = jnp.where(mask > 0, pos, -1).astype(jnp.int32)
    cnt = mask.reshape(n_exp, nt, TOK_TILE).sum(-1)
    cend = jnp.cumsum(cnt, axis=1)
    cstart = cend - cnt
    sb_lo = jnp.minimum(cstart // SLOT_BLK, nsb - 1)
    nvis = jnp.where(cnt > 0, (cend - 1) // SLOT_BLK - sb_lo + 1, 0)
    n_work = n_exp * (nsb + nt)
    n_blk = n_exp * nsb
    grp = MOE_GROUP
    e, t, sb, valid = _visit_list(nvis, sb_lo, n_work)
    blk = e * nsb + sb
    item, ok, first, last = _group_items(blk, valid, n_blk, grp, n_work // grp + n_blk + 1)
    blk_g = blk[item[0]]
    dispatch = (blk_g // nsb, blk_g, first, last, t[item], jnp.where(ok > 0, sb[item] * SLOT_BLK, NO_SLOT))
    nvis_t = nvis.T.at[:, 0].max(1)
    t2, e2, sb2, valid2 = _visit_list(nvis_t, sb_lo.T, n_work + nt)
    item, ok, first, last = _group_items(t2, valid2, nt, grp, (n_work + nt) // grp + nt + 1)
    combine = (t2[item[0]], first, last, e2[item], e2[item] * nsb + sb2[item],
               jnp.where(ok > 0, sb2[item] * SLOT_BLK, NO_SLOT))
    return posm, dispatch, combine


def _dispatch_kernel(group, e_ref, blk_ref, first_ref, last_ref, tile_ref, base_ref, *refs):
    posm_refs, tok_refs, aff_refs = refs[:group], refs[group:2 * group], refs[2 * group:3 * group]
    o_ref, gate_ref, acc_ref, accg_ref = refs[3 * group:]
    w = pl.program_id(0)
    rows = _iota((SLOT_BLK, TOK_TILE), 0)
    tot, totg = None, None
    for q in range(group):
        onehot = ((rows + base_ref[q, w]) == posm_refs[q][...]).astype(BF16)
        part = _dot(onehot, tok_refs[q][...])
        partg = _dot_exact_lhs(onehot, aff_refs[q][...])
        tot = part if tot is None else tot + part
        totg = partg if totg is None else totg + partg

    @pl.when(first_ref[w] == 1)
    def _():
        acc_ref[...] = tot
        accg_ref[...] = totg

    @pl.when(first_ref[w] == 0)
    def _():
        acc_ref[...] += tot
        accg_ref[...] += totg

    @pl.when(last_ref[w] == 1)
    def _():
        o_ref[...] = acc_ref[...].astype(o_ref.dtype)
        gate_ref[...] = accg_ref[...]


def moe_dispatch(tokens, aff, posm, plan, cap):
    n, d = tokens.shape
    n_exp = posm.shape[0]
    grp = MOE_GROUP
    n_steps = plan[0].shape[0]
    pos_spec = lambda q: pl.BlockSpec((None, 1, TOK_TILE), lambda w, e, b, f, l, t, s: (e[w], 0, t[q, w]))
    tok_spec = lambda q, width: pl.BlockSpec((TOK_TILE, width), lambda w, e, b, f, l, t, s: (t[q, w], 0))
    out_map = lambda w, e, b, f, l, t, s: (b[w], 0)
    grid_spec = pltpu.PrefetchScalarGridSpec(
        num_scalar_prefetch=6,
        grid=(n_steps,),
        in_specs=([pos_spec(q) for q in range(grp)] + [tok_spec(q, d) for q in range(grp)]
                  + [tok_spec(q, LANE) for q in range(grp)]),
        out_specs=[pl.BlockSpec((SLOT_BLK, d), out_map), pl.BlockSpec((SLOT_BLK, LANE), out_map)],
        scratch_shapes=[pltpu.VMEM((SLOT_BLK, d), F32), pltpu.VMEM((SLOT_BLK, LANE), F32)],
    )
    posm3 = posm.reshape(n_exp, 1, n)
    return pl.pallas_call(
        functools.partial(_dispatch_kernel, grp),
        grid_spec=grid_spec,
        out_shape=[jax.ShapeDtypeStruct((n_exp * cap, d), BF16),
                   jax.ShapeDtypeStruct((n_exp * cap, LANE), F32)],
        compiler_params=_cparams("arbitrary"),
        name="moe_dispatch",
    )(*plan, *([posm3] * grp), *([tokens] * grp), *([aff] * grp))


def _ffn_up_kernel(x_ref, w1_ref, w3_ref, o_ref):
    x = x_ref[...]
    a = _dot(x, w1_ref[...])
    o_ref[...] = (a * _sigmoid(a) * _dot(x, w3_ref[...])).astype(o_ref.dtype)


def _ffn_down_kernel(h_ref, w2_ref, aff_ref, o_ref):
    aff = aff_ref[...]
    gval = jnp.sum(jnp.where(_iota(aff.shape, 1) == pl.program_id(0), aff, 0.0), axis=1, keepdims=True)
    o_ref[...] = (_dot(h_ref[...], w2_ref[...]) * gval).astype(o_ref.dtype)


def expert_ffn(xe, aff_rows, w1, w3, w2, layer, cap, tn=512):
    _, n_exp, d, ff = w1.shape
    tm = _pick(cap, (1024, 512, 256, 128))
    nm = cap // tm
    tf = _pick(ff, (tn, 256, 128))
    hid = pl.pallas_call(
        _ffn_up_kernel,
        grid=(n_exp, nm, ff // tf),
        in_specs=[
            pl.BlockSpec((tm, d), lambda e, i, j: (e * nm + i, 0)),
            pl.BlockSpec((None, None, d, tf), lambda e, i, j: (layer, e, 0, j)),
            pl.BlockSpec((None, None, d, tf), lambda e, i, j: (layer, e, 0, j)),
        ],
        out_specs=pl.BlockSpec((tm, tf), lambda e, i, j: (e * nm + i, j)),
        out_shape=jax.ShapeDtypeStruct((n_exp * cap, ff), BF16),
        compiler_params=_cparams("parallel", "parallel", "parallel"),
        name="expert_ffn_up",
    )(xe, w1, w3)
    td = _pick(d, (tn, 256, 128))
    return pl.pallas_call(
        _ffn_down_kernel,
        grid=(n_exp, nm, d // td),
        in_specs=[
            pl.BlockSpec((tm, ff), lambda e, i, j: (e * nm + i, 0)),
            pl.BlockSpec((None, None, ff, td), lambda e, i, j: (layer, e, 0, j)),
            pl.BlockSpec((tm, LANE), lambda e, i, j: (e * nm + i, 0)),
        ],
        out_specs=pl.BlockSpec((tm, td), lambda e, i, j: (e * nm + i, j)),
        out_shape=jax.ShapeDtypeStruct((n_exp * cap, d), BF16),
        compiler_params=_cparams("parallel", "parallel", "parallel"),
        name="expert_ffn_down",
    )(hid, w2, aff_rows)


def _combine_kernel(alpha, group, tile_ref, first_ref, last_ref, e_ref, blk_ref, base_ref,
                    posm_ref, *refs):
    ye_refs = refs[:group]
    x_ref, gate_ref, g_ref, b_ref, o_ref, acc_ref = refs[group:]
    w = pl.program_id(0)
    posm = posm_ref[...].astype(F32)
    lane = _iota(posm.shape, 1)
    cols = _iota((TOK_TILE, SLOT_BLK), 1)
    tot = None
    for q in range(group):
        col = jnp.sum(jnp.where(lane == e_ref[q, w], posm, 0.0), axis=1, keepdims=True)
        onehot = ((cols + base_ref[q, w]).astype(F32) == col).astype(BF16)
        part = _dot(onehot, ye_refs[q][...])
        tot = part if tot is None else tot + part

    @pl.when(first_ref[w] == 1)
    def _():
        acc_ref[...] = tot

    @pl.when(first_ref[w] == 0)
    def _():
        acc_ref[...] += tot

    @pl.when(last_ref[w] == 1)
    def _():
        o_ref[...] = _post_norm(alpha, x_ref[...], gate_ref[...], acc_ref[...], g_ref[...], b_ref[...])


def moe_combine_postnorm(ye, posm_t, plan, x, gate, ln_g, ln_b, alpha, seq_len):
    n, d = x.shape
    n_exp = posm_t.shape[1]
    grp = MOE_GROUP
    n_steps = plan[0].shape[0]
    assert seq_len % TOK_TILE == 0
    tile = lambda w, t, *_: (t[w], 0)
    if gate.shape[0] == 1:
        gate_map = lambda w, t, *_: (0, 0, 0)
    else:
        gate_map = lambda w, t, *_: (t[w] * TOK_TILE // seq_len, 0, 0)
    vec = pl.BlockSpec((1, d), lambda w, *_: (0, 0))
    ye_spec = lambda q: pl.BlockSpec((SLOT_BLK, d), lambda w, t, f, l, e, b, s: (b[q, w], 0))
    grid_spec = pltpu.PrefetchScalarGridSpec(
        num_scalar_prefetch=6,
        grid=(n_steps,),
        in_specs=([pl.BlockSpec((TOK_TILE, n_exp), tile)] + [ye_spec(q) for q in range(grp)]
                  + [pl.BlockSpec((TOK_TILE, d), tile), pl.BlockSpec((None, 1, d), gate_map), vec, vec]),
        out_specs=pl.BlockSpec((TOK_TILE, d), tile),
        scratch_shapes=[pltpu.VMEM((TOK_TILE, d), F32)],
    )
    return pl.pallas_call(
        functools.partial(_combine_kernel, alpha, grp),
        grid_spec=grid_spec,
        out_shape=jax.ShapeDtypeStruct((n, d), F32),
        compiler_params=_cparams("arbitrary"),
        name="moe_combine_postnorm",
    )(*plan, posm_t, *([ye] * grp), x, gate, ln_g, ln_b)


def ec_moe_postnorm(x, mods_shift, mods_scale, gate, router_w, w1, w3, w2, layer, ln_g, ln_b, alpha):
    bsz, t, d = x.shape
    n = bsz * t
    n_exp = router_w.shape[1]
    cap = max(1, CAP_FACTOR * n // n_exp)
    assert cap % SLOT_BLK == 0 and n % TOK_TILE == 0
    h, aff = ln_modulate_router(x, mods_shift, mods_scale, router_w)
    h = h.reshape(n, d)
    aff = aff.reshape(n, LANE)
    mask = expert_choice_mask(aff[:, :n_exp].T, cap)
    posm, dispatch, combine = routing_plan(mask, cap)
    xe, aff_rows = moe_dispatch(h, aff, posm, dispatch, cap)
    ye = expert_ffn(xe, aff_rows, w1, w3, w2, layer, cap)
    out = moe_combine_postnorm(ye, posm.T, combine, x.reshape(n, d), gate, ln_g, ln_b, alpha, t)
    return out.reshape(bsz, t, d)


def prep_rwkv_params(rw, aw):
    a_cols = rw["mu"].shape[0]
    ap = round_up(a_cols, LANE)
    gp = ap - 3 * aw - 2 * LANE
    assert rw["w2"].shape[:2] == (2, LANE // 2) and rw["a2"].shape[:2] == (2, LANE // 2)
    assert gp >= rw["g2"].shape[0] and aw % LANE == 0
    row = lambda a: a.reshape(1, aw)
    return dict(
        mu=jnp.pad(rw["mu"], (0, ap - a_cols)).reshape(1, ap),
        w0=rw["w0"], w2=rw["w2"].reshape(LANE, aw), a0=rw["a0"], a2=rw["a2"].reshape(LANE, aw),
        g2=jnp.pad(rw["g2"], ((0, gp - rw["g2"].shape[0]), (0, 0))),
        k_k=row(rw["k_k"]), k_a=row(rw["k_a"]), r_k=row(rw["r_k"]),
        ln_w=row(rw["ln_w"]), ln_b=row(rw["ln_b"]), ap=ap, aw=aw)


def states_to_pairs(s):
    bsz, h = s.shape[:2]
    st = jnp.swapaxes(s, -1, -2).reshape(bsz, h // 2, 2, A_HEAD, A_HEAD)
    z = jnp.zeros_like(st[:, :, 0])
    top = jnp.concatenate([st[:, :, 0], z], axis=-1)
    bot = jnp.concatenate([z, st[:, :, 1]], axis=-1)
    return jnp.concatenate([top, bot], axis=-2)


def pairs_to_states(sp):
    bsz = sp.shape[0]
    st = jnp.stack([sp[:, :, :A_HEAD, :A_HEAD], sp[:, :, A_HEAD:, A_HEAD:]], axis=2)
    return jnp.swapaxes(st.reshape(bsz, -1, A_HEAD, A_HEAD), -1, -2)


def rwkv_branch(u, bsz, t, rwp, s0f, s0b):
    aw, ap = rwp["aw"], rwp["ap"]
    n = bsz * t
    r_, v_, kk, lwf, lwb, bf, bb, kdf, kdb, bonus, g = rwkv_prep(u, t, rwp, aw, ap)
    sh = lambda a: a.reshape(bsz, t, aw)
    yf, sf = rwkv_scan(sh(lwf), sh(kk), sh(bf), sh(kdf), sh(r_), sh(v_), states_to_pairs(s0f), False)
    yb, sb = rwkv_scan(sh(lwb), sh(kk), sh(bb), sh(kdb), sh(r_), sh(v_), states_to_pairs(s0b), True)
    ya = rwkv_post(yf.reshape(n, aw), yb.reshape(n, aw), bonus, g, rwp["ln_w"], rwp["ln_b"])
    return ya, pairs_to_states(sf), pairs_to_states(sb)


def _mod_rows(m):
    return [m[:, i][:, None, :] for i in range(6)]


def kernel(x_prompt, x_sample, cache_win_k, cache_win_v, cache_nb_k, cache_nb_v, state_rwkv_fwd, state_rwkv_bwd, c, c_ctx, ada_w, ada_b, w_in, rwkv_mu, rwkv_w0, rwkv_w2, rwkv_a0, rwkv_a2, rwkv_g2, rwkv_kk, rwkv_ka, rwkv_rk, rwkv_lnx_w, rwkv_lnx_b, win_sink, nb_rpb, w_br_a, w_br_b, w_br_c, w_out, ln1_g, ln1_b, ln2_g, ln2_b, router_w, exp_w1, exp_w3, exp_w2):
    depth, d, in_w = w_in.shape
    bsz, seq, _ = x_prompt.shape
    dbsz, dseq, _ = x_sample.shape
    aw = rwkv_w0.shape[-1]
    a_heads = aw // A_HEAD
    a_cols = rwkv_mu.shape[-1]
    ap = round_up(a_cols, LANE)
    n_b = win_sink.shape[-1]
    n_kv = cache_win_k.shape[3]
    group = n_b // n_kv
    n_c = nb_rpb.shape[1]
    bw, cw = n_b * HEAD_DIM, n_c * HEAD_DIM
    alpha = float((2 * depth) ** 0.25)
    assert in_w == a_cols + bw + 2 * n_kv * HEAD_DIM + 3 * cw + 3 * d

    qb0 = ap // LANE
    kb0 = qb0 + n_b
    vb0 = kb0 + n_kv
    qn0 = vb0 + n_kv
    kn0 = qn0 + n_c
    vn0 = kn0 + n_c
    gate_col0 = (vn0 + n_c) * LANE

    rows = 1 + dbsz
    cvecs = jnp.pad(jnp.concatenate([c_ctx[None], c], axis=0), ((0, round_up(rows, 16) - rows), (0, 0)))
    mods = ada_mods(cvecs, ada_w, ada_b).reshape(depth, -1, 6, d)

    xp, xs = x_prompt, x_sample
    np_tok, ns_tok = bsz * seq, dbsz * dseq
    win_k, win_v, nb_k, nb_v, st_f, st_b = [], [], [], [], [], []
    zero_state = jnp.zeros((bsz, a_heads, A_HEAD, A_HEAD), F32)
    w_cat = jnp.concatenate(
        [w_in[:, :, :a_cols].astype(BF16), jnp.zeros((depth, d, ap - a_cols), BF16),
         w_in[:, :, a_cols:].astype(BF16)], axis=2)
    w1, w3, w2 = exp_w1.astype(BF16), exp_w3.astype(BF16), exp_w2.astype(BF16)
    for l in range(depth):
        rwp = prep_rwkv_params(
            {"mu": rwkv_mu[l], "w0": rwkv_w0[l], "w2": rwkv_w2[l], "a0": rwkv_a0[l], "a2": rwkv_a2[l],
             "g2": rwkv_g2[l], "k_k": rwkv_kk[l], "k_a": rwkv_ka[l], "r_k": rwkv_rk[l],
             "ln_w": rwkv_lnx_w[l], "ln_b": rwkv_lnx_b[l]}, aw)
        wa, wb, wc = w_br_a[l].astype(BF16), w_br_b[l].astype(BF16), w_br_c[l].astype(BF16)
        wo = w_out[l].astype(BF16)
        g1, b1 = ln1_g[l].reshape(1, d), ln1_b[l].reshape(1, d)
        g2, b2 = ln2_g[l].reshape(1, d), ln2_b[l].reshape(1, d)

        sh1, sc1, gt1, sh2, sc2, gt2 = _mod_rows(mods[l, 0:1])
        h = ln_modulate(xp, sh1, sc1).reshape(np_tok, d)
        u = matmul(h, w_cat, l, name="in_proj")
        ya, s_f, s_b = rwkv_branch(u, bsz, seq, rwp, zero_state, zero_state)
        yb = ctx_attention(u, bsz, seq, qb0, kb0, vb0, n_b, group, win_sink[l])
        yc = ctx_attention(u, bsz, seq, qn0, kn0, vn0, n_c, 1, None)
        m = merge_branches(ya, yb, yc, wa, wb, wc, u, gate_col0)
        x1 = outproj_postnorm(m, wo, xp.reshape(np_tok, d), gt1, g1, b1, alpha, seq)
        xp = ec_moe_postnorm(x1.reshape(bsz, seq, d), sh2, sc2, gt2, router_w[l], w1, w3, w2, l, g2, b2, alpha)
        cols = lambda c0, nh: u[:, c0 * LANE:(c0 + nh) * LANE].reshape(bsz, seq, nh, HEAD_DIM)
        win_k.append(cols(kb0, n_kv))
        win_v.append(cols(vb0, n_kv))
        nb_k.append(cols(kn0, n_c))
        nb_v.append(cols(vn0, n_c))
        st_f.append(s_f)
        st_b.append(s_b)

        sh1, sc1, gt1, sh2, sc2, gt2 = _mod_rows(mods[l, 1:1 + dbsz])
        h = ln_modulate(xs, sh1, sc1).reshape(ns_tok, d)
        u = matmul(h, w_cat, l, name="in_proj")
        ya, _, _ = rwkv_branch(u, dbsz, dseq, rwp, state_rwkv_fwd[:, l], state_rwkv_bwd[:, l])
        qk_rot = rope(u, dbsz, dseq, qb0, n_b + n_kv)
        heads_first = lambda a: jnp.swapaxes(a[:, l], 1, 2)
        yb = window_attention(qk_rot, u, vb0, dbsz, dseq, n_kv, group,
                              heads_first(cache_win_k), heads_first(cache_win_v), win_sink[l])
        yc = neighbourhood_attention(u, (qn0, kn0, vn0), dbsz, dseq, n_c,
                                     heads_first(cache_nb_k), heads_first(cache_nb_v), nb_rpb[l])
        m = merge_branches(ya, yb, yc, wa, wb, wc, u, gate_col0)
        x1 = outproj_postnorm(m, wo, xs.reshape(ns_tok, d), gt1, g1, b1, alpha, dseq)
        xs = ec_moe_postnorm(x1.reshape(dbsz, dseq, d), sh2, sc2, gt2, router_w[l], w1, w3, w2, l, g2, b2, alpha)

    stack = lambda xs_: jnp.stack(xs_, axis=1)
    return (xp, xs, stack(win_k), stack(win_v), stack(nb_k), stack(nb_v), stack(st_f), stack(st_b))
```

```python
import functools

import numpy as np
import jax
import jax.numpy as jnp
from jax import lax
from jax.experimental import pallas as pl
from jax.experimental.pallas import tpu as pltpu

F32 = jnp.float32
BF16 = jnp.bfloat16

HEAD_DIM = 128
A_HEAD = 64
GRID_W = 64
WIN = 128
QBLK = 128
NB_ROWS = 8
NB_COLS = 16
CAP_FACTOR = 2
ROPE_THETA = 10000.0
LN_EPS = 1e-6
GN_EPS = 64e-5
NEG = -1e30

LANE = 128
SUBLANE = 8
VMEM_LIMIT_BYTES = 56 * 1024 * 1024

SCAN_CHUNK = 64
SLOT_BLK = 128
TOK_TILE = 256


def _cparams(*sem):
    return pltpu.CompilerParams(dimension_semantics=sem, vmem_limit_bytes=VMEM_LIMIT_BYTES)


def _dot(a, b):
    return jnp.dot(a, b, preferred_element_type=F32)


def _dot_nt(a, b):
    return lax.dot_general(a, b, (((1,), (1,)), ((), ())), preferred_element_type=F32)


def _split2(x):
    hi = x.astype(BF16)
    lo = (x - hi.astype(F32)).astype(BF16)
    return hi, lo


def _split3(x):
    hi = x.astype(BF16)
    r1 = x - hi.astype(F32)
    mid = r1.astype(BF16)
    lo = (r1 - mid.astype(F32)).astype(BF16)
    return hi, mid, lo


def _dot3(a, b):
    ah, al = _split2(a)
    bh, bl = _split2(b)
    return _dot(ah, bh) + (_dot(ah, bl) + _dot(al, bh))


def _dot3_nt(a, b):
    ah, al = _split2(a)
    bh, bl = _split2(b)
    return _dot_nt(ah, bh) + (_dot_nt(ah, bl) + _dot_nt(al, bh))


def _dot_exact_lhs(a_bf16, b):
    bh, bm, bl = _split3(b)
    return _dot(a_bf16, bh) + (_dot(a_bf16, bm) + _dot(a_bf16, bl))


def _sigmoid(x):
    return 1.0 / (1.0 + jnp.exp(-x))


def _iota(shape, dim):
    return lax.broadcasted_iota(jnp.int32, shape, dim)


def _ada_kernel(c_ref, w_ref, b_ref, o_ref):
    c = c_ref[...]
    a = c * _sigmoid(c)
    o_ref[...] = _dot3(a, w_ref[...]) + b_ref[...]


def ada_mods(cvecs, ada_w, ada_b, tn=512):
    depth, d, n6 = ada_w.shape
    rows = cvecs.shape[0]
    return pl.pallas_call(
        _ada_kernel,
        grid=(depth, n6 // tn),
        in_specs=[
            pl.BlockSpec((rows, d), lambda l, j: (0, 0)),
            pl.BlockSpec((None, d, tn), lambda l, j: (l, 0, j)),
            pl.BlockSpec((None, 1, tn), lambda l, j: (l, 0, j)),
        ],
        out_specs=pl.BlockSpec((None, rows, tn), lambda l, j: (l, 0, j)),
        out_shape=jax.ShapeDtypeStruct((depth, rows, n6), F32),
        compiler_params=_cparams("parallel", "parallel"),
        name="ada_mods",
    )(cvecs, ada_w, ada_b.reshape(depth, 1, n6))


def _ln_rows(x):
    mu = jnp.mean(x, axis=-1, keepdims=True)
    xc = x - mu
    var = jnp.mean(xc * xc, axis=-1, keepdims=True)
    return xc * lax.rsqrt(var + LN_EPS)


def _lnmod_kernel(x_ref, sh_ref, sc_ref, o_ref):
    y = _ln_rows(x_ref[...])
    o_ref[...] = (y * (1.0 + sc_ref[...]) + sh_ref[...]).astype(o_ref.dtype)


def _mod_spec(mod, d):
    if mod.shape[0] == 1:
        return pl.BlockSpec((None, 1, d), lambda b, i: (0, 0, 0))
    return pl.BlockSpec((None, 1, d), lambda b, i: (b, 0, 0))


def ln_modulate(x, shift, scale, tt=256):
    bsz, t, d = x.shape
    return pl.pallas_call(
        _lnmod_kernel,
        grid=(bsz, t // tt),
        in_specs=[
            pl.BlockSpec((None, tt, d), lambda b, i: (b, i, 0)),
            _mod_spec(shift, d),
            _mod_spec(scale, d),
        ],
        out_specs=pl.BlockSpec((None, tt, d), lambda b, i: (b, i, 0)),
        out_shape=jax.ShapeDtypeStruct((bsz, t, d), BF16),
        compiler_params=_cparams("parallel", "parallel"),
        name="ln_modulate",
    )(x, shift, scale)


def _lnmod_router_kernel(n_exp, x_ref, sh_ref, sc_ref, rw_ref, h_ref, aff_ref):
    y = _ln_rows(x_ref[...])
    h = y * (1.0 + sc_ref[...]) + sh_ref[...]
    h_ref[...] = h.astype(h_ref.dtype)
    logits = _dot3(h, rw_ref[...])
    lane = _iota(logits.shape, 1)
    logits = jnp.where(lane < n_exp, logits, NEG)
    m = jnp.max(logits, axis=-1, keepdims=True)
    e = jnp.exp(logits - m)
    aff_ref[...] = e / jnp.sum(e, axis=-1, keepdims=True)


def ln_modulate_router(x, shift, scale, router_w, tt=256):
    bsz, t, d = x.shape
    n_exp = router_w.shape[1]
    rw = jnp.pad(router_w, ((0, 0), (0, LANE - n_exp)))
    return pl.pallas_call(
        functools.partial(_lnmod_router_kernel, n_exp),
        grid=(bsz, t // tt),
        in_specs=[
            pl.BlockSpec((None, tt, d), lambda b, i: (b, i, 0)),
            _mod_spec(shift, d),
            _mod_spec(scale, d),
            pl.BlockSpec((d, LANE), lambda b, i: (0, 0)),
        ],
        out_specs=[
            pl.BlockSpec((None, tt, d), lambda b, i: (b, i, 0)),
            pl.BlockSpec((None, tt, LANE), lambda b, i: (b, i, 0)),
        ],
        out_shape=[
            jax.ShapeDtypeStruct((bsz, t, d), BF16),
            jax.ShapeDtypeStruct((bsz, t, LANE), F32),
        ],
        compiler_params=_cparams("parallel", "parallel"),
        name="ln_modulate_router",
    )(x, shift, scale, rw)


def _mm_kernel(a_ref, b_ref, o_ref):
    o_ref[...] = _dot(a_ref[...], b_ref[...]).astype(o_ref.dtype)


def _pick(n, pref):
    for c in pref:
        if n % c == 0:
            return c
    return n


def matmul(a, b, layer, out_dtype=F32, tm=None, tn=None, name="matmul"):
    m, k = a.shape
    n = b.shape[2]
    tm = tm or _pick(m, (1024, 512, 256, 128))
    tn = tn or _pick(n, (768, 512, 256, 128))
    return pl.pallas_call(
        _mm_kernel,
        grid=(m // tm, n // tn),
        in_specs=[
            pl.BlockSpec((tm, k), lambda i, j: (i, 0)),
            pl.BlockSpec((None, k, tn), lambda i, j: (layer, 0, j)),
        ],
        out_specs=pl.BlockSpec((tm, tn), lambda i, j: (i, j)),
        out_shape=jax.ShapeDtypeStruct((m, n), out_dtype),
        compiler_params=_cparams("parallel", "parallel"),
        name=name,
    )(a, b)


HEAD_SHIFT = 6
assert (1 << HEAD_SHIFT) == A_HEAD and 2 * A_HEAD == LANE and SCAN_CHUNK == A_HEAD


def _head_block_ones():
    i = _iota((LANE, LANE), 0) >> HEAD_SHIFT
    j = _iota((LANE, LANE), 1) >> HEAD_SHIFT
    return (i == j).astype(BF16)


def _seg_sum(x):
    bd = _head_block_ones()
    outs = []
    for c in range(x.shape[1] // LANE):
        hi, mid, lo = _split3(x[:, c * LANE:(c + 1) * LANE])
        outs.append(_dot(hi, bd) + (_dot(mid, bd) + _dot(lo, bd)))
    return jnp.concatenate(outs, axis=1)


SCAN_FIELDS = 6


def _rwkv_prep_kernel(aw, gp, gw, tiles_per_seq,
                      x_ref, xp_ref, xn_ref, mu_ref, w0_ref, w2_ref, a0_ref, a2_ref, g2_ref,
                      kkp_ref, kap_ref, rkp_ref,
                      packf_o, packb_o, bonus_o, g_o):
    i = pl.program_id(0)
    x = x_ref[...]
    tt = x.shape[0]
    row = _iota(x.shape, 0)
    pos = i % tiles_per_seq
    prev_row = jnp.where(pos == 0, 0.0, xp_ref[SUBLANE - 1:SUBLANE, :])
    next_row = jnp.where(pos == tiles_per_seq - 1, 0.0, xn_ref[0:1, :])
    prev = jnp.where(row == 0, prev_row, pltpu.roll(x, 1, 0))
    nxt = jnp.where(row == tt - 1, next_row, pltpu.roll(x, tt - 1, 0))
    xs = x + mu_ref[...] * (0.5 * (prev + nxt) - x)

    r = xs[:, 0:aw]
    k = xs[:, aw:2 * aw]
    v = xs[:, 2 * aw:3 * aw]
    o = 3 * aw
    wlo = jnp.tanh(xs[:, o:o + LANE])
    alo = xs[:, o + LANE:o + 2 * LANE]
    glo = _sigmoid(xs[:, o + 2 * LANE:o + 2 * LANE + gp])
    lane = _iota((1, LANE), 1)
    exp_mhalf = float(np.exp(-0.5))

    g_o[...] = _dot3(glo, g2_ref[...])

    kk = k * kkp_ref[...]
    nrm = jnp.sqrt(_seg_sum(kk * kk))
    kk = kk / jnp.maximum(nrm, 1e-12)
    bonus_o[...] = _seg_sum(r * k * rkp_ref[...]) * v

    for d, pack_o in enumerate((packf_o, packb_o)):
        sel = ((lane >> HEAD_SHIFT) == d).astype(F32)
        wl = w0_ref[d:d + 1, :] + _dot3(wlo * sel, w2_ref[...])
        lw = -_sigmoid(wl) * exp_mhalf
        a = _sigmoid(a0_ref[d:d + 1, :] + _dot3(alo * sel, a2_ref[...]))
        fields = (lw, kk, kk * a, k * (1.0 + (a - 1.0) * kap_ref[...]), r, v)
        for g in range(aw // gw):
            for f, arr in enumerate(fields):
                c0 = (g * SCAN_FIELDS + f) * gw
                pack_o[:, c0:c0 + gw] = arr[:, g * gw:(g + 1) * gw]


def scan_group_width(aw, max_pairs=6):
    n_pairs = aw // LANE
    return LANE * max(p for p in range(1, max_pairs + 1) if n_pairs % p == 0)


def rwkv_prep(u, seq_len, rw, aw, ap, tt=128):
    n = u.shape[0]
    gw = scan_group_width(aw)
    gp = ap - 3 * aw - 2 * LANE
    n8 = n // SUBLANE
    tpb = tt // SUBLANE
    full = lambda shape: pl.BlockSpec(shape, lambda i: (0,) * len(shape))
    tok = pl.BlockSpec((tt, aw), lambda i: (i, 0))
    pack = pl.BlockSpec((tt, SCAN_FIELDS * aw), lambda i: (i, 0))
    pack_shape = jax.ShapeDtypeStruct((n, SCAN_FIELDS * aw), F32)
    outs = pl.pallas_call(
        functools.partial(_rwkv_prep_kernel, aw, gp, gw, seq_len // tt),
        grid=(n // tt,),
        in_specs=[
            pl.BlockSpec((tt, ap), lambda i: (i, 0)),
            pl.BlockSpec((SUBLANE, ap), lambda i: (jnp.maximum(i * tpb - 1, 0), 0)),
            pl.BlockSpec((SUBLANE, ap), lambda i: (jnp.minimum((i + 1) * tpb, n8 - 1), 0)),
            full((1, ap)), full((2, aw)), full((LANE, aw)), full((2, aw)), full((LANE, aw)),
            full((gp, aw)), full((1, aw)), full((1, aw)), full((1, aw)),
        ],
        out_specs=[pack, pack, tok, tok],
        out_shape=[pack_shape, pack_shape] + [jax.ShapeDtypeStruct((n, aw), F32)] * 2,
        compiler_params=_cparams("parallel"),
        name="rwkv_prep",
    )(u, u, u, rw["mu"], rw["w0"], rw["w2"], rw["a0"], rw["a2"], rw["g2"],
      rw["k_k"], rw["k_a"], rw["r_k"])
    return outs


def _scan_chunks(rev, toks, sts, consts):
    tri, strict, incl, blk, eye, head0, t2, s2 = consts
    c = SCAN_CHUNK
    ident = (t2 == s2).astype(F32)
    zero = jnp.zeros((), BF16)
    cat = lambda *xs: jnp.concatenate(xs, axis=0)

    def each(f, *lists):
        return [f(*args) for args in zip(*lists)]

    def bd(x):
        xb = x.astype(BF16)
        return jnp.where(blk, cat(xb, xb), zero)

    lw, kk, bb, kd, r, v = (list(t) for t in zip(*toks))
    big_l = each(lambda a: _dot_exact_lhs(tri, a), lw)
    l_tot = each(lambda a: a[0:1, :] if rev else a[c - 1:c, :], big_l)
    lhs = each(lambda k_, r_, l_, w_: cat(-k_ * jnp.exp(l_ - w_), r_ * jnp.exp(l_)).astype(BF16),
               kk, r, big_l, lw)
    gi = each(lambda l_: jnp.exp(-l_), big_l)
    bt = each(lambda b_, g_: (b_ * g_).astype(BF16), bb, gi)
    kt = each(lambda k_, g_: (k_ * g_).astype(BF16), kd, gi)
    rhs = each(lambda b_, k_: cat(jnp.where(head0, b_, zero), jnp.where(head0, zero, b_),
                                  jnp.where(head0, k_, zero), jnp.where(head0, zero, k_)), bt, kt)
    p = each(_dot_nt, lhs, rhs)
    nab = each(lambda p_: jnp.where(strict, p_[0:c, 0:LANE], 0.0).astype(BF16), p)
    nrb = each(lambda p_: jnp.where(incl, p_[c:2 * c, 0:LANE], 0.0).astype(BF16), p)
    nkk = each(lambda p_: cat(jnp.where(strict, p_[0:c, LANE:2 * LANE], 0.0),
                              jnp.where(incl, p_[c:2 * c, LANE:2 * LANE], 0.0)).astype(BF16), p)
    z = each(lambda l_, s_: _dot(l_, s_.astype(BF16)), lhs, sts)
    w = each(lambda n_, v_: _dot(n_, bd(v_)), nkk, v)
    x = each(lambda z_, w_: z_[0:c] + w_[0:c], z, w)
    blk8 = (t2 >> 3) == (s2 >> 3)
    n0 = each(lambda n_: jnp.where(blk8, n_, zero), nab)
    inv = each(lambda n_: ident + n_.astype(F32), n0)
    pw = each(lambda n_: _dot(n_, bd(n_)), n0)
    inv = each(lambda i_, p_: i_ + _dot(i_.astype(BF16), bd(p_)), inv, pw)
    pw = each(lambda p_: _dot(p_.astype(BF16), bd(p_)), pw)
    inv = each(lambda i_, p_: i_ + _dot(i_.astype(BF16), bd(p_)), inv, pw)
    for lb in range(3, HEAD_SHIFT):
        new = ((t2 >> (lb + 1)) == (s2 >> (lb + 1))) & ((t2 >> lb) != (s2 >> lb))
        half = each(lambda i_, n_: _dot(i_.astype(BF16), bd(jnp.where(new, n_, zero))), inv, nab)
        inv = each(lambda i_, h_: i_ + _dot(h_.astype(BF16), bd(i_)), inv, half)
    x = each(lambda i_, x_: _dot(i_.astype(BF16), bd(x_)), inv, x)
    y = each(lambda z_, w_, n_, x_: z_[c:2 * c] + w_[c:2 * c] + _dot(n_, bd(x_)), z, w, nrb, x)
    gr = each(lambda t_, l_: jnp.exp(t_ - l_), l_tot, big_l)
    upd = each(lambda b_, k_, g_, x_, v_: _dot3(cat(b_ * g_, k_ * g_).T, cat(x_, v_)), bb, kd, gr, x, v)
    g_col = each(lambda t_: jnp.sum(jnp.where(eye, jnp.exp(t_), 0.0), axis=1, keepdims=True), l_tot)
    st_new = each(lambda s_, g_, u_: s_ * g_ + jnp.where(blk, u_, 0.0), sts, g_col, upd)
    return y, st_new


def _scan_kernel(rev, npar, n_chunks, tok_ref, s0_ref, y_ref, st_out_ref, st_scr):
    ci = pl.program_id(2)

    @pl.when(ci == 0)
    def _():
        st_scr[...] = s0_ref[...]

    c = SCAN_CHUNK
    ti = _iota((c, c), 0)
    si = _iota((c, c), 1)
    tri = ((si >= ti) if rev else (si <= ti)).astype(BF16)
    t2 = _iota((c, LANE), 0)
    s2 = _iota((c, LANE), 1) & (c - 1)
    strict = (s2 > t2) if rev else (s2 < t2)
    incl = (s2 >= t2) if rev else (s2 <= t2)
    bi = _iota((LANE, LANE), 0)
    bj = _iota((LANE, LANE), 1)
    blk = (bi >> HEAD_SHIFT) == (bj >> HEAD_SHIFT)
    eye = bi == bj
    head0 = _iota((c, LANE), 1) < A_HEAD
    consts = (tri, strict, incl, blk, eye, head0, t2, s2)
    lanes = [slice(p * LANE, (p + 1) * LANE) for p in range(npar)]
    gw = npar * LANE
    toks = [tuple(tok_ref[:, f * gw + p * LANE:f * gw + (p + 1) * LANE] for f in range(SCAN_FIELDS))
            for p in range(npar)]
    ys, sts = _scan_chunks(rev, toks, [st_scr[p] for p in range(npar)], consts)
    for p, sl in enumerate(lanes):
        y_ref[:, sl] = ys[p]
        st_scr[p] = sts[p]

    @pl.when(ci == n_chunks - 1)
    def _():
        st_out_ref[...] = st_scr[...]


def rwkv_scan(packed, s0, rev):
    bsz, t, width = packed.shape
    aw = width // SCAN_FIELDS
    n_pairs = aw // LANE
    npar = scan_group_width(aw) // LANE
    n_chunks = t // SCAN_CHUNK
    if rev:
        tmap = lambda b, g, c: (b, n_chunks - 1 - c, g)
    else:
        tmap = lambda b, g, c: (b, c, g)
    tok = pl.BlockSpec((None, SCAN_CHUNK, npar * LANE), tmap)
    tok_in = pl.BlockSpec((None, SCAN_CHUNK, SCAN_FIELDS * npar * LANE), tmap)
    st = pl.BlockSpec((None, npar, LANE, LANE), lambda b, g, c: (b, g, 0, 0))
    return pl.pallas_call(
        functools.partial(_scan_kernel, rev, npar, n_chunks),
        grid=(bsz, n_pairs // npar, n_chunks),
        in_specs=[tok_in, st],
        out_specs=[tok, st],
        out_shape=[jax.ShapeDtypeStruct((bsz, t, aw), F32),
                   jax.ShapeDtypeStruct(s0.shape, F32)],
        scratch_shapes=[pltpu.VMEM((npar, LANE, LANE), F32)],
        compiler_params=_cparams("parallel", "parallel", "arbitrary"),
        name="rwkv_scan_bwd" if rev else "rwkv_scan_fwd",
    )(packed, s0)


def _rwkv_post_kernel(yf_ref, yb_ref, bonus_ref, g_ref, lnw_ref, lnb_ref, o_ref):
    y = yf_ref[...] + yb_ref[...]
    inv = 1.0 / A_HEAD
    mu = _seg_sum(y) * inv
    yc = y - mu
    var = _seg_sum(yc * yc) * inv
    yn = yc * lax.rsqrt(var + GN_EPS) * lnw_ref[...] + lnb_ref[...]
    o_ref[...] = ((yn + bonus_ref[...]) * g_ref[...]).astype(o_ref.dtype)


def rwkv_post(yf, yb, bonus, g, ln_w, ln_b, tt=256):
    n, aw = yf.shape
    tok = pl.BlockSpec((tt, aw), lambda i: (i, 0))
    par = pl.BlockSpec((1, aw), lambda i: (0, 0))
    return pl.pallas_call(
        _rwkv_post_kernel,
        grid=(n // tt,),
        in_specs=[tok, tok, tok, tok, par, par],
        out_specs=tok,
        out_shape=jax.ShapeDtypeStruct((n, aw), BF16),
        compiler_params=_cparams("parallel"),
        name="rwkv_post",
    )(yf, yb, bonus, g, ln_w, ln_b)


def round_up(x, m):
    return (x + m - 1) // m * m


ATT_SCALE = HEAD_DIM ** -0.5


def _ctx_attn_kernel(q_ref, k_ref, v_ref, sink_ref, o_ref):
    s = _dot3_nt(q_ref[...] * ATT_SCALE, k_ref[...])
    sk = sink_ref[0:1, 0:1]
    m = jnp.maximum(jnp.max(s, axis=-1, keepdims=True), sk)
    e = jnp.exp(s - m)
    den = jnp.sum(e, axis=-1, keepdims=True) + jnp.exp(sk - m)
    o = _dot(e.astype(BF16), v_ref[...].astype(BF16))
    o_ref[...] = (o / den).astype(o_ref.dtype)


def ctx_attention(u, bsz, t, q0, k0, v0, n_heads, group, sink):
    sink = jnp.full((n_heads,), NEG, F32) if sink is None else sink.astype(F32)
    sink = jnp.broadcast_to(sink[:, None, None], (n_heads, 1, LANE))
    return pl.pallas_call(
        _ctx_attn_kernel,
        grid=(bsz, n_heads),
        in_specs=[
            pl.BlockSpec((t, HEAD_DIM), lambda b, h: (b, q0 + h)),
            pl.BlockSpec((t, HEAD_DIM), lambda b, h: (b, k0 + h // group)),
            pl.BlockSpec((t, HEAD_DIM), lambda b, h: (b, v0 + h // group)),
            pl.BlockSpec((None, 1, LANE), lambda b, h: (h, 0, 0)),
        ],
        out_specs=pl.BlockSpec((t, HEAD_DIM), lambda b, h: (b, h)),
        out_shape=jax.ShapeDtypeStruct((bsz * t, n_heads * HEAD_DIM), BF16),
        compiler_params=_cparams("parallel", "parallel"),
        name="ctx_attention",
    )(u, u, u, sink)


def rope_tables(n_tok):
    half = HEAD_DIM // 2
    quarter = half // 2
    tok = jnp.arange(n_tok)
    row = (tok // GRID_W).astype(F32)
    col = (tok % GRID_W).astype(F32)
    inv = ROPE_THETA ** (-jnp.arange(quarter, dtype=F32) / quarter)
    ang_r = row[:, None] * inv[None]
    ang_c = col[:, None] * inv[None]
    cos = jnp.concatenate([jnp.cos(ang_r)] * 2 + [jnp.cos(ang_c)] * 2, axis=1)
    sr, sc = jnp.sin(ang_r), jnp.sin(ang_c)
    z = jnp.zeros_like(sr)
    sin_a = jnp.concatenate([-sr, z, -sc, z], axis=1)
    sin_b = jnp.concatenate([z, sr, z, sc], axis=1)
    return cos, sin_a, sin_b


def _rope_kernel(n_heads, x_ref, cos_ref, sa_ref, sb_ref, o_ref):
    q = HEAD_DIM // 4
    cos, sa, sb = cos_ref[...], sa_ref[...], sb_ref[...]
    for h in range(n_heads):
        sl = slice(h * HEAD_DIM, (h + 1) * HEAD_DIM)
        x = x_ref[:, sl]
        o_ref[:, sl] = x * cos + pltpu.roll(x, HEAD_DIM - q, 1) * sa + pltpu.roll(x, q, 1) * sb


def rope(u, bsz, n_tok, col0, n_heads, tt=256):
    cos, sin_a, sin_b = rope_tables(n_tok)
    nt = n_tok // tt
    hb = max(k for k in range(1, n_heads + 1) if n_heads % k == 0 and col0 % k == 0)
    width = hb * HEAD_DIM
    cb = col0 // hb
    tab = pl.BlockSpec((tt, HEAD_DIM), lambda b, i, j: (i, 0))
    return pl.pallas_call(
        functools.partial(_rope_kernel, hb),
        grid=(bsz, nt, n_heads // hb),
        in_specs=[pl.BlockSpec((tt, width), lambda b, i, j: (b * nt + i, cb + j)), tab, tab, tab],
        out_specs=pl.BlockSpec((tt, width), lambda b, i, j: (b * nt + i, j)),
        out_shape=jax.ShapeDtypeStruct((bsz * n_tok, n_heads * HEAD_DIM), F32),
        compiler_params=_cparams("parallel", "parallel", "parallel"),
        name="rope",
    )(u, cos, sin_a, sin_b)


def _win_attn_kernel(group, n_tok, sink_ref, q_ref, kp_ref, kc_ref, kn_ref, vp_ref, vc_ref, vn_ref,
                     ck_ref, cv_ref, o_ref):
    kv = pl.program_id(1)
    i = pl.program_id(2)
    rows = group * QBLK
    q = jnp.concatenate([q_ref[:, g * HEAD_DIM:(g + 1) * HEAD_DIM] for g in range(group)], axis=0)
    q = q * ATT_SCALE
    k_loc = jnp.concatenate([kp_ref[...], kc_ref[...], kn_ref[...]], axis=0)
    v_loc = jnp.concatenate([vp_ref[...], vc_ref[...], vn_ref[...]], axis=0).astype(BF16)
    s_loc = _dot3_nt(q, k_loc)
    qi = _iota(s_loc.shape, 0) & (QBLK - 1)
    kj = _iota(s_loc.shape, 1)
    pos = i * QBLK - WIN + kj
    ok = (kj - qi >= 0) & (kj - qi <= 2 * WIN) & (pos >= 0) & (pos < n_tok)
    s_loc = jnp.where(ok, s_loc, NEG)
    s_ctx = _dot3_nt(q, ck_ref[...])
    rg = _iota((rows, 1), 0) >> int(np.log2(QBLK))
    sk = jnp.zeros((rows, 1), F32)
    for g in range(group):
        sk = jnp.where(rg == g, sink_ref[kv * group + g], sk)
    m = jnp.maximum(jnp.maximum(jnp.max(s_loc, axis=-1, keepdims=True),
                                jnp.max(s_ctx, axis=-1, keepdims=True)), sk)
    e_loc = jnp.exp(s_loc - m)
    e_ctx = jnp.exp(s_ctx - m)
    den = (jnp.sum(e_loc, axis=-1, keepdims=True) + jnp.sum(e_ctx, axis=-1, keepdims=True)
           + jnp.exp(sk - m))
    o = (_dot(e_loc.astype(BF16), v_loc) + _dot(e_ctx.astype(BF16), cv_ref[...].astype(BF16))) / den
    for g in range(group):
        o_ref[:, g * HEAD_DIM:(g + 1) * HEAD_DIM] = o[g * QBLK:(g + 1) * QBLK].astype(o_ref.dtype)


def window_attention(qk_rot, u, vb0, bsz, n_tok, n_kv, group, ck, cv, sink):
    assert WIN == QBLK
    nb = n_tok // QBLK
    nq = n_kv * group
    past = ck.shape[2]
    prv = lambda i: jnp.maximum(i - 1, 0)
    nxt = lambda i: jnp.minimum(i + 1, nb - 1)
    blk = lambda f, c0: pl.BlockSpec((QBLK, HEAD_DIM), lambda b, kv, i, s: (b * nb + f(i), c0 + kv))
    same = lambda i: i
    cache = pl.BlockSpec((None, None, past, HEAD_DIM), lambda b, kv, i, s: (b, kv, 0, 0))
    grid_spec = pltpu.PrefetchScalarGridSpec(
        num_scalar_prefetch=1,
        grid=(bsz, n_kv, nb),
        in_specs=[
            pl.BlockSpec((QBLK, group * HEAD_DIM), lambda b, kv, i, s: (b * nb + i, kv)),
            blk(prv, nq), blk(same, nq), blk(nxt, nq),
            blk(prv, vb0), blk(same, vb0), blk(nxt, vb0),
            cache, cache,
        ],
        out_specs=pl.BlockSpec((QBLK, group * HEAD_DIM), lambda b, kv, i, s: (b * nb + i, kv)),
    )
    return pl.pallas_call(
        functools.partial(_win_attn_kernel, group, n_tok),
        grid_spec=grid_spec,
        out_shape=jax.ShapeDtypeStruct((bsz * n_tok, nq * HEAD_DIM), BF16),
        compiler_params=_cparams("parallel", "parallel", "parallel"),
        name="window_attention",
    )(sink.astype(F32), qk_rot, qk_rot, qk_rot, qk_rot, u, u, u, ck, cv)


NB_QROWS = 4
NB_QTOK = NB_QROWS * GRID_W
NB_KTOK = 3 * NB_QTOK


def nb_tables(rpb, rows):
    kh = min(NB_ROWS, rows)
    ql = np.arange(NB_QTOK)
    kl = np.arange(NB_KTOK)
    r_rel, c = ql // GRID_W, ql % GRID_W
    kr_rel, kc = kl // GRID_W - NB_QROWS, kl % GRID_W
    n_heads = rpb.shape[0]
    gcol = np.arange(GRID_W)
    col_idx = np.clip(gcol[None, :] - gcol[:, None] + NB_COLS - 1, 0, 2 * NB_COLS - 2)
    t_col = jnp.take(rpb.astype(F32), jnp.asarray(col_idx.reshape(-1)), axis=2)
    t_col = t_col.reshape(n_heads, 2 * NB_ROWS - 1, GRID_W * GRID_W)
    qr = np.arange(NB_QROWS)
    krr = np.arange(3 * NB_QROWS) - NB_QROWS
    row_idx = np.clip(krr[None, :] - qr[:, None] + NB_ROWS - 1, 0, 2 * NB_ROWS - 2)
    bias = jnp.take(t_col, jnp.asarray(row_idx.reshape(-1)), axis=1)
    bias = bias.reshape(n_heads, NB_QROWS, 3 * NB_QROWS, GRID_W, GRID_W)
    bias = jnp.transpose(bias, (0, 1, 3, 2, 4)).reshape(n_heads, NB_QTOK, NB_KTOK)
    win_start = np.clip(c - NB_COLS // 2, 0, GRID_W - NB_COLS)
    col_ok = (kc[None, :] >= win_start[:, None]) & (kc[None, :] < win_start[:, None] + NB_COLS)
    masks = []
    for j in range(rows // NB_QROWS):
        r = j * NB_QROWS + r_rel
        kr = j * NB_QROWS + kr_rel
        row_start = np.clip(r - kh // 2, 0, rows - kh)
        row_ok = (kr[None, :] >= row_start[:, None]) & (kr[None, :] < row_start[:, None] + kh)
        masks.append(row_ok & col_ok)
    return bias, jnp.asarray(np.stack(masks).astype(np.float32))


def _nb_attn_kernel(q_ref, kp_ref, kc_ref, kn_ref, vp_ref, vc_ref, vn_ref, ck_ref, cv_ref,
                    bias_ref, mask_ref, o_ref):
    q = q_ref[...] * ATT_SCALE
    k_loc = jnp.concatenate([kp_ref[...], kc_ref[...], kn_ref[...]], axis=0)
    v_loc = jnp.concatenate([vp_ref[...], vc_ref[...], vn_ref[...]], axis=0).astype(BF16)
    s_loc = jnp.where(mask_ref[...] > 0.0, _dot3_nt(q, k_loc) + bias_ref[...], NEG)
    s_ctx = _dot3_nt(q, ck_ref[...])
    m = jnp.maximum(jnp.max(s_loc, axis=-1, keepdims=True), jnp.max(s_ctx, axis=-1, keepdims=True))
    e_loc = jnp.exp(s_loc - m)
    e_ctx = jnp.exp(s_ctx - m)
    den = jnp.sum(e_loc, axis=-1, keepdims=True) + jnp.sum(e_ctx, axis=-1, keepdims=True)
    o = _dot(e_loc.astype(BF16), v_loc) + _dot(e_ctx.astype(BF16), cv_ref[...].astype(BF16))
    o_ref[...] = (o / den).astype(o_ref.dtype)


def neighbourhood_attention(u, cols, bsz, n_tok, n_heads, ck, cv, rpb):
    q0, k0, v0 = cols
    rows = n_tok // GRID_W
    assert rows % NB_QROWS == 0 and rows >= NB_ROWS
    nj = rows // NB_QROWS
    past = ck.shape[2]
    bias, mask = nb_tables(rpb, rows)
    prv = lambda j: jnp.maximum(j - 1, 0)
    nxt = lambda j: jnp.minimum(j + 1, nj - 1)
    same = lambda j: j
    blk = lambda f, c0: pl.BlockSpec((NB_QTOK, HEAD_DIM), lambda b, h, j: (b * nj + f(j), c0 + h))
    cache = pl.BlockSpec((None, None, past, HEAD_DIM), lambda b, h, j: (b, h, 0, 0))
    return pl.pallas_call(
        _nb_attn_kernel,
        grid=(bsz, n_heads, nj),
        in_specs=[
            blk(same, q0),
            blk(prv, k0), blk(same, k0), blk(nxt, k0),
            blk(prv, v0), blk(same, v0), blk(nxt, v0),
            cache, cache,
            pl.BlockSpec((None, NB_QTOK, NB_KTOK), lambda b, h, j: (h, 0, 0)),
            pl.BlockSpec((None, NB_QTOK, NB_KTOK), lambda b, h, j: (j, 0, 0)),
        ],
        out_specs=pl.BlockSpec((NB_QTOK, HEAD_DIM), lambda b, h, j: (b * nj + j, h)),
        out_shape=jax.ShapeDtypeStruct((bsz * n_tok, n_heads * HEAD_DIM), BF16),
        compiler_params=_cparams("parallel", "parallel", "parallel"),
        name="neighbourhood_attention",
    )(u, u, u, u, u, u, u, ck, cv, bias, mask)


def _merge_kernel(ya_ref, yb_ref, yc_ref, wa_ref, wb_ref, wc_ref, ga_ref, gb_ref, gc_ref, o_ref):
    m = _sigmoid(ga_ref[...]) * _dot(ya_ref[...], wa_ref[...])
    m = m + _sigmoid(gb_ref[...]) * _dot(yb_ref[...], wb_ref[...])
    m = m + _sigmoid(gc_ref[...]) * _dot(yc_ref[...], wc_ref[...])
    o_ref[...] = m.astype(o_ref.dtype)


def merge_branches(ya, yb, yc, wa, wb, wc, u, gate_col0, tm=512, tn=512):
    n, aw = ya.shape
    bw, cw = wb.shape[0], wc.shape[0]
    d = wa.shape[1]
    tm = _pick(n, (tm, 256, 128))
    tn = _pick(d, (tn, 256, 128))
    assert gate_col0 % tn == 0
    g0 = gate_col0 // tn
    nd = d // tn
    gate = lambda gi: pl.BlockSpec((tm, tn), lambda i, j: (i, g0 + gi * nd + j))
    return pl.pallas_call(
        _merge_kernel,
        grid=(n // tm, nd),
        in_specs=[
            pl.BlockSpec((tm, aw), lambda i, j: (i, 0)),
            pl.BlockSpec((tm, bw), lambda i, j: (i, 0)),
            pl.BlockSpec((tm, cw), lambda i, j: (i, 0)),
            pl.BlockSpec((aw, tn), lambda i, j: (0, j)),
            pl.BlockSpec((bw, tn), lambda i, j: (0, j)),
            pl.BlockSpec((cw, tn), lambda i, j: (0, j)),
            gate(0), gate(1), gate(2),
        ],
        out_specs=pl.BlockSpec((tm, tn), lambda i, j: (i, j)),
        out_shape=jax.ShapeDtypeStruct((n, d), BF16),
        compiler_params=_cparams("parallel", "parallel"),
        name="merge_branches",
    )(ya, yb, yc, wa, wb, wc, u, u, u)


def _post_norm(alpha, x, gate, y, g, b):
    r = alpha * x + gate * y
    return _ln_rows(r) * g + b


def _outproj_ln_kernel(alpha, nk, m_ref, w_ref, x_ref, gate_ref, g_ref, b_ref, o_ref):
    k = pl.program_id(1)

    @pl.when(k == 0)
    def _():
        o_ref[...] = _dot(m_ref[...], w_ref[...])

    @pl.when(k > 0)
    def _():
        o_ref[...] += _dot(m_ref[...], w_ref[...])

    @pl.when(k == nk - 1)
    def _():
        rows = o_ref.shape[0]
        step = min(rows, 128)
        for r0 in range(0, rows, step):
            sl = slice(r0, r0 + step)
            o_ref[sl, :] = _post_norm(alpha, x_ref[sl, :], gate_ref[...], o_ref[sl, :],
                                      g_ref[...], b_ref[...])


def _tile_mod_spec(mod, d, tm, seq_len):
    if mod.shape[0] == 1:
        return pl.BlockSpec((None, 1, d), lambda i, *_: (0, 0, 0))
    return pl.BlockSpec((None, 1, d), lambda i, *_: (i * tm // seq_len, 0, 0))


def outproj_postnorm(m, w_out, x, gate, ln_g, ln_b, alpha, seq_len, tm=512, tk=512):
    n, d = x.shape
    kdim = m.shape[1]
    tm = _pick(n if gate.shape[0] == 1 else seq_len, (tm, 256, 128))
    tk = _pick(kdim, (tk, 256, 128))
    nk = kdim // tk
    vec = pl.BlockSpec((1, d), lambda i, k: (0, 0))
    return pl.pallas_call(
        functools.partial(_outproj_ln_kernel, alpha, nk),
        grid=(n // tm, nk),
        in_specs=[
            pl.BlockSpec((tm, tk), lambda i, k: (i, k)),
            pl.BlockSpec((tk, d), lambda i, k: (k, 0)),
            pl.BlockSpec((tm, d), lambda i, k: (i, 0), pipeline_mode=pl.Buffered(1)),
            _tile_mod_spec(gate, d, tm, seq_len),
            vec, vec,
        ],
        out_specs=pl.BlockSpec((tm, d), lambda i, k: (i, 0)),
        out_shape=jax.ShapeDtypeStruct((n, d), F32),
        compiler_params=_cparams("parallel", "arbitrary"),
        name="outproj_postnorm",
    )(m, w_out, x, gate, ln_g, ln_b)


def _select_kernel(cap, n, aff_ref, mask_ref):
    bits = lax.bitcast_convert_type(aff_ref[...], jnp.int32)
    n_exp = bits.shape[0]
    capf = jnp.float32(cap)

    def count(pred):
        return jnp.sum(pred.astype(F32), axis=1, keepdims=True)

    def value_step(_, carry):
        lo, hi = carry
        mid = lo + ((hi - lo + 1) >> 1)
        ok = count(bits >= mid) >= capf
        return jnp.where(ok, mid, lo), jnp.where(ok, hi, mid - 1)

    lo0 = jnp.zeros((n_exp, 1), jnp.int32)
    hi0 = jnp.full((n_exp, 1), 0x7F800000, jnp.int32)
    thr, _ = lax.fori_loop(0, 32, value_step, (lo0, hi0))
    gt = bits > thr
    eq = bits == thr
    need = capf - count(gt)
    idx = _iota(bits.shape, 1)

    def index_step(_, carry):
        lo, hi = carry
        mid = (lo + hi) >> 1
        ok = count(eq & (idx < mid)) >= need
        return jnp.where(ok, lo, mid), jnp.where(ok, mid, hi)

    _, bound = lax.fori_loop(0, int(np.ceil(np.log2(n))) + 1, index_step,
                             (jnp.zeros((n_exp, 1), jnp.int32), jnp.full((n_exp, 1), n, jnp.int32)))
    mask_ref[...] = (gt | (eq & (idx < bound))).astype(jnp.int32)


def expert_choice_mask(aff_t, cap):
    n_exp, n = aff_t.shape
    return pl.pallas_call(
        functools.partial(_select_kernel, cap, n),
        out_shape=jax.ShapeDtypeStruct((n_exp, n), jnp.int32),
        compiler_params=pltpu.CompilerParams(vmem_limit_bytes=VMEM_LIMIT_BYTES),
        name="expert_choice_mask",
    )(aff_t)


def _count_le(sorted_vals, x):
    return jnp.sum((sorted_vals[None, :] <= x[:, None]).astype(jnp.int32), axis=1)


def _visit_list(nvis, first_blk, n_work):
    na, nb = nvis.shape
    flat = nvis.reshape(-1)
    off_end = jnp.cumsum(flat)
    off_start = off_end - flat
    total = off_end[-1]
    w = jnp.minimum(jnp.arange(n_work, dtype=jnp.int32), total - 1)
    idx = jnp.minimum(_count_le(off_end, w), na * nb - 1)
    blk = first_blk.reshape(-1)[idx] + (w - off_start[idx])
    valid = (jnp.arange(n_work) < total).astype(jnp.int32)
    return idx // nb, idx % nb, blk.astype(jnp.int32), valid


MOE_GROUP = 4
NO_SLOT = 1 << 28


def _group_items(key, valid, n_keys, group, n_steps):
    n_items = key.shape[0]
    cnt = jnp.zeros((n_keys,), jnp.int32).at[key].add(valid)
    start = jnp.cumsum(cnt) - cnt
    per_key = (cnt + group - 1) // group
    s_end = jnp.cumsum(per_key)
    s_start = s_end - per_key
    total = s_end[-1]
    step = jnp.arange(n_steps, dtype=jnp.int32)
    real = step < total
    sc = jnp.minimum(step, total - 1)
    k = jnp.minimum(_count_le(s_end, sc), n_keys - 1)
    j = sc - s_start[k]
    within = j[None, :] * group + jnp.arange(group, dtype=jnp.int32)[:, None]
    ok = (within < cnt[k][None, :]) & real[None, :]
    item = jnp.clip(start[k][None, :] + jnp.minimum(within, cnt[k][None, :] - 1), 0, n_items - 1)
    first = ((j == 0) & real).astype(jnp.int32)
    last = ((j == per_key[k] - 1) & real).astype(jnp.int32)
    return item, ok.astype(jnp.int32), first, last


def routing_plan(mask, cap):
    n_exp, n = mask.shape
    nt = n // TOK_TILE
    nsb = cap // SLOT_BLK
    pos = jnp.cumsum(mask, axis=1) - mask
    posm = jnp.where(mask > 0, pos, -1).astype(jnp.int32)
    cnt = mask.reshape(n_exp, nt, TOK_TILE).sum(-1)
    cend = jnp.cumsum(cnt, axis=1)
    cstart = cend - cnt
    sb_lo = jnp.minimum(cstart // SLOT_BLK, nsb - 1)
    nvis = jnp.where(cnt > 0, (cend - 1) // SLOT_BLK - sb_lo + 1, 0)
    n_work = n_exp * (nsb + nt)
    n_blk = n_exp * nsb
    grp = MOE_GROUP
    e, t, sb, valid = _visit_list(nvis, sb_lo, n_work)
    blk = e * nsb + sb
    item, ok, first, last = _group_items(blk, valid, n_blk, grp, n_work // grp + n_blk + 1)
    blk_g = blk[item[0]]
    dispatch = (blk_g // nsb, blk_g, first, last, t[item], jnp.where(ok > 0, sb[item] * SLOT_BLK, NO_SLOT))
    nvis_t = nvis.T.at[:, 0].max(1)
    t2, e2, sb2, valid2 = _visit_list(nvis_t, sb_lo.T, n_work + nt)
    item, ok, first, last = _group_items(t2, valid2, nt, grp, (n_work + nt) // grp + nt + 1)
    combine = (t2[item[0]], first, last, e2[item], e2[item] * nsb + sb2[item],
               jnp.where(ok > 0, sb2[item] * SLOT_BLK, NO_SLOT))
    return posm, dispatch, combine


def _dispatch_kernel(group, e_ref, blk_ref, first_ref, last_ref, tile_ref, base_ref, *refs):
    posm_refs, tok_refs, aff_refs = refs[:group], refs[group:2 * group], refs[2 * group:3 * group]
    o_ref, gate_ref, acc_ref, accg_ref = refs[3 * group:]
    w = pl.program_id(0)
    rows = _iota((SLOT_BLK, TOK_TILE), 0)
    tot, totg = None, None
    for q in range(group):
        onehot = ((rows + base_ref[q, w]) == posm_refs[q][...]).astype(BF16)
        part = _dot(onehot, tok_refs[q][...])
        partg = _dot_exact_lhs(onehot, aff_refs[q][...])
        tot = part if tot is None else tot + part
        totg = partg if totg is None else totg + partg

    @pl.when(first_ref[w] == 1)
    def _():
        acc_ref[...] = tot
        accg_ref[...] = totg

    @pl.when(first_ref[w] == 0)
    def _():
        acc_ref[...] += tot
        accg_ref[...] += totg

    @pl.when(last_ref[w] == 1)
    def _():
        o_ref[...] = acc_ref[...].astype(o_ref.dtype)
        gate_ref[...] = accg_ref[...]


def moe_dispatch(tokens, aff, posm, plan, cap):
    n, d = tokens.shape
    n_exp = posm.shape[0]
    grp = MOE_GROUP
    n_steps = plan[0].shape[0]
    pos_spec = lambda q: pl.BlockSpec((None, 1, TOK_TILE), lambda w, e, b, f, l, t, s: (e[w], 0, t[q, w]))
    tok_spec = lambda q, width: pl.BlockSpec((TOK_TILE, width), lambda w, e, b, f, l, t, s: (t[q, w], 0))
    out_map = lambda w, e, b, f, l, t, s: (b[w], 0)
    grid_spec = pltpu.PrefetchScalarGridSpec(
        num_scalar_prefetch=6,
        grid=(n_steps,),
        in_specs=([pos_spec(q) for q in range(grp)] + [tok_spec(q, d) for q in range(grp)]
                  + [tok_spec(q, LANE) for q in range(grp)]),
        out_specs=[pl.BlockSpec((SLOT_BLK, d), out_map), pl.BlockSpec((SLOT_BLK, LANE), out_map)],
        scratch_shapes=[pltpu.VMEM((SLOT_BLK, d), F32), pltpu.VMEM((SLOT_BLK, LANE), F32)],
    )
    posm3 = posm.reshape(n_exp, 1, n)
    return pl.pallas_call(
        functools.partial(_dispatch_kernel, grp),
        grid_spec=grid_spec,
        out_shape=[jax.ShapeDtypeStruct((n_exp * cap, d), BF16),
                   jax.ShapeDtypeStruct((n_exp * cap, LANE), F32)],
        compiler_params=_cparams("arbitrary"),
        name="moe_dispatch",
    )(*plan, *([posm3] * grp), *([tokens] * grp), *([aff] * grp))


def _ffn_up_kernel(x_ref, w1_ref, w3_ref, o_ref):
    x = x_ref[...]
    a = _dot(x, w1_ref[...])
    o_ref[...] = (a * _sigmoid(a) * _dot(x, w3_ref[...])).astype(o_ref.dtype)


def _ffn_down_kernel(h_ref, w2_ref, aff_ref, o_ref):
    aff = aff_ref[...]
    gval = jnp.sum(jnp.where(_iota(aff.shape, 1) == pl.program_id(0), aff, 0.0), axis=1, keepdims=True)
    o_ref[...] = (_dot(h_ref[...], w2_ref[...]) * gval).astype(o_ref.dtype)


def expert_ffn(xe, aff_rows, w1, w3, w2, layer, cap, tn=512):
    _, n_exp, d, ff = w1.shape
    tm = _pick(cap, (1024, 512, 256, 128))
    nm = cap // tm
    tf = _pick(ff, (tn, 256, 128))
    hid = pl.pallas_call(
        _ffn_up_kernel,
        grid=(n_exp, nm, ff // tf),
        in_specs=[
            pl.BlockSpec((tm, d), lambda e, i, j: (e * nm + i, 0)),
            pl.BlockSpec((None, None, d, tf), lambda e, i, j: (layer, e, 0, j)),
            pl.BlockSpec((None, None, d, tf), lambda e, i, j: (layer, e, 0, j)),
        ],
        out_specs=pl.BlockSpec((tm, tf), lambda e, i, j: (e * nm + i, j)),
        out_shape=jax.ShapeDtypeStruct((n_exp * cap, ff), BF16),
        compiler_params=_cparams("parallel", "parallel", "parallel"),
        name="expert_ffn_up",
    )(xe, w1, w3)
    td = _pick(d, (tn, 256, 128))
    return pl.pallas_call(
        _ffn_down_kernel,
        grid=(n_exp, nm, d // td),
        in_specs=[
            pl.BlockSpec((tm, ff), lambda e, i, j: (e * nm + i, 0)),
            pl.BlockSpec((None, None, ff, td), lambda e, i, j: (layer, e, 0, j)),
            pl.BlockSpec((tm, LANE), lambda e, i, j: (e * nm + i, 0)),
        ],
        out_specs=pl.BlockSpec((tm, td), lambda e, i, j: (e * nm + i, j)),
        out_shape=jax.ShapeDtypeStruct((n_exp * cap, d), BF16),
        compiler_params=_cparams("parallel", "parallel", "parallel"),
        name="expert_ffn_down",
    )(hid, w2, aff_rows)


def _combine_kernel(alpha, group, tile_ref, first_ref, last_ref, e_ref, blk_ref, base_ref,
                    posm_ref, *refs):
    ye_refs = refs[:group]
    x_ref, gate_ref, g_ref, b_ref, o_ref, acc_ref = refs[group:]
    w = pl.program_id(0)
    posm = posm_ref[...].astype(F32)
    lane = _iota(posm.shape, 1)
    cols = _iota((TOK_TILE, SLOT_BLK), 1)
    tot = None
    for q in range(group):
        col = jnp.sum(jnp.where(lane == e_ref[q, w], posm, 0.0), axis=1, keepdims=True)
        onehot = ((cols + base_ref[q, w]).astype(F32) == col).astype(BF16)
        part = _dot(onehot, ye_refs[q][...])
        tot = part if tot is None else tot + part

    @pl.when(first_ref[w] == 1)
    def _():
        acc_ref[...] = tot

    @pl.when(first_ref[w] == 0)
    def _():
        acc_ref[...] += tot

    @pl.when(last_ref[w] == 1)
    def _():
        o_ref[...] = _post_norm(alpha, x_ref[...], gate_ref[...], acc_ref[...], g_ref[...], b_ref[...])


def moe_combine_postnorm(ye, posm_t, plan, x, gate, ln_g, ln_b, alpha, seq_len):
    n, d = x.shape
    n_exp = posm_t.shape[1]
    grp = MOE_GROUP
    n_steps = plan[0].shape[0]
    assert seq_len % TOK_TILE == 0
    tile = lambda w, t, *_: (t[w], 0)
    if gate.shape[0] == 1:
        gate_map = lambda w, t, *_: (0, 0, 0)
    else:
        gate_map = lambda w, t, *_: (t[w] * TOK_TILE // seq_len, 0, 0)
    vec = pl.BlockSpec((1, d), lambda w, *_: (0, 0))
    ye_spec = lambda q: pl.BlockSpec((SLOT_BLK, d), lambda w, t, f, l, e, b, s: (b[q, w], 0))
    grid_spec = pltpu.PrefetchScalarGridSpec(
        num_scalar_prefetch=6,
        grid=(n_steps,),
        in_specs=([pl.BlockSpec((TOK_TILE, n_exp), tile)] + [ye_spec(q) for q in range(grp)]
                  + [pl.BlockSpec((TOK_TILE, d), tile), pl.BlockSpec((None, 1, d), gate_map), vec, vec]),
        out_specs=pl.BlockSpec((TOK_TILE, d), tile),
        scratch_shapes=[pltpu.VMEM((TOK_TILE, d), F32)],
    )
    return pl.pallas_call(
        functools.partial(_combine_kernel, alpha, grp),
        grid_spec=grid_spec,
        out_shape=jax.ShapeDtypeStruct((n, d), F32),
        compiler_params=_cparams("arbitrary"),
        name="moe_combine_postnorm",
    )(*plan, posm_t, *([ye] * grp), x, gate, ln_g, ln_b)


def ec_moe_postnorm(x, mods_shift, mods_scale, gate, router_w, w1, w3, w2, layer, ln_g, ln_b, alpha):
    bsz, t, d = x.shape
    n = bsz * t
    n_exp = router_w.shape[1]
    cap = max(1, CAP_FACTOR * n // n_exp)
    assert cap % SLOT_BLK == 0 and n % TOK_TILE == 0
    h, aff = ln_modulate_router(x, mods_shift, mods_scale, router_w)
    h = h.reshape(n, d)
    aff = aff.reshape(n, LANE)
    mask = expert_choice_mask(aff[:, :n_exp].T, cap)
    posm, dispatch, combine = routing_plan(mask, cap)
    xe, aff_rows = moe_dispatch(h, aff, posm, dispatch, cap)
    ye = expert_ffn(xe, aff_rows, w1, w3, w2, layer, cap)
    out = moe_combine_postnorm(ye, posm.T, combine, x.reshape(n, d), gate, ln_g, ln_b, alpha, t)
    return out.reshape(bsz, t, d)


def prep_rwkv_params(rw, aw):
    a_cols = rw["mu"].shape[0]
    ap = round_up(a_cols, LANE)
    gp = ap - 3 * aw - 2 * LANE
    assert rw["w2"].shape[:2] == (2, LANE // 2) and rw["a2"].shape[:2] == (2, LANE // 2)
    assert gp >= rw["g2"].shape[0] and aw % LANE == 0
    row = lambda a: a.reshape(1, aw)
    return dict(
        mu=jnp.pad(rw["mu"], (0, ap - a_cols)).reshape(1, ap),
        w0=rw["w0"], w2=rw["w2"].reshape(LANE, aw), a0=rw["a0"], a2=rw["a2"].reshape(LANE, aw),
        g2=jnp.pad(rw["g2"], ((0, gp - rw["g2"].shape[0]), (0, 0))),
        k_k=row(rw["k_k"]), k_a=row(rw["k_a"]), r_k=row(rw["r_k"]),
        ln_w=row(rw["ln_w"]), ln_b=row(rw["ln_b"]), ap=ap, aw=aw)


def states_to_pairs(s):
    bsz, h = s.shape[:2]
    st = jnp.swapaxes(s, -1, -2).reshape(bsz, h // 2, 2, A_HEAD, A_HEAD)
    z = jnp.zeros_like(st[:, :, 0])
    top = jnp.concatenate([st[:, :, 0], z], axis=-1)
    bot = jnp.concatenate([z, st[:, :, 1]], axis=-1)
    return jnp.concatenate([top, bot], axis=-2)


def pairs_to_states(sp):
    bsz = sp.shape[0]
    st = jnp.stack([sp[:, :, :A_HEAD, :A_HEAD], sp[:, :, A_HEAD:, A_HEAD:]], axis=2)
    return jnp.swapaxes(st.reshape(bsz, -1, A_HEAD, A_HEAD), -1, -2)


def rwkv_branch(u, bsz, t, rwp, s0f, s0b):
    aw, ap = rwp["aw"], rwp["ap"]
    n = bsz * t
    pack_f, pack_b, bonus, g = rwkv_prep(u, t, rwp, aw, ap)
    sh = lambda a: a.reshape(bsz, t, SCAN_FIELDS * aw)
    yf, sf = rwkv_scan(sh(pack_f), states_to_pairs(s0f), False)
    yb, sb = rwkv_scan(sh(pack_b), states_to_pairs(s0b), True)
    ya = rwkv_post(yf.reshape(n, aw), yb.reshape(n, aw), bonus, g, rwp["ln_w"], rwp["ln_b"])
    return ya, pairs_to_states(sf), pairs_to_states(sb)


def _mod_rows(m):
    return [m[:, i][:, None, :] for i in range(6)]


def kernel(x_prompt, x_sample, cache_win_k, cache_win_v, cache_nb_k, cache_nb_v, state_rwkv_fwd, state_rwkv_bwd, c, c_ctx, ada_w, ada_b, w_in, rwkv_mu, rwkv_w0, rwkv_w2, rwkv_a0, rwkv_a2, rwkv_g2, rwkv_kk, rwkv_ka, rwkv_rk, rwkv_lnx_w, rwkv_lnx_b, win_sink, nb_rpb, w_br_a, w_br_b, w_br_c, w_out, ln1_g, ln1_b, ln2_g, ln2_b, router_w, exp_w1, exp_w3, exp_w2):
    depth, d, in_w = w_in.shape
    bsz, seq, _ = x_prompt.shape
    dbsz, dseq, _ = x_sample.shape
    aw = rwkv_w0.shape[-1]
    a_heads = aw // A_HEAD
    a_cols = rwkv_mu.shape[-1]
    ap = round_up(a_cols, LANE)
    n_b = win_sink.shape[-1]
    n_kv = cache_win_k.shape[3]
    group = n_b // n_kv
    n_c = nb_rpb.shape[1]
    bw, cw = n_b * HEAD_DIM, n_c * HEAD_DIM
    alpha = float((2 * depth) ** 0.25)
    assert in_w == a_cols + bw + 2 * n_kv * HEAD_DIM + 3 * cw + 3 * d

    qb0 = ap // LANE
    kb0 = qb0 + n_b
    vb0 = kb0 + n_kv
    qn0 = vb0 + n_kv
    kn0 = qn0 + n_c
    vn0 = kn0 + n_c
    gate_col0 = (vn0 + n_c) * LANE

    rows = 1 + dbsz
    cvecs = jnp.pad(jnp.concatenate([c_ctx[None], c], axis=0), ((0, round_up(rows, 16) - rows), (0, 0)))
    mods = ada_mods(cvecs, ada_w, ada_b).reshape(depth, -1, 6, d)

    xp, xs = x_prompt, x_sample
    np_tok, ns_tok = bsz * seq, dbsz * dseq
    win_k, win_v, nb_k, nb_v, st_f, st_b = [], [], [], [], [], []
    zero_state = jnp.zeros((bsz, a_heads, A_HEAD, A_HEAD), F32)
    w_cat = jnp.concatenate(
        [w_in[:, :, :a_cols].astype(BF16), jnp.zeros((depth, d, ap - a_cols), BF16),
         w_in[:, :, a_cols:].astype(BF16)], axis=2)
    w1, w3, w2 = exp_w1.astype(BF16), exp_w3.astype(BF16), exp_w2.astype(BF16)
    for l in range(depth):
        rwp = prep_rwkv_params(
            {"mu": rwkv_mu[l], "w0": rwkv_w0[l], "w2": rwkv_w2[l], "a0": rwkv_a0[l], "a2": rwkv_a2[l],
             "g2": rwkv_g2[l], "k_k": rwkv_kk[l], "k_a": rwkv_ka[l], "r_k": rwkv_rk[l],
             "ln_w": rwkv_lnx_w[l], "ln_b": rwkv_lnx_b[l]}, aw)
        wa, wb, wc = w_br_a[l].astype(BF16), w_br_b[l].astype(BF16), w_br_c[l].astype(BF16)
        wo = w_out[l].astype(BF16)
        g1, b1 = ln1_g[l].reshape(1, d), ln1_b[l].reshape(1, d)
        g2, b2 = ln2_g[l].reshape(1, d), ln2_b[l].reshape(1, d)

        sh1, sc1, gt1, sh2, sc2, gt2 = _mod_rows(mods[l, 0:1])
        h = ln_modulate(xp, sh1, sc1).reshape(np_tok, d)
        u = matmul(h, w_cat, l, name="in_proj")
        ya, s_f, s_b = rwkv_branch(u, bsz, seq, rwp, zero_state, zero_state)
        yb = ctx_attention(u, bsz, seq, qb0, kb0, vb0, n_b, group, win_sink[l])
        yc = ctx_attention(u, bsz, seq, qn0, kn0, vn0, n_c, 1, None)
        m = merge_branches(ya, yb, yc, wa, wb, wc, u, gate_col0)
        x1 = outproj_postnorm(m, wo, xp.reshape(np_tok, d), gt1, g1, b1, alpha, seq)
        xp = ec_moe_postnorm(x1.reshape(bsz, seq, d), sh2, sc2, gt2, router_w[l], w1, w3, w2, l, g2, b2, alpha)
        cols = lambda c0, nh: u[:, c0 * LANE:(c0 + nh) * LANE].reshape(bsz, seq, nh, HEAD_DIM)
        win_k.append(cols(kb0, n_kv))
        win_v.append(cols(vb0, n_kv))
        nb_k.append(cols(kn0, n_c))
        nb_v.append(cols(vn0, n_c))
        st_f.append(s_f)
        st_b.append(s_b)

        sh1, sc1, gt1, sh2, sc2, gt2 = _mod_rows(mods[l, 1:1 + dbsz])
        h = ln_modulate(xs, sh1, sc1).reshape(ns_tok, d)
        u = matmul(h, w_cat, l, name="in_proj")
        ya, _, _ = rwkv_branch(u, dbsz, dseq, rwp, state_rwkv_fwd[:, l], state_rwkv_bwd[:, l])
        qk_rot = rope(u, dbsz, dseq, qb0, n_b + n_kv)
        heads_first = lambda a: jnp.swapaxes(a[:, l], 1, 2)
        yb = window_attention(qk_rot, u, vb0, dbsz, dseq, n_kv, group,
                              heads_first(cache_win_k), heads_first(cache_win_v), win_sink[l])
        yc = neighbourhood_attention(u, (qn0, kn0, vn0), dbsz, dseq, n_c,
                                     heads_first(cache_nb_k), heads_first(cache_nb_v), nb_rpb[l])
        m = merge_branches(ya, yb, yc, wa, wb, wc, u, gate_col0)
        x1 = outproj_postnorm(m, wo, xs.reshape(ns_tok, d), gt1, g1, b1, alpha, dseq)
        xs = ec_moe_postnorm(x1.reshape(dbsz, dseq, d), sh2, sc2, gt2, router_w[l], w1, w3, w2, l, g2, b2, alpha)

    stack = lambda xs_: jnp.stack(xs_, axis=1)
    return (xp, xs, stack(win_k), stack(win_v), stack(nb_k), stack(nb_v), stack(st_f), stack(st_b))
```

```python
import functools

import numpy as np
import jax
import jax.numpy as jnp
from jax import lax
from jax.experimental import pallas as pl
from jax.experimental.pallas import tpu as pltpu

F32 = jnp.float32
BF16 = jnp.bfloat16

HEAD_DIM = 128
A_HEAD = 64
GRID_W = 64
WIN = 128
QBLK = 128
NB_ROWS = 8
NB_COLS = 16
CAP_FACTOR = 2
ROPE_THETA = 10000.0
LN_EPS = 1e-6
GN_EPS = 64e-5
NEG = -1e30

LANE = 128
SUBLANE = 8
VMEM_LIMIT_BYTES = 56 * 1024 * 1024

SCAN_CHUNK = 64
SLOT_BLK = 128
TOK_TILE = 256


def _cparams(*sem):
    return pltpu.CompilerParams(dimension_semantics=sem, vmem_limit_bytes=VMEM_LIMIT_BYTES)


def _dot(a, b):
    return jnp.dot(a, b, preferred_element_type=F32)


def _dot_nt(a, b):
    return lax.dot_general(a, b, (((1,), (1,)), ((), ())), preferred_element_type=F32)


def _split2(x):
    hi = x.astype(BF16)
    lo = (x - hi.astype(F32)).astype(BF16)
    return hi, lo


def _split3(x):
    hi = x.astype(BF16)
    r1 = x - hi.astype(F32)
    mid = r1.astype(BF16)
    lo = (r1 - mid.astype(F32)).astype(BF16)
    return hi, mid, lo


def _dot3(a, b):
    ah, al = _split2(a)
    bh, bl = _split2(b)
    return _dot(ah, bh) + (_dot(ah, bl) + _dot(al, bh))


def _dot3_nt(a, b):
    ah, al = _split2(a)
    bh, bl = _split2(b)
    return _dot_nt(ah, bh) + (_dot_nt(ah, bl) + _dot_nt(al, bh))


def _dot_exact_lhs(a_bf16, b):
    bh, bm, bl = _split3(b)
    return _dot(a_bf16, bh) + (_dot(a_bf16, bm) + _dot(a_bf16, bl))


def _sigmoid(x):
    return 1.0 / (1.0 + jnp.exp(-x))


def _iota(shape, dim):
    return lax.broadcasted_iota(jnp.int32, shape, dim)


def _ada_kernel(c_ref, w_ref, b_ref, o_ref):
    c = c_ref[...]
    a = c * _sigmoid(c)
    o_ref[...] = _dot3(a, w_ref[...]) + b_ref[...]


def ada_mods(cvecs, ada_w, ada_b, tn=512):
    depth, d, n6 = ada_w.shape
    rows = cvecs.shape[0]
    return pl.pallas_call(
        _ada_kernel,
        grid=(depth, n6 // tn),
        in_specs=[
            pl.BlockSpec((rows, d), lambda l, j: (0, 0)),
            pl.BlockSpec((None, d, tn), lambda l, j: (l, 0, j)),
            pl.BlockSpec((None, 1, tn), lambda l, j: (l, 0, j)),
        ],
        out_specs=pl.BlockSpec((None, rows, tn), lambda l, j: (l, 0, j)),
        out_shape=jax.ShapeDtypeStruct((depth, rows, n6), F32),
        compiler_params=_cparams("parallel", "parallel"),
        name="ada_mods",
    )(cvecs, ada_w, ada_b.reshape(depth, 1, n6))


def _ln_rows(x):
    mu = jnp.mean(x, axis=-1, keepdims=True)
    xc = x - mu
    var = jnp.mean(xc * xc, axis=-1, keepdims=True)
    return xc * lax.rsqrt(var + LN_EPS)


def _lnmod_kernel(x_ref, sh_ref, sc_ref, o_ref):
    y = _ln_rows(x_ref[...])
    o_ref[...] = (y * (1.0 + sc_ref[...]) + sh_ref[...]).astype(o_ref.dtype)


def _mod_spec(mod, d):
    if mod.shape[0] == 1:
        return pl.BlockSpec((None, 1, d), lambda b, i: (0, 0, 0))
    return pl.BlockSpec((None, 1, d), lambda b, i: (b, 0, 0))


def ln_modulate(x, shift, scale, tt=256):
    bsz, t, d = x.shape
    return pl.pallas_call(
        _lnmod_kernel,
        grid=(bsz, t // tt),
        in_specs=[
            pl.BlockSpec((None, tt, d), lambda b, i: (b, i, 0)),
            _mod_spec(shift, d),
            _mod_spec(scale, d),
        ],
        out_specs=pl.BlockSpec((None, tt, d), lambda b, i: (b, i, 0)),
        out_shape=jax.ShapeDtypeStruct((bsz, t, d), BF16),
        compiler_params=_cparams("parallel", "parallel"),
        name="ln_modulate",
    )(x, shift, scale)


def _lnmod_router_kernel(n_exp, seq_len, x_ref, sh_ref, sc_ref, rw_ref, h_ref, aff_ref):
    y = _ln_rows(x_ref[...])
    h = y * (1.0 + sc_ref[...]) + sh_ref[...]
    h_ref[...] = h
    logits = _dot3(h, rw_ref[...])
    lane = _iota(logits.shape, 1)
    logits = jnp.where(lane < n_exp, logits, NEG)
    m = jnp.max(logits, axis=-1, keepdims=True)
    e = jnp.exp(logits - m)
    aff = e / jnp.sum(e, axis=-1, keepdims=True)
    tt = aff.shape[0]
    tok = pl.program_id(0) * seq_len + pl.program_id(1) * tt + _iota(aff.shape, 0)
    aff_ref[...] = jnp.where(lane == n_exp, tok.astype(F32), aff)


def ln_modulate_router(x, shift, scale, router_w, tt=256):
    bsz, t, d = x.shape
    n_exp = router_w.shape[1]
    assert n_exp < LANE and bsz * t < (1 << 24)
    rw = jnp.pad(router_w, ((0, 0), (0, LANE - n_exp)))
    return pl.pallas_call(
        functools.partial(_lnmod_router_kernel, n_exp, t),
        grid=(bsz, t // tt),
        in_specs=[
            pl.BlockSpec((None, tt, d), lambda b, i: (b, i, 0)),
            _mod_spec(shift, d),
            _mod_spec(scale, d),
            pl.BlockSpec((d, LANE), lambda b, i: (0, 0)),
        ],
        out_specs=[
            pl.BlockSpec((None, tt, d), lambda b, i: (b, i, 0)),
            pl.BlockSpec((None, tt, LANE), lambda b, i: (b, i, 0)),
        ],
        out_shape=[
            jax.ShapeDtypeStruct((bsz, t, d), F32),
            jax.ShapeDtypeStruct((bsz, t, LANE), F32),
        ],
        compiler_params=_cparams("parallel", "parallel"),
        name="ln_modulate_router",
    )(x, shift, scale, rw)


def _mm_kernel(a_ref, b_ref, o_ref):
    o_ref[...] = _dot(a_ref[...], b_ref[...]).astype(o_ref.dtype)


def _pick(n, pref):
    for c in pref:
        if n % c == 0:
            return c
    return n


def matmul(a, b, layer, out_dtype=F32, tm=None, tn=None, name="matmul"):
    m, k = a.shape
    n = b.shape[2]
    tm = tm or _pick(m, (1024, 512, 256, 128))
    tn = tn or _pick(n, (768, 512, 256, 128))
    return pl.pallas_call(
        _mm_kernel,
        grid=(m // tm, n // tn),
        in_specs=[
            pl.BlockSpec((tm, k), lambda i, j: (i, 0)),
            pl.BlockSpec((None, k, tn), lambda i, j: (layer, 0, j)),
        ],
        out_specs=pl.BlockSpec((tm, tn), lambda i, j: (i, j)),
        out_shape=jax.ShapeDtypeStruct((m, n), out_dtype),
        compiler_params=_cparams("parallel", "parallel"),
        name=name,
    )(a, b)


HEAD_SHIFT = 6
assert (1 << HEAD_SHIFT) == A_HEAD and 2 * A_HEAD == LANE and SCAN_CHUNK == A_HEAD


def _head_block_ones():
    i = _iota((LANE, LANE), 0) >> HEAD_SHIFT
    j = _iota((LANE, LANE), 1) >> HEAD_SHIFT
    return (i == j).astype(BF16)


def _seg_sum(x):
    bd = _head_block_ones()
    outs = []
    for c in range(x.shape[1] // LANE):
        hi, mid, lo = _split3(x[:, c * LANE:(c + 1) * LANE])
        outs.append(_dot(hi, bd) + (_dot(mid, bd) + _dot(lo, bd)))
    return jnp.concatenate(outs, axis=1)


SCAN_FIELDS = 6


def _rwkv_prep_kernel(aw, gp, gw, tiles_per_seq,
                      x_ref, xp_ref, xn_ref, mu_ref, w0_ref, w2_ref, a0_ref, a2_ref, g2_ref,
                      kkp_ref, kap_ref, rkp_ref,
                      packf_o, packb_o, bonus_o, g_o):
    i = pl.program_id(0)
    x = x_ref[...]
    tt = x.shape[0]
    row = _iota(x.shape, 0)
    pos = i % tiles_per_seq
    prev_row = jnp.where(pos == 0, 0.0, xp_ref[SUBLANE - 1:SUBLANE, :])
    next_row = jnp.where(pos == tiles_per_seq - 1, 0.0, xn_ref[0:1, :])
    prev = jnp.where(row == 0, prev_row, pltpu.roll(x, 1, 0))
    nxt = jnp.where(row == tt - 1, next_row, pltpu.roll(x, tt - 1, 0))
    xs = x + mu_ref[...] * (0.5 * (prev + nxt) - x)

    r = xs[:, 0:aw]
    k = xs[:, aw:2 * aw]
    v = xs[:, 2 * aw:3 * aw]
    o = 3 * aw
    wlo = jnp.tanh(xs[:, o:o + LANE])
    alo = xs[:, o + LANE:o + 2 * LANE]
    glo = _sigmoid(xs[:, o + 2 * LANE:o + 2 * LANE + gp])
    lane = _iota((1, LANE), 1)
    exp_mhalf = float(np.exp(-0.5))

    g_o[...] = _dot3(glo, g2_ref[...])

    kk = k * kkp_ref[...]
    nrm = jnp.sqrt(_seg_sum(kk * kk))
    kk = kk / jnp.maximum(nrm, 1e-12)
    bonus_o[...] = _seg_sum(r * k * rkp_ref[...]) * v

    for d, pack_o in enumerate((packf_o, packb_o)):
        sel = ((lane >> HEAD_SHIFT) == d).astype(F32)
        wl = w0_ref[d:d + 1, :] + _dot3(wlo * sel, w2_ref[...])
        lw = -_sigmoid(wl) * exp_mhalf
        a = _sigmoid(a0_ref[d:d + 1, :] + _dot3(alo * sel, a2_ref[...]))
        fields = (lw, kk, kk * a, k * (1.0 + (a - 1.0) * kap_ref[...]), r, v)
        for g in range(aw // gw):
            for f, arr in enumerate(fields):
                c0 = (g * SCAN_FIELDS + f) * gw
                pack_o[:, c0:c0 + gw] = arr[:, g * gw:(g + 1) * gw]


def scan_group_width(aw, max_pairs=6):
    n_pairs = aw // LANE
    return LANE * max(p for p in range(1, max_pairs + 1) if n_pairs % p == 0)


def rwkv_prep(u, seq_len, rw, aw, ap, tt=128):
    n = u.shape[0]
    gw = scan_group_width(aw)
    gp = ap - 3 * aw - 2 * LANE
    n8 = n // SUBLANE
    tpb = tt // SUBLANE
    full = lambda shape: pl.BlockSpec(shape, lambda i: (0,) * len(shape))
    tok = pl.BlockSpec((tt, aw), lambda i: (i, 0))
    pack = pl.BlockSpec((tt, SCAN_FIELDS * aw), lambda i: (i, 0))
    pack_shape = jax.ShapeDtypeStruct((n, SCAN_FIELDS * aw), F32)
    outs = pl.pallas_call(
        functools.partial(_rwkv_prep_kernel, aw, gp, gw, seq_len // tt),
        grid=(n // tt,),
        in_specs=[
            pl.BlockSpec((tt, ap), lambda i: (i, 0)),
            pl.BlockSpec((SUBLANE, ap), lambda i: (jnp.maximum(i * tpb - 1, 0), 0)),
            pl.BlockSpec((SUBLANE, ap), lambda i: (jnp.minimum((i + 1) * tpb, n8 - 1), 0)),
            full((1, ap)), full((2, aw)), full((LANE, aw)), full((2, aw)), full((LANE, aw)),
            full((gp, aw)), full((1, aw)), full((1, aw)), full((1, aw)),
        ],
        out_specs=[pack, pack, tok, tok],
        out_shape=[pack_shape, pack_shape] + [jax.ShapeDtypeStruct((n, aw), F32)] * 2,
        compiler_params=_cparams("parallel"),
        name="rwkv_prep",
    )(u, u, u, rw["mu"], rw["w0"], rw["w2"], rw["a0"], rw["a2"], rw["g2"],
      rw["k_k"], rw["k_a"], rw["r_k"])
    return outs


def _scan_chunks(rev, toks, sts, consts):
    tri, strict, incl, blk, eye, head0, t2, s2 = consts
    c = SCAN_CHUNK
    ident = (t2 == s2).astype(F32)
    zero = jnp.zeros((), BF16)
    cat = lambda *xs: jnp.concatenate(xs, axis=0)

    def each(f, *lists):
        return [f(*args) for args in zip(*lists)]

    def bd(x):
        xb = x.astype(BF16)
        return jnp.where(blk, cat(xb, xb), zero)

    lw, kk, bb, kd, r, v = (list(t) for t in zip(*toks))
    big_l = each(lambda a: _dot_exact_lhs(tri, a), lw)
    l_tot = each(lambda a: a[0:1, :] if rev else a[c - 1:c, :], big_l)
    lhs = each(lambda k_, r_, l_, w_: cat(-k_ * jnp.exp(l_ - w_), r_ * jnp.exp(l_)).astype(BF16),
               kk, r, big_l, lw)
    gi = each(lambda l_: jnp.exp(-l_), big_l)
    bt = each(lambda b_, g_: (b_ * g_).astype(BF16), bb, gi)
    kt = each(lambda k_, g_: (k_ * g_).astype(BF16), kd, gi)
    rhs = each(lambda b_, k_: cat(jnp.where(head0, b_, zero), jnp.where(head0, zero, b_),
                                  jnp.where(head0, k_, zero), jnp.where(head0, zero, k_)), bt, kt)
    p = each(_dot_nt, lhs, rhs)
    nab = each(lambda p_: jnp.where(strict, p_[0:c, 0:LANE], 0.0).astype(BF16), p)
    nrb = each(lambda p_: jnp.where(incl, p_[c:2 * c, 0:LANE], 0.0).astype(BF16), p)
    nkk = each(lambda p_: cat(jnp.where(strict, p_[0:c, LANE:2 * LANE], 0.0),
                              jnp.where(incl, p_[c:2 * c, LANE:2 * LANE], 0.0)).astype(BF16), p)
    z = each(lambda l_, s_: _dot(l_, s_.astype(BF16)), lhs, sts)
    w = each(lambda n_, v_: _dot(n_, bd(v_)), nkk, v)
    x = each(lambda z_, w_: z_[0:c] + w_[0:c], z, w)
    blk8 = (t2 >> 3) == (s2 >> 3)
    n0 = each(lambda n_: jnp.where(blk8, n_, zero), nab)
    inv = each(lambda n_: ident + n_.astype(F32), n0)
    pw = each(lambda n_: _dot(n_, bd(n_)), n0)
    inv = each(lambda i_, p_: i_ + _dot(i_.astype(BF16), bd(p_)), inv, pw)
    pw = each(lambda p_: _dot(p_.astype(BF16), bd(p_)), pw)
    inv = each(lambda i_, p_: i_ + _dot(i_.astype(BF16), bd(p_)), inv, pw)
    for lb in range(3, HEAD_SHIFT):
        new = ((t2 >> (lb + 1)) == (s2 >> (lb + 1))) & ((t2 >> lb) != (s2 >> lb))
        half = each(lambda i_, n_: _dot(i_.astype(BF16), bd(jnp.where(new, n_, zero))), inv, nab)
        inv = each(lambda i_, h_: i_ + _dot(h_.astype(BF16), bd(i_)), inv, half)
    x = each(lambda i_, x_: _dot(i_.astype(BF16), bd(x_)), inv, x)
    y = each(lambda z_, w_, n_, x_: z_[c:2 * c] + w_[c:2 * c] + _dot(n_, bd(x_)), z, w, nrb, x)
    gr = each(lambda t_, l_: jnp.exp(t_ - l_), l_tot, big_l)
    upd = each(lambda b_, k_, g_, x_, v_: _dot3(cat(b_ * g_, k_ * g_).T, cat(x_, v_)), bb, kd, gr, x, v)
    g_col = each(lambda t_: jnp.sum(jnp.where(eye, jnp.exp(t_), 0.0), axis=1, keepdims=True), l_tot)
    st_new = each(lambda s_, g_, u_: s_ * g_ + jnp.where(blk, u_, 0.0), sts, g_col, upd)
    return y, st_new


def _scan_kernel(rev, npar, n_chunks, tok_ref, s0_ref, y_ref, st_out_ref, st_scr):
    ci = pl.program_id(2)

    @pl.when(ci == 0)
    def _():
        st_scr[...] = s0_ref[...]

    c = SCAN_CHUNK
    ti = _iota((c, c), 0)
    si = _iota((c, c), 1)
    tri = ((si >= ti) if rev else (si <= ti)).astype(BF16)
    t2 = _iota((c, LANE), 0)
    s2 = _iota((c, LANE), 1) & (c - 1)
    strict = (s2 > t2) if rev else (s2 < t2)
    incl = (s2 >= t2) if rev else (s2 <= t2)
    bi = _iota((LANE, LANE), 0)
    bj = _iota((LANE, LANE), 1)
    blk = (bi >> HEAD_SHIFT) == (bj >> HEAD_SHIFT)
    eye = bi == bj
    head0 = _iota((c, LANE), 1) < A_HEAD
    consts = (tri, strict, incl, blk, eye, head0, t2, s2)
    lanes = [slice(p * LANE, (p + 1) * LANE) for p in range(npar)]
    gw = npar * LANE
    toks = [tuple(tok_ref[:, f * gw + p * LANE:f * gw + (p + 1) * LANE] for f in range(SCAN_FIELDS))
            for p in range(npar)]
    ys, sts = _scan_chunks(rev, toks, [st_scr[p] for p in range(npar)], consts)
    for p, sl in enumerate(lanes):
        y_ref[:, sl] = ys[p]
        st_scr[p] = sts[p]

    @pl.when(ci == n_chunks - 1)
    def _():
        st_out_ref[...] = st_scr[...]


def rwkv_scan(packed, s0, rev):
    bsz, t, width = packed.shape
    aw = width // SCAN_FIELDS
    n_pairs = aw // LANE
    npar = scan_group_width(aw) // LANE
    n_chunks = t // SCAN_CHUNK
    if rev:
        tmap = lambda b, g, c: (b, n_chunks - 1 - c, g)
    else:
        tmap = lambda b, g, c: (b, c, g)
    tok = pl.BlockSpec((None, SCAN_CHUNK, npar * LANE), tmap)
    tok_in = pl.BlockSpec((None, SCAN_CHUNK, SCAN_FIELDS * npar * LANE), tmap)
    st = pl.BlockSpec((None, npar, LANE, LANE), lambda b, g, c: (b, g, 0, 0))
    return pl.pallas_call(
        functools.partial(_scan_kernel, rev, npar, n_chunks),
        grid=(bsz, n_pairs // npar, n_chunks),
        in_specs=[tok_in, st],
        out_specs=[tok, st],
        out_shape=[jax.ShapeDtypeStruct((bsz, t, aw), F32),
                   jax.ShapeDtypeStruct(s0.shape, F32)],
        scratch_shapes=[pltpu.VMEM((npar, LANE, LANE), F32)],
        compiler_params=_cparams("parallel", "parallel", "arbitrary"),
        name="rwkv_scan_bwd" if rev else "rwkv_scan_fwd",
    )(packed, s0)


def _rwkv_post_kernel(yf_ref, yb_ref, bonus_ref, g_ref, lnw_ref, lnb_ref, o_ref):
    y = yf_ref[...] + yb_ref[...]
    inv = 1.0 / A_HEAD
    mu = _seg_sum(y) * inv
    yc = y - mu
    var = _seg_sum(yc * yc) * inv
    yn = yc * lax.rsqrt(var + GN_EPS) * lnw_ref[...] + lnb_ref[...]
    o_ref[...] = ((yn + bonus_ref[...]) * g_ref[...]).astype(o_ref.dtype)


def rwkv_post(yf, yb, bonus, g, ln_w, ln_b, tt=256):
    n, aw = yf.shape
    tok = pl.BlockSpec((tt, aw), lambda i: (i, 0))
    par = pl.BlockSpec((1, aw), lambda i: (0, 0))
    return pl.pallas_call(
        _rwkv_post_kernel,
        grid=(n // tt,),
        in_specs=[tok, tok, tok, tok, par, par],
        out_specs=tok,
        out_shape=jax.ShapeDtypeStruct((n, aw), BF16),
        compiler_params=_cparams("parallel"),
        name="rwkv_post",
    )(yf, yb, bonus, g, ln_w, ln_b)


def round_up(x, m):
    return (x + m - 1) // m * m


ATT_SCALE = HEAD_DIM ** -0.5


def _ctx_attn_kernel(q_ref, k_ref, v_ref, sink_ref, o_ref):
    s = _dot3_nt(q_ref[...] * ATT_SCALE, k_ref[...])
    sk = sink_ref[0:1, 0:1]
    m = jnp.maximum(jnp.max(s, axis=-1, keepdims=True), sk)
    e = jnp.exp(s - m)
    den = jnp.sum(e, axis=-1, keepdims=True) + jnp.exp(sk - m)
    o = _dot(e.astype(BF16), v_ref[...].astype(BF16))
    o_ref[...] = (o / den).astype(o_ref.dtype)


def ctx_attention(u, bsz, t, q0, k0, v0, n_heads, group, sink):
    sink = jnp.full((n_heads,), NEG, F32) if sink is None else sink.astype(F32)
    sink = jnp.broadcast_to(sink[:, None, None], (n_heads, 1, LANE))
    return pl.pallas_call(
        _ctx_attn_kernel,
        grid=(bsz, n_heads),
        in_specs=[
            pl.BlockSpec((t, HEAD_DIM), lambda b, h: (b, q0 + h)),
            pl.BlockSpec((t, HEAD_DIM), lambda b, h: (b, k0 + h // group)),
            pl.BlockSpec((t, HEAD_DIM), lambda b, h: (b, v0 + h // group)),
            pl.BlockSpec((None, 1, LANE), lambda b, h: (h, 0, 0)),
        ],
        out_specs=pl.BlockSpec((t, HEAD_DIM), lambda b, h: (b, h)),
        out_shape=jax.ShapeDtypeStruct((bsz * t, n_heads * HEAD_DIM), BF16),
        compiler_params=_cparams("parallel", "parallel"),
        name="ctx_attention",
    )(u, u, u, sink)


def rope_tables(n_tok):
    half = HEAD_DIM // 2
    quarter = half // 2
    tok = jnp.arange(n_tok)
    row = (tok // GRID_W).astype(F32)
    col = (tok % GRID_W).astype(F32)
    inv = ROPE_THETA ** (-jnp.arange(quarter, dtype=F32) / quarter)
    ang_r = row[:, None] * inv[None]
    ang_c = col[:, None] * inv[None]
    cos = jnp.concatenate([jnp.cos(ang_r)] * 2 + [jnp.cos(ang_c)] * 2, axis=1)
    sr, sc = jnp.sin(ang_r), jnp.sin(ang_c)
    z = jnp.zeros_like(sr)
    sin_a = jnp.concatenate([-sr, z, -sc, z], axis=1)
    sin_b = jnp.concatenate([z, sr, z, sc], axis=1)
    return cos, sin_a, sin_b


def _rope_kernel(n_heads, x_ref, cos_ref, sa_ref, sb_ref, o_ref):
    q = HEAD_DIM // 4
    cos, sa, sb = cos_ref[...], sa_ref[...], sb_ref[...]
    for h in range(n_heads):
        sl = slice(h * HEAD_DIM, (h + 1) * HEAD_DIM)
        x = x_ref[:, sl]
        o_ref[:, sl] = x * cos + pltpu.roll(x, HEAD_DIM - q, 1) * sa + pltpu.roll(x, q, 1) * sb


def rope(u, bsz, n_tok, col0, n_heads, tt=256):
    cos, sin_a, sin_b = rope_tables(n_tok)
    nt = n_tok // tt
    hb = max(k for k in range(1, n_heads + 1) if n_heads % k == 0 and col0 % k == 0)
    width = hb * HEAD_DIM
    cb = col0 // hb
    tab = pl.BlockSpec((tt, HEAD_DIM), lambda b, i, j: (i, 0))
    return pl.pallas_call(
        functools.partial(_rope_kernel, hb),
        grid=(bsz, nt, n_heads // hb),
        in_specs=[pl.BlockSpec((tt, width), lambda b, i, j: (b * nt + i, cb + j)), tab, tab, tab],
        out_specs=pl.BlockSpec((tt, width), lambda b, i, j: (b * nt + i, j)),
        out_shape=jax.ShapeDtypeStruct((bsz * n_tok, n_heads * HEAD_DIM), F32),
        compiler_params=_cparams("parallel", "parallel", "parallel"),
        name="rope",
    )(u, cos, sin_a, sin_b)


def _win_attn_kernel(group, n_tok, sink_ref, q_ref, kp_ref, kc_ref, kn_ref, vp_ref, vc_ref, vn_ref,
                     ck_ref, cv_ref, o_ref):
    kv = pl.program_id(1)
    i = pl.program_id(2)
    rows = group * QBLK
    q = jnp.concatenate([q_ref[:, g * HEAD_DIM:(g + 1) * HEAD_DIM] for g in range(group)], axis=0)
    q = q * ATT_SCALE
    k_loc = jnp.concatenate([kp_ref[...], kc_ref[...], kn_ref[...]], axis=0)
    v_loc = jnp.concatenate([vp_ref[...], vc_ref[...], vn_ref[...]], axis=0).astype(BF16)
    s_loc = _dot3_nt(q, k_loc)
    qi = _iota(s_loc.shape, 0) & (QBLK - 1)
    kj = _iota(s_loc.shape, 1)
    pos = i * QBLK - WIN + kj
    ok = (kj - qi >= 0) & (kj - qi <= 2 * WIN) & (pos >= 0) & (pos < n_tok)
    s_loc = jnp.where(ok, s_loc, NEG)
    s_ctx = _dot3_nt(q, ck_ref[...])
    rg = _iota((rows, 1), 0) >> int(np.log2(QBLK))
    sk = jnp.zeros((rows, 1), F32)
    for g in range(group):
        sk = jnp.where(rg == g, sink_ref[kv * group + g], sk)
    m = jnp.maximum(jnp.maximum(jnp.max(s_loc, axis=-1, keepdims=True),
                                jnp.max(s_ctx, axis=-1, keepdims=True)), sk)
    e_loc = jnp.exp(s_loc - m)
    e_ctx = jnp.exp(s_ctx - m)
    den = (jnp.sum(e_loc, axis=-1, keepdims=True) + jnp.sum(e_ctx, axis=-1, keepdims=True)
           + jnp.exp(sk - m))
    o = (_dot(e_loc.astype(BF16), v_loc) + _dot(e_ctx.astype(BF16), cv_ref[...].astype(BF16))) / den
    for g in range(group):
        o_ref[:, g * HEAD_DIM:(g + 1) * HEAD_DIM] = o[g * QBLK:(g + 1) * QBLK].astype(o_ref.dtype)


def window_attention(qk_rot, u, vb0, bsz, n_tok, n_kv, group, ck, cv, sink):
    assert WIN == QBLK
    nb = n_tok // QBLK
    nq = n_kv * group
    past = ck.shape[2]
    prv = lambda i: jnp.maximum(i - 1, 0)
    nxt = lambda i: jnp.minimum(i + 1, nb - 1)
    blk = lambda f, c0: pl.BlockSpec((QBLK, HEAD_DIM), lambda b, kv, i, s: (b * nb + f(i), c0 + kv))
    same = lambda i: i
    cache = pl.BlockSpec((None, None, past, HEAD_DIM), lambda b, kv, i, s: (b, kv, 0, 0))
    grid_spec = pltpu.PrefetchScalarGridSpec(
        num_scalar_prefetch=1,
        grid=(bsz, n_kv, nb),
        in_specs=[
            pl.BlockSpec((QBLK, group * HEAD_DIM), lambda b, kv, i, s: (b * nb + i, kv)),
            blk(prv, nq), blk(same, nq), blk(nxt, nq),
            blk(prv, vb0), blk(same, vb0), blk(nxt, vb0),
            cache, cache,
        ],
        out_specs=pl.BlockSpec((QBLK, group * HEAD_DIM), lambda b, kv, i, s: (b * nb + i, kv)),
    )
    return pl.pallas_call(
        functools.partial(_win_attn_kernel, group, n_tok),
        grid_spec=grid_spec,
        out_shape=jax.ShapeDtypeStruct((bsz * n_tok, nq * HEAD_DIM), BF16),
        compiler_params=_cparams("parallel", "parallel", "parallel"),
        name="window_attention",
    )(sink.astype(F32), qk_rot, qk_rot, qk_rot, qk_rot, u, u, u, ck, cv)


NB_QROWS = 4
NB_QTOK = NB_QROWS * GRID_W
NB_KTOK = 3 * NB_QTOK


def nb_tables(rpb, rows):
    kh = min(NB_ROWS, rows)
    ql = np.arange(NB_QTOK)
    kl = np.arange(NB_KTOK)
    r_rel, c = ql // GRID_W, ql % GRID_W
    kr_rel, kc = kl // GRID_W - NB_QROWS, kl % GRID_W
    n_heads = rpb.shape[0]
    gcol = np.arange(GRID_W)
    col_idx = np.clip(gcol[None, :] - gcol[:, None] + NB_COLS - 1, 0, 2 * NB_COLS - 2)
    t_col = jnp.take(rpb.astype(F32), jnp.asarray(col_idx.reshape(-1)), axis=2)
    t_col = t_col.reshape(n_heads, 2 * NB_ROWS - 1, GRID_W * GRID_W)
    qr = np.arange(NB_QROWS)
    krr = np.arange(3 * NB_QROWS) - NB_QROWS
    row_idx = np.clip(krr[None, :] - qr[:, None] + NB_ROWS - 1, 0, 2 * NB_ROWS - 2)
    bias = jnp.take(t_col, jnp.asarray(row_idx.reshape(-1)), axis=1)
    bias = bias.reshape(n_heads, NB_QROWS, 3 * NB_QROWS, GRID_W, GRID_W)
    bias = jnp.transpose(bias, (0, 1, 3, 2, 4)).reshape(n_heads, NB_QTOK, NB_KTOK)
    win_start = np.clip(c - NB_COLS // 2, 0, GRID_W - NB_COLS)
    col_ok = (kc[None, :] >= win_start[:, None]) & (kc[None, :] < win_start[:, None] + NB_COLS)
    masks = []
    for j in range(rows // NB_QROWS):
        r = j * NB_QROWS + r_rel
        kr = j * NB_QROWS + kr_rel
        row_start = np.clip(r - kh // 2, 0, rows - kh)
        row_ok = (kr[None, :] >= row_start[:, None]) & (kr[None, :] < row_start[:, None] + kh)
        masks.append(row_ok & col_ok)
    return bias, jnp.asarray(np.stack(masks).astype(np.float32))


def _nb_attn_kernel(q_ref, kp_ref, kc_ref, kn_ref, vp_ref, vc_ref, vn_ref, ck_ref, cv_ref,
                    bias_ref, mask_ref, o_ref):
    q = q_ref[...] * ATT_SCALE
    k_loc = jnp.concatenate([kp_ref[...], kc_ref[...], kn_ref[...]], axis=0)
    v_loc = jnp.concatenate([vp_ref[...], vc_ref[...], vn_ref[...]], axis=0).astype(BF16)
    s_loc = jnp.where(mask_ref[...] > 0.0, _dot3_nt(q, k_loc) + bias_ref[...], NEG)
    s_ctx = _dot3_nt(q, ck_ref[...])
    m = jnp.maximum(jnp.max(s_loc, axis=-1, keepdims=True), jnp.max(s_ctx, axis=-1, keepdims=True))
    e_loc = jnp.exp(s_loc - m)
    e_ctx = jnp.exp(s_ctx - m)
    den = jnp.sum(e_loc, axis=-1, keepdims=True) + jnp.sum(e_ctx, axis=-1, keepdims=True)
    o = _dot(e_loc.astype(BF16), v_loc) + _dot(e_ctx.astype(BF16), cv_ref[...].astype(BF16))
    o_ref[...] = (o / den).astype(o_ref.dtype)


def neighbourhood_attention(u, cols, bsz, n_tok, n_heads, ck, cv, rpb):
    q0, k0, v0 = cols
    rows = n_tok // GRID_W
    assert rows % NB_QROWS == 0 and rows >= NB_ROWS
    nj = rows // NB_QROWS
    past = ck.shape[2]
    bias, mask = nb_tables(rpb, rows)
    prv = lambda j: jnp.maximum(j - 1, 0)
    nxt = lambda j: jnp.minimum(j + 1, nj - 1)
    same = lambda j: j
    blk = lambda f, c0: pl.BlockSpec((NB_QTOK, HEAD_DIM), lambda b, h, j: (b * nj + f(j), c0 + h))
    cache = pl.BlockSpec((None, None, past, HEAD_DIM), lambda b, h, j: (b, h, 0, 0))
    return pl.pallas_call(
        _nb_attn_kernel,
        grid=(bsz, n_heads, nj),
        in_specs=[
            blk(same, q0),
            blk(prv, k0), blk(same, k0), blk(nxt, k0),
            blk(prv, v0), blk(same, v0), blk(nxt, v0),
            cache, cache,
            pl.BlockSpec((None, NB_QTOK, NB_KTOK), lambda b, h, j: (h, 0, 0)),
            pl.BlockSpec((None, NB_QTOK, NB_KTOK), lambda b, h, j: (j, 0, 0)),
        ],
        out_specs=pl.BlockSpec((NB_QTOK, HEAD_DIM), lambda b, h, j: (b * nj + j, h)),
        out_shape=jax.ShapeDtypeStruct((bsz * n_tok, n_heads * HEAD_DIM), BF16),
        compiler_params=_cparams("parallel", "parallel", "parallel"),
        name="neighbourhood_attention",
    )(u, u, u, u, u, u, u, ck, cv, bias, mask)


def _merge_kernel(ya_ref, yb_ref, yc_ref, wa_ref, wb_ref, wc_ref, ga_ref, gb_ref, gc_ref, o_ref):
    m = _sigmoid(ga_ref[...]) * _dot(ya_ref[...], wa_ref[...])
    m = m + _sigmoid(gb_ref[...]) * _dot(yb_ref[...], wb_ref[...])
    m = m + _sigmoid(gc_ref[...]) * _dot(yc_ref[...], wc_ref[...])
    o_ref[...] = m.astype(o_ref.dtype)


def merge_branches(ya, yb, yc, wa, wb, wc, u, gate_col0, tm=512, tn=512):
    n, aw = ya.shape
    bw, cw = wb.shape[0], wc.shape[0]
    d = wa.shape[1]
    tm = _pick(n, (tm, 256, 128))
    tn = _pick(d, (tn, 256, 128))
    assert gate_col0 % tn == 0
    g0 = gate_col0 // tn
    nd = d // tn
    gate = lambda gi: pl.BlockSpec((tm, tn), lambda i, j: (i, g0 + gi * nd + j))
    return pl.pallas_call(
        _merge_kernel,
        grid=(n // tm, nd),
        in_specs=[
            pl.BlockSpec((tm, aw), lambda i, j: (i, 0)),
            pl.BlockSpec((tm, bw), lambda i, j: (i, 0)),
            pl.BlockSpec((tm, cw), lambda i, j: (i, 0)),
            pl.BlockSpec((aw, tn), lambda i, j: (0, j)),
            pl.BlockSpec((bw, tn), lambda i, j: (0, j)),
            pl.BlockSpec((cw, tn), lambda i, j: (0, j)),
            gate(0), gate(1), gate(2),
        ],
        out_specs=pl.BlockSpec((tm, tn), lambda i, j: (i, j)),
        out_shape=jax.ShapeDtypeStruct((n, d), BF16),
        compiler_params=_cparams("parallel", "parallel"),
        name="merge_branches",
    )(ya, yb, yc, wa, wb, wc, u, u, u)


def _post_norm(alpha, x, gate, y, g, b):
    r = alpha * x + gate * y
    return _ln_rows(r) * g + b


def _outproj_ln_kernel(alpha, nk, m_ref, w_ref, x_ref, gate_ref, g_ref, b_ref, o_ref):
    k = pl.program_id(1)

    @pl.when(k == 0)
    def _():
        o_ref[...] = _dot(m_ref[...], w_ref[...])

    @pl.when(k > 0)
    def _():
        o_ref[...] += _dot(m_ref[...], w_ref[...])

    @pl.when(k == nk - 1)
    def _():
        rows = o_ref.shape[0]
        step = min(rows, 128)
        for r0 in range(0, rows, step):
            sl = slice(r0, r0 + step)
            o_ref[sl, :] = _post_norm(alpha, x_ref[sl, :], gate_ref[...], o_ref[sl, :],
                                      g_ref[...], b_ref[...])


def _tile_mod_spec(mod, d, tm, seq_len):
    if mod.shape[0] == 1:
        return pl.BlockSpec((None, 1, d), lambda i, *_: (0, 0, 0))
    return pl.BlockSpec((None, 1, d), lambda i, *_: (i * tm // seq_len, 0, 0))


def outproj_postnorm(m, w_out, x, gate, ln_g, ln_b, alpha, seq_len, tm=512, tk=512):
    n, d = x.shape
    kdim = m.shape[1]
    tm = _pick(n if gate.shape[0] == 1 else seq_len, (tm, 256, 128))
    tk = _pick(kdim, (tk, 256, 128))
    nk = kdim // tk
    vec = pl.BlockSpec((1, d), lambda i, k: (0, 0))
    return pl.pallas_call(
        functools.partial(_outproj_ln_kernel, alpha, nk),
        grid=(n // tm, nk),
        in_specs=[
            pl.BlockSpec((tm, tk), lambda i, k: (i, k)),
            pl.BlockSpec((tk, d), lambda i, k: (k, 0)),
            pl.BlockSpec((tm, d), lambda i, k: (i, 0), pipeline_mode=pl.Buffered(1)),
            _tile_mod_spec(gate, d, tm, seq_len),
            vec, vec,
        ],
        out_specs=pl.BlockSpec((tm, d), lambda i, k: (i, 0)),
        out_shape=jax.ShapeDtypeStruct((n, d), F32),
        compiler_params=_cparams("parallel", "arbitrary"),
        name="outproj_postnorm",
    )(m, w_out, x, gate, ln_g, ln_b)


def _select_kernel(cap, n, aff_ref, mask_ref):
    bits = lax.bitcast_convert_type(aff_ref[...], jnp.int32)
    n_exp = bits.shape[0]
    capf = jnp.float32(cap)

    def count(pred):
        return jnp.sum(pred.astype(F32), axis=1, keepdims=True)

    def value_step(_, carry):
        lo, hi = carry
        mid = lo + ((hi - lo + 1) >> 1)
        ok = count(bits >= mid) >= capf
        return jnp.where(ok, mid, lo), jnp.where(ok, hi, mid - 1)

    lo0 = jnp.zeros((n_exp, 1), jnp.int32)
    hi0 = jnp.full((n_exp, 1), 0x7F800000, jnp.int32)
    thr, _ = lax.fori_loop(0, 32, value_step, (lo0, hi0))
    gt = bits > thr
    eq = bits == thr
    need = capf - count(gt)
    idx = _iota(bits.shape, 1)

    def index_step(_, carry):
        lo, hi = carry
        mid = (lo + hi) >> 1
        ok = count(eq & (idx < mid)) >= need
        return jnp.where(ok, lo, mid), jnp.where(ok, mid, hi)

    _, bound = lax.fori_loop(0, int(np.ceil(np.log2(n))) + 1, index_step,
                             (jnp.zeros((n_exp, 1), jnp.int32), jnp.full((n_exp, 1), n, jnp.int32)))
    mask_ref[...] = (gt | (eq & (idx < bound))).astype(jnp.int32)


def expert_choice_mask(aff_t, cap):
    n_exp, n = aff_t.shape
    return pl.pallas_call(
        functools.partial(_select_kernel, cap, n),
        out_shape=jax.ShapeDtypeStruct((n_exp, n), jnp.int32),
        compiler_params=pltpu.CompilerParams(vmem_limit_bytes=VMEM_LIMIT_BYTES),
        name="expert_choice_mask",
    )(aff_t)


def _count_le(sorted_vals, x):
    return jnp.sum((sorted_vals[None, :] <= x[:, None]).astype(jnp.int32), axis=1)


def _visit_list(nvis, first_blk, n_work):
    na, nb = nvis.shape
    flat = nvis.reshape(-1)
    off_end = jnp.cumsum(flat)
    off_start = off_end - flat
    total = off_end[-1]
    w = jnp.minimum(jnp.arange(n_work, dtype=jnp.int32), total - 1)
    idx = jnp.minimum(_count_le(off_end, w), na * nb - 1)
    blk = first_blk.reshape(-1)[idx] + (w - off_start[idx])
    valid = (jnp.arange(n_work) < total).astype(jnp.int32)
    return idx // nb, idx % nb, blk.astype(jnp.int32), valid


MOE_GROUP = 4
NO_SLOT = 1 << 28


def _group_items(key, valid, n_keys, group, n_steps):
    n_items = key.shape[0]
    cnt = jnp.zeros((n_keys,), jnp.int32).at[key].add(valid)
    start = jnp.cumsum(cnt) - cnt
    per_key = (cnt + group - 1) // group
    s_end = jnp.cumsum(per_key)
    s_start = s_end - per_key
    total = s_end[-1]
    step = jnp.arange(n_steps, dtype=jnp.int32)
    real = step < total
    sc = jnp.minimum(step, total - 1)
    k = jnp.minimum(_count_le(s_end, sc), n_keys - 1)
    j = sc - s_start[k]
    within = j[None, :] * group + jnp.arange(group, dtype=jnp.int32)[:, None]
    ok = (within < cnt[k][None, :]) & real[None, :]
    item = jnp.clip(start[k][None, :] + jnp.minimum(within, cnt[k][None, :] - 1), 0, n_items - 1)
    first = ((j == 0) & real).astype(jnp.int32)
    last = ((j == per_key[k] - 1) & real).astype(jnp.int32)
    return item, ok.astype(jnp.int32), first, last


def routing_plan(mask, cap):
    n_exp, n = mask.shape
    nt = n // TOK_TILE
    nsb = cap // SLOT_BLK
    pos = jnp.cumsum(mask, axis=1) - mask
    posm = jnp.where(mask > 0, pos, -1).astype(jnp.int32)
    cnt = mask.reshape(n_exp, nt, TOK_TILE).sum(-1)
    cend = jnp.cumsum(cnt, axis=1)
    cstart = cend - cnt
    sb_lo = jnp.minimum(cstart // SLOT_BLK, nsb - 1)
    nvis = jnp.where(cnt > 0, (cend - 1) // SLOT_BLK - sb_lo + 1, 0)
    n_work = n_exp * (nsb + nt)
    n_blk = n_exp * nsb
    grp = MOE_GROUP
    e, t, sb, valid = _visit_list(nvis, sb_lo, n_work)
    blk = e * nsb + sb
    item, ok, first, last = _group_items(blk, valid, n_blk, grp, n_work // grp + n_blk + 1)
    blk_g = blk[item[0]]
    dispatch = (blk_g // nsb, blk_g, first, last, t[item], jnp.where(ok > 0, sb[item] * SLOT_BLK, NO_SLOT))
    nvis_t = nvis.T.at[:, 0].max(1)
    t2, e2, sb2, valid2 = _visit_list(nvis_t, sb_lo.T, n_work + nt)
    item, ok, first, last = _group_items(t2, valid2, nt, grp, (n_work + nt) // grp + nt + 1)
    combine = (t2[item[0]], first, last, e2[item], e2[item] * nsb + sb2[item],
               jnp.where(ok > 0, sb2[item] * SLOT_BLK, NO_SLOT))
    return posm, dispatch, combine


def _route_kernel(group, e_ref, blk_ref, first_ref, last_ref, tile_ref, base_ref, *refs):
    posm_refs, aff_refs = refs[:group], refs[group:2 * group]
    o_ref, acc_ref = refs[2 * group:]
    w = pl.program_id(0)
    rows = _iota((SLOT_BLK, TOK_TILE), 0)
    tot = None
    for q in range(group):
        onehot = ((rows + base_ref[q, w]) == posm_refs[q][...]).astype(BF16)
        part = _dot_exact_lhs(onehot, aff_refs[q][...])
        tot = part if tot is None else tot + part

    @pl.when(first_ref[w] == 1)
    def _():
        acc_ref[...] = tot

    @pl.when(first_ref[w] == 0)
    def _():
        acc_ref[...] += tot

    @pl.when(last_ref[w] == 1)
    def _():
        o_ref[...] = acc_ref[...]


def moe_route(aff, posm, plan, cap):
    n = aff.shape[0]
    n_exp = posm.shape[0]
    grp = MOE_GROUP
    n_steps = plan[0].shape[0]
    pos_spec = lambda q: pl.BlockSpec((None, 1, TOK_TILE), lambda w, e, b, f, l, t, s: (e[w], 0, t[q, w]))
    aff_spec = lambda q: pl.BlockSpec((TOK_TILE, LANE), lambda w, e, b, f, l, t, s: (t[q, w], 0))
    grid_spec = pltpu.PrefetchScalarGridSpec(
        num_scalar_prefetch=6,
        grid=(n_steps,),
        in_specs=[pos_spec(q) for q in range(grp)] + [aff_spec(q) for q in range(grp)],
        out_specs=pl.BlockSpec((SLOT_BLK, LANE), lambda w, e, b, f, l, t, s: (b[w], 0)),
        scratch_shapes=[pltpu.VMEM((SLOT_BLK, LANE), F32)],
    )
    posm3 = posm.reshape(n_exp, 1, n)
    return pl.pallas_call(
        functools.partial(_route_kernel, grp),
        grid_spec=grid_spec,
        out_shape=jax.ShapeDtypeStruct((n_exp * cap, LANE), F32),
        compiler_params=_cparams("arbitrary"),
        name="moe_route",
    )(*plan, *([posm3] * grp), *([aff] * grp))


GATHER_ROWS = 256


def _gather_kernel(idx_ref, tok_hbm, o_ref, buf, sem):
    base = pl.program_id(0) * GATHER_ROWS

    def row_copy(r):
        return pltpu.make_async_copy(tok_hbm.at[pl.ds(idx_ref[base + r], 1), :],
                                     buf.at[pl.ds(r, 1), :], sem)

    def start(r, carry):
        row_copy(r).start()
        return carry

    def wait(r, carry):
        row_copy(r).wait()
        return carry

    lax.fori_loop(0, GATHER_ROWS, start, 0, unroll=8)
    lax.fori_loop(0, GATHER_ROWS, wait, 0, unroll=8)
    o_ref[...] = buf[...].astype(o_ref.dtype)


def moe_gather(tokens, idx):
    n, d = tokens.shape
    rows = idx.shape[0]
    assert rows % GATHER_ROWS == 0
    grid_spec = pltpu.PrefetchScalarGridSpec(
        num_scalar_prefetch=1,
        grid=(rows // GATHER_ROWS,),
        in_specs=[pl.BlockSpec(memory_space=pl.ANY)],
        out_specs=pl.BlockSpec((GATHER_ROWS, d), lambda i, idx_ref: (i, 0)),
        scratch_shapes=[pltpu.VMEM((GATHER_ROWS, d), F32), pltpu.SemaphoreType.DMA(())],
    )
    return pl.pallas_call(
        _gather_kernel,
        grid_spec=grid_spec,
        out_shape=jax.ShapeDtypeStruct((rows, d), BF16),
        compiler_params=_cparams("arbitrary"),
        name="moe_gather",
    )(idx, tokens)


def _ffn_up_kernel(x_ref, w1_ref, w3_ref, o_ref):
    x = x_ref[...]
    a = _dot(x, w1_ref[...])
    o_ref[...] = (a * _sigmoid(a) * _dot(x, w3_ref[...])).astype(o_ref.dtype)


def _ffn_down_kernel(h_ref, w2_ref, aff_ref, o_ref):
    aff = aff_ref[...]
    gval = jnp.sum(jnp.where(_iota(aff.shape, 1) == pl.program_id(0), aff, 0.0), axis=1, keepdims=True)
    o_ref[...] = (_dot(h_ref[...], w2_ref[...]) * gval).astype(o_ref.dtype)


def expert_ffn(xe, aff_rows, w1, w3, w2, layer, cap, tn=512):
    _, n_exp, d, ff = w1.shape
    tm = _pick(cap, (1024, 512, 256, 128))
    nm = cap // tm
    tf = _pick(ff, (tn, 256, 128))
    hid = pl.pallas_call(
        _ffn_up_kernel,
        grid=(n_exp, nm, ff // tf),
        in_specs=[
            pl.BlockSpec((tm, d), lambda e, i, j: (e * nm + i, 0)),
            pl.BlockSpec((None, None, d, tf), lambda e, i, j: (layer, e, 0, j)),
            pl.BlockSpec((None, None, d, tf), lambda e, i, j: (layer, e, 0, j)),
        ],
        out_specs=pl.BlockSpec((tm, tf), lambda e, i, j: (e * nm + i, j)),
        out_shape=jax.ShapeDtypeStruct((n_exp * cap, ff), BF16),
        compiler_params=_cparams("parallel", "parallel", "parallel"),
        name="expert_ffn_up",
    )(xe, w1, w3)
    td = _pick(d, (tn, 256, 128))
    return pl.pallas_call(
        _ffn_down_kernel,
        grid=(n_exp, nm, d // td),
        in_specs=[
            pl.BlockSpec((tm, ff), lambda e, i, j: (e * nm + i, 0)),
            pl.BlockSpec((None, None, ff, td), lambda e, i, j: (layer, e, 0, j)),
            pl.BlockSpec((tm, LANE), lambda e, i, j: (e * nm + i, 0)),
        ],
        out_specs=pl.BlockSpec((tm, td), lambda e, i, j: (e * nm + i, j)),
        out_shape=jax.ShapeDtypeStruct((n_exp * cap, d), BF16),
        compiler_params=_cparams("parallel", "parallel", "parallel"),
        name="expert_ffn_down",
    )(hid, w2, aff_rows)


def _combine_kernel(alpha, group, tile_ref, first_ref, last_ref, e_ref, blk_ref, base_ref,
                    posm_ref, *refs):
    ye_refs = refs[:group]
    x_ref, gate_ref, g_ref, b_ref, o_ref, acc_ref = refs[group:]
    w = pl.program_id(0)
    posm = posm_ref[...].astype(F32)
    lane = _iota(posm.shape, 1)
    cols = _iota((TOK_TILE, SLOT_BLK), 1)
    tot = None
    for q in range(group):
        col = jnp.sum(jnp.where(lane == e_ref[q, w], posm, 0.0), axis=1, keepdims=True)
        onehot = ((cols + base_ref[q, w]).astype(F32) == col).astype(BF16)
        part = _dot(onehot, ye_refs[q][...])
        tot = part if tot is None else tot + part

    @pl.when(first_ref[w] == 1)
    def _():
        acc_ref[...] = tot

    @pl.when(first_ref[w] == 0)
    def _():
        acc_ref[...] += tot

    @pl.when(last_ref[w] == 1)
    def _():
        o_ref[...] = _post_norm(alpha, x_ref[...], gate_ref[...], acc_ref[...], g_ref[...], b_ref[...])


def moe_combine_postnorm(ye, posm_t, plan, x, gate, ln_g, ln_b, alpha, seq_len):
    n, d = x.shape
    n_exp = posm_t.shape[1]
    grp = MOE_GROUP
    n_steps = plan[0].shape[0]
    assert seq_len % TOK_TILE == 0
    tile = lambda w, t, *_: (t[w], 0)
    if gate.shape[0] == 1:
        gate_map = lambda w, t, *_: (0, 0, 0)
    else:
        gate_map = lambda w, t, *_: (t[w] * TOK_TILE // seq_len, 0, 0)
    vec = pl.BlockSpec((1, d), lambda w, *_: (0, 0))
    ye_spec = lambda q: pl.BlockSpec((SLOT_BLK, d), lambda w, t, f, l, e, b, s: (b[q, w], 0))
    grid_spec = pltpu.PrefetchScalarGridSpec(
        num_scalar_prefetch=6,
        grid=(n_steps,),
        in_specs=([pl.BlockSpec((TOK_TILE, n_exp), tile)] + [ye_spec(q) for q in range(grp)]
                  + [pl.BlockSpec((TOK_TILE, d), tile), pl.BlockSpec((None, 1, d), gate_map), vec, vec]),
        out_specs=pl.BlockSpec((TOK_TILE, d), tile),
        scratch_shapes=[pltpu.VMEM((TOK_TILE, d), F32)],
    )
    return pl.pallas_call(
        functools.partial(_combine_kernel, alpha, grp),
        grid_spec=grid_spec,
        out_shape=jax.ShapeDtypeStruct((n, d), F32),
        compiler_params=_cparams("arbitrary"),
        name="moe_combine_postnorm",
    )(*plan, posm_t, *([ye] * grp), x, gate, ln_g, ln_b)


def ec_moe_postnorm(x, mods_shift, mods_scale, gate, router_w, w1, w3, w2, layer, ln_g, ln_b, alpha):
    bsz, t, d = x.shape
    n = bsz * t
    n_exp = router_w.shape[1]
    cap = max(1, CAP_FACTOR * n // n_exp)
    assert cap % SLOT_BLK == 0 and n % TOK_TILE == 0
    h, aff = ln_modulate_router(x, mods_shift, mods_scale, router_w)
    h = h.reshape(n, d)
    aff = aff.reshape(n, LANE)
    mask = expert_choice_mask(aff[:, :n_exp].T, cap)
    posm, dispatch, combine = routing_plan(mask, cap)
    aff_rows = moe_route(aff, posm, dispatch, cap)
    slot_token = jnp.clip(aff_rows[:, n_exp].astype(jnp.int32), 0, n - 1)
    xe = moe_gather(h, slot_token)
    ye = expert_ffn(xe, aff_rows, w1, w3, w2, layer, cap)
    out = moe_combine_postnorm(ye, posm.T, combine, x.reshape(n, d), gate, ln_g, ln_b, alpha, t)
    return out.reshape(bsz, t, d)


def prep_rwkv_params(rw, aw):
    a_cols = rw["mu"].shape[0]
    ap = round_up(a_cols, LANE)
    gp = ap - 3 * aw - 2 * LANE
    assert rw["w2"].shape[:2] == (2, LANE // 2) and rw["a2"].shape[:2] == (2, LANE // 2)
    assert gp >= rw["g2"].shape[0] and aw % LANE == 0
    row = lambda a: a.reshape(1, aw)
    return dict(
        mu=jnp.pad(rw["mu"], (0, ap - a_cols)).reshape(1, ap),
        w0=rw["w0"], w2=rw["w2"].reshape(LANE, aw), a0=rw["a0"], a2=rw["a2"].reshape(LANE, aw),
        g2=jnp.pad(rw["g2"], ((0, gp - rw["g2"].shape[0]), (0, 0))),
        k_k=row(rw["k_k"]), k_a=row(rw["k_a"]), r_k=row(rw["r_k"]),
        ln_w=row(rw["ln_w"]), ln_b=row(rw["ln_b"]), ap=ap, aw=aw)


def states_to_pairs(s):
    bsz, h = s.shape[:2]
    st = jnp.swapaxes(s, -1, -2).reshape(bsz, h // 2, 2, A_HEAD, A_HEAD)
    z = jnp.zeros_like(st[:, :, 0])
    top = jnp.concatenate([st[:, :, 0], z], axis=-1)
    bot = jnp.concatenate([z, st[:, :, 1]], axis=-1)
    return jnp.concatenate([top, bot], axis=-2)


def pairs_to_states(sp):
    bsz = sp.shape[0]
    st = jnp.stack([sp[:, :, :A_HEAD, :A_HEAD], sp[:, :, A_HEAD:, A_HEAD:]], axis=2)
    return jnp.swapaxes(st.reshape(bsz, -1, A_HEAD, A_HEAD), -1, -2)


def rwkv_branch(u, bsz, t, rwp, s0f, s0b):
    aw, ap = rwp["aw"], rwp["ap"]
    n = bsz * t
    pack_f, pack_b, bonus, g = rwkv_prep(u, t, rwp, aw, ap)
    sh = lambda a: a.reshape(bsz, t, SCAN_FIELDS * aw)
    yf, sf = rwkv_scan(sh(pack_f), states_to_pairs(s0f), False)
    yb, sb = rwkv_scan(sh(pack_b), states_to_pairs(s0b), True)
    ya = rwkv_post(yf.reshape(n, aw), yb.reshape(n, aw), bonus, g, rwp["ln_w"], rwp["ln_b"])
    return ya, pairs_to_states(sf), pairs_to_states(sb)


def _mod_rows(m):
    return [m[:, i][:, None, :] for i in range(6)]


def kernel(x_prompt, x_sample, cache_win_k, cache_win_v, cache_nb_k, cache_nb_v, state_rwkv_fwd, state_rwkv_bwd, c, c_ctx, ada_w, ada_b, w_in, rwkv_mu, rwkv_w0, rwkv_w2, rwkv_a0, rwkv_a2, rwkv_g2, rwkv_kk, rwkv_ka, rwkv_rk, rwkv_lnx_w, rwkv_lnx_b, win_sink, nb_rpb, w_br_a, w_br_b, w_br_c, w_out, ln1_g, ln1_b, ln2_g, ln2_b, router_w, exp_w1, exp_w3, exp_w2):
    depth, d, in_w = w_in.shape
    bsz, seq, _ = x_prompt.shape
    dbsz, dseq, _ = x_sample.shape
    aw = rwkv_w0.shape[-1]
    a_heads = aw // A_HEAD
    a_cols = rwkv_mu.shape[-1]
    ap = round_up(a_cols, LANE)
    n_b = win_sink.shape[-1]
    n_kv = cache_win_k.shape[3]
    group = n_b // n_kv
    n_c = nb_rpb.shape[1]
    bw, cw = n_b * HEAD_DIM, n_c * HEAD_DIM
    alpha = float((2 * depth) ** 0.25)
    assert in_w == a_cols + bw + 2 * n_kv * HEAD_DIM + 3 * cw + 3 * d

    qb0 = ap // LANE
    kb0 = qb0 + n_b
    vb0 = kb0 + n_kv
    qn0 = vb0 + n_kv
    kn0 = qn0 + n_c
    vn0 = kn0 + n_c
    gate_col0 = (vn0 + n_c) * LANE

    rows = 1 + dbsz
    cvecs = jnp.pad(jnp.concatenate([c_ctx[None], c], axis=0), ((0, round_up(rows, 16) - rows), (0, 0)))
    mods = ada_mods(cvecs, ada_w, ada_b).reshape(depth, -1, 6, d)

    xp, xs = x_prompt, x_sample
    np_tok, ns_tok = bsz * seq, dbsz * dseq
    win_k, win_v, nb_k, nb_v, st_f, st_b = [], [], [], [], [], []
    zero_state = jnp.zeros((bsz, a_heads, A_HEAD, A_HEAD), F32)
    w_cat = jnp.concatenate(
        [w_in[:, :, :a_cols].astype(BF16), jnp.zeros((depth, d, ap - a_cols), BF16),
         w_in[:, :, a_cols:].astype(BF16)], axis=2)
    w1, w3, w2 = exp_w1.astype(BF16), exp_w3.astype(BF16), exp_w2.astype(BF16)
    for l in range(depth):
        rwp = prep_rwkv_params(
            {"mu": rwkv_mu[l], "w0": rwkv_w0[l], "w2": rwkv_w2[l], "a0": rwkv_a0[l], "a2": rwkv_a2[l],
             "g2": rwkv_g2[l], "k_k": rwkv_kk[l], "k_a": rwkv_ka[l], "r_k": rwkv_rk[l],
             "ln_w": rwkv_lnx_w[l], "ln_b": rwkv_lnx_b[l]}, aw)
        wa, wb, wc = w_br_a[l].astype(BF16), w_br_b[l].astype(BF16), w_br_c[l].astype(BF16)
        wo = w_out[l].astype(BF16)
        g1, b1 = ln1_g[l].reshape(1, d), ln1_b[l].reshape(1, d)
        g2, b2 = ln2_g[l].reshape(1, d), ln2_b[l].reshape(1, d)

        sh1, sc1, gt1, sh2, sc2, gt2 = _mod_rows(mods[l, 0:1])
        h = ln_modulate(xp, sh1, sc1).reshape(np_tok, d)
        u = matmul(h, w_cat, l, name="in_proj")
        ya, s_f, s_b = rwkv_branch(u, bsz, seq, rwp, zero_state, zero_state)
        yb = ctx_attention(u, bsz, seq, qb0, kb0, vb0, n_b, group, win_sink[l])
        yc = ctx_attention(u, bsz, seq, qn0, kn0, vn0, n_c, 1, None)
        m = merge_branches(ya, yb, yc, wa, wb, wc, u, gate_col0)
        x1 = outproj_postnorm(m, wo, xp.reshape(np_tok, d), gt1, g1, b1, alpha, seq)
        xp = ec_moe_postnorm(x1.reshape(bsz, seq, d), sh2, sc2, gt2, router_w[l], w1, w3, w2, l, g2, b2, alpha)
        cols = lambda c0, nh: u[:, c0 * LANE:(c0 + nh) * LANE].reshape(bsz, seq, nh, HEAD_DIM)
        win_k.append(cols(kb0, n_kv))
        win_v.append(cols(vb0, n_kv))
        nb_k.append(cols(kn0, n_c))
        nb_v.append(cols(vn0, n_c))
        st_f.append(s_f)
        st_b.append(s_b)

        sh1, sc1, gt1, sh2, sc2, gt2 = _mod_rows(mods[l, 1:1 + dbsz])
        h = ln_modulate(xs, sh1, sc1).reshape(ns_tok, d)
        u = matmul(h, w_cat, l, name="in_proj")
        ya, _, _ = rwkv_branch(u, dbsz, dseq, rwp, state_rwkv_fwd[:, l], state_rwkv_bwd[:, l])
        qk_rot = rope(u, dbsz, dseq, qb0, n_b + n_kv)
        heads_first = lambda a: jnp.swapaxes(a[:, l], 1, 2)
        yb = window_attention(qk_rot, u, vb0, dbsz, dseq, n_kv, group,
                              heads_first(cache_win_k), heads_first(cache_win_v), win_sink[l])
        yc = neighbourhood_attention(u, (qn0, kn0, vn0), dbsz, dseq, n_c,
                                     heads_first(cache_nb_k), heads_first(cache_nb_v), nb_rpb[l])
        m = merge_branches(ya, yb, yc, wa, wb, wc, u, gate_col0)
        x1 = outproj_postnorm(m, wo, xs.reshape(ns_tok, d), gt1, g1, b1, alpha, dseq)
        xs = ec_moe_postnorm(x1.reshape(dbsz, dseq, d), sh2, sc2, gt2, router_w[l], w1, w3, w2, l, g2, b2, alpha)

    stack = lambda xs_: jnp.stack(xs_, axis=1)
    return (xp, xs, stack(win_k), stack(win_v), stack(nb_k), stack(nb_v), stack(st_f), stack(st_b))
```

```python
import functools

import numpy as np
import jax
import jax.numpy as jnp
from jax import lax
from jax.experimental import pallas as pl
from jax.experimental.pallas import tpu as pltpu

F32 = jnp.float32
BF16 = jnp.bfloat16

HEAD_DIM = 128
A_HEAD = 64
GRID_W = 64
WIN = 128
QBLK = 128
NB_ROWS = 8
NB_COLS = 16
CAP_FACTOR = 2
ROPE_THETA = 10000.0
LN_EPS = 1e-6
GN_EPS = 64e-5
NEG = -1e30

LANE = 128
SUBLANE = 8
VMEM_LIMIT_BYTES = 56 * 1024 * 1024

SCAN_CHUNK = 64
SLOT_BLK = 128
TOK_TILE = 256
ROUTE_TILE = 1024


def _cparams(*sem):
    return pltpu.CompilerParams(dimension_semantics=sem, vmem_limit_bytes=VMEM_LIMIT_BYTES)


def _dot(a, b):
    return jnp.dot(a, b, preferred_element_type=F32)


def _dot_nt(a, b):
    return lax.dot_general(a, b, (((1,), (1,)), ((), ())), preferred_element_type=F32)


def _split2(x):
    hi = x.astype(BF16)
    lo = (x - hi.astype(F32)).astype(BF16)
    return hi, lo


def _split3(x):
    hi = x.astype(BF16)
    r1 = x - hi.astype(F32)
    mid = r1.astype(BF16)
    lo = (r1 - mid.astype(F32)).astype(BF16)
    return hi, mid, lo


def _dot3(a, b):
    ah, al = _split2(a)
    bh, bl = _split2(b)
    return _dot(ah, bh) + (_dot(ah, bl) + _dot(al, bh))


def _hi_lo(w):
    hi, lo = _split2(w)
    return jnp.stack([hi, lo])


def _dot3_split_rhs(a, b_ref):
    ah, al = _split2(a)
    bh = b_ref[0]
    return _dot(ah, bh) + (_dot(ah, b_ref[1]) + _dot(al, bh))


def _dot3_nt(a, b):
    ah, al = _split2(a)
    bh, bl = _split2(b)
    return _dot_nt(ah, bh) + (_dot_nt(ah, bl) + _dot_nt(al, bh))


def _dot_exact_lhs(a_bf16, b):
    bh, bm, bl = _split3(b)
    return _dot(a_bf16, bh) + (_dot(a_bf16, bm) + _dot(a_bf16, bl))


def _sigmoid(x):
    return 1.0 / (1.0 + jnp.exp(-x))


def _iota(shape, dim):
    return lax.broadcasted_iota(jnp.int32, shape, dim)


def _ada_kernel(c_ref, w_ref, b_ref, o_ref):
    c = c_ref[...]
    a = c * _sigmoid(c)
    o_ref[...] = _dot3(a, w_ref[...]) + b_ref[...]


def ada_mods(cvecs, ada_w, ada_b, tn=512):
    depth, d, n6 = ada_w.shape
    rows = cvecs.shape[0]
    return pl.pallas_call(
        _ada_kernel,
        grid=(depth, n6 // tn),
        in_specs=[
            pl.BlockSpec((rows, d), lambda l, j: (0, 0)),
            pl.BlockSpec((None, d, tn), lambda l, j: (l, 0, j)),
            pl.BlockSpec((None, 1, tn), lambda l, j: (l, 0, j)),
        ],
        out_specs=pl.BlockSpec((None, rows, tn), lambda l, j: (l, 0, j)),
        out_shape=jax.ShapeDtypeStruct((depth, rows, n6), F32),
        compiler_params=_cparams("parallel", "parallel"),
        name="ada_mods",
    )(cvecs, ada_w, ada_b.reshape(depth, 1, n6))


def _ln_rows(x):
    mu = jnp.mean(x, axis=-1, keepdims=True)
    xc = x - mu
    var = jnp.mean(xc * xc, axis=-1, keepdims=True)
    return xc * lax.rsqrt(var + LN_EPS)


def _lnmod_kernel(x_ref, sh_ref, sc_ref, o_ref):
    y = _ln_rows(x_ref[...])
    o_ref[...] = (y * (1.0 + sc_ref[...]) + sh_ref[...]).astype(o_ref.dtype)


def _mod_spec(mod, d):
    if mod.shape[0] == 1:
        return pl.BlockSpec((None, 1, d), lambda b, i: (0, 0, 0))
    return pl.BlockSpec((None, 1, d), lambda b, i: (b, 0, 0))


def ln_modulate(x, shift, scale, tt=256):
    bsz, t, d = x.shape
    return pl.pallas_call(
        _lnmod_kernel,
        grid=(bsz, t // tt),
        in_specs=[
            pl.BlockSpec((None, tt, d), lambda b, i: (b, i, 0)),
            _mod_spec(shift, d),
            _mod_spec(scale, d),
        ],
        out_specs=pl.BlockSpec((None, tt, d), lambda b, i: (b, i, 0)),
        out_shape=jax.ShapeDtypeStruct((bsz, t, d), BF16),
        compiler_params=_cparams("parallel", "parallel"),
        name="ln_modulate",
    )(x, shift, scale)


def _lnmod_router_kernel(n_exp, seq_len, x_ref, sh_ref, sc_ref, rw_ref, h_ref, aff_ref):
    y = _ln_rows(x_ref[...])
    h = y * (1.0 + sc_ref[...]) + sh_ref[...]
    h_ref[...] = h
    logits = _dot3(h, rw_ref[...])
    lane = _iota(logits.shape, 1)
    logits = jnp.where(lane < n_exp, logits, NEG)
    m = jnp.max(logits, axis=-1, keepdims=True)
    e = jnp.exp(logits - m)
    aff = e / jnp.sum(e, axis=-1, keepdims=True)
    tt = aff.shape[0]
    tok = pl.program_id(0) * seq_len + pl.program_id(1) * tt + _iota(aff.shape, 0)
    aff_ref[...] = jnp.where(lane == n_exp, tok.astype(F32), aff)


def ln_modulate_router(x, shift, scale, router_w, tt=256):
    bsz, t, d = x.shape
    n_exp = router_w.shape[1]
    assert n_exp < LANE and bsz * t < (1 << 24)
    rw = jnp.pad(router_w, ((0, 0), (0, LANE - n_exp)))
    return pl.pallas_call(
        functools.partial(_lnmod_router_kernel, n_exp, t),
        grid=(bsz, t // tt),
        in_specs=[
            pl.BlockSpec((None, tt, d), lambda b, i: (b, i, 0)),
            _mod_spec(shift, d),
            _mod_spec(scale, d),
            pl.BlockSpec((d, LANE), lambda b, i: (0, 0)),
        ],
        out_specs=[
            pl.BlockSpec((None, tt, d), lambda b, i: (b, i, 0)),
            pl.BlockSpec((None, tt, LANE), lambda b, i: (b, i, 0)),
        ],
        out_shape=[
            jax.ShapeDtypeStruct((bsz, t, d), F32),
            jax.ShapeDtypeStruct((bsz, t, LANE), F32),
        ],
        compiler_params=_cparams("parallel", "parallel"),
        name="ln_modulate_router",
    )(x, shift, scale, rw)


def _mm_kernel(a_ref, b_ref, o_ref):
    o_ref[...] = _dot(a_ref[...], b_ref[...]).astype(o_ref.dtype)


def _pick(n, pref):
    for c in pref:
        if n % c == 0:
            return c
    return n


def matmul(a, b, layer, out_dtype=F32, tm=None, tn=None, name="matmul"):
    m, k = a.shape
    n = b.shape[2]
    tm = tm or _pick(m, (1024, 512, 256, 128))
    tn = tn or _pick(n, (768, 512, 256, 128))
    return pl.pallas_call(
        _mm_kernel,
        grid=(m // tm, n // tn),
        in_specs=[
            pl.BlockSpec((tm, k), lambda i, j: (i, 0)),
            pl.BlockSpec((None, k, tn), lambda i, j: (layer, 0, j)),
        ],
        out_specs=pl.BlockSpec((tm, tn), lambda i, j: (i, j)),
        out_shape=jax.ShapeDtypeStruct((m, n), out_dtype),
        compiler_params=_cparams("parallel", "parallel"),
        name=name,
    )(a, b)


HEAD_SHIFT = 6
assert (1 << HEAD_SHIFT) == A_HEAD and 2 * A_HEAD == LANE and SCAN_CHUNK == A_HEAD


def _head_block_ones():
    i = _iota((LANE, LANE), 0) >> HEAD_SHIFT
    j = _iota((LANE, LANE), 1) >> HEAD_SHIFT
    return (i == j).astype(BF16)


def _seg_sum(x):
    bd = _head_block_ones()
    outs = []
    for c in range(x.shape[1] // LANE):
        hi, mid, lo = _split3(x[:, c * LANE:(c + 1) * LANE])
        outs.append(_dot(hi, bd) + (_dot(mid, bd) + _dot(lo, bd)))
    return jnp.concatenate(outs, axis=1)


SCAN_FIELDS = 6


def _rwkv_prep_kernel(aw, gp, gw, tiles_per_seq,
                      x_ref, xp_ref, xn_ref, mu_ref, w0_ref, w2_ref, a0_ref, a2_ref, g2_ref,
                      kkp_ref, kap_ref, rkp_ref,
                      packf_o, packb_o, bonus_o, g_o):
    i = pl.program_id(0)
    x = x_ref[...]
    tt = x.shape[0]
    row = _iota(x.shape, 0)
    pos = i % tiles_per_seq
    prev_row = jnp.where(pos == 0, 0.0, xp_ref[SUBLANE - 1:SUBLANE, :])
    next_row = jnp.where(pos == tiles_per_seq - 1, 0.0, xn_ref[0:1, :])
    prev = jnp.where(row == 0, prev_row, pltpu.roll(x, 1, 0))
    nxt = jnp.where(row == tt - 1, next_row, pltpu.roll(x, tt - 1, 0))
    xs = x + mu_ref[...] * (0.5 * (prev + nxt) - x)

    r = xs[:, 0:aw]
    k = xs[:, aw:2 * aw]
    v = xs[:, 2 * aw:3 * aw]
    o = 3 * aw
    wlo = jnp.tanh(xs[:, o:o + LANE])
    alo = xs[:, o + LANE:o + 2 * LANE]
    glo = _sigmoid(xs[:, o + 2 * LANE:o + 2 * LANE + gp])
    lane = _iota((1, LANE), 1)
    exp_mhalf = float(np.exp(-0.5))

    g_o[...] = _dot3_split_rhs(glo, g2_ref)

    kk = k * kkp_ref[...]
    nrm = jnp.sqrt(_seg_sum(kk * kk))
    kk = kk / jnp.maximum(nrm, 1e-12)
    bonus_o[...] = _seg_sum(r * k * rkp_ref[...]) * v

    for d, pack_o in enumerate((packf_o, packb_o)):
        sel = ((lane >> HEAD_SHIFT) == d).astype(F32)
        wl = w0_ref[d:d + 1, :] + _dot3_split_rhs(wlo * sel, w2_ref)
        lw = -_sigmoid(wl) * exp_mhalf
        a = _sigmoid(a0_ref[d:d + 1, :] + _dot3_split_rhs(alo * sel, a2_ref))
        fields = (lw, kk, kk * a, k * (1.0 + (a - 1.0) * kap_ref[...]), r, v)
        for g in range(aw // gw):
            for f, arr in enumerate(fields):
                c0 = (g * SCAN_FIELDS + f) * gw
                pack_o[:, c0:c0 + gw] = arr[:, g * gw:(g + 1) * gw]


def scan_group_width(aw, max_pairs=6):
    n_pairs = aw // LANE
    return LANE * max(p for p in range(1, max_pairs + 1) if n_pairs % p == 0)


def rwkv_prep(u, seq_len, rw, aw, ap, tt=128):
    n = u.shape[0]
    gw = scan_group_width(aw)
    gp = ap - 3 * aw - 2 * LANE
    n8 = n // SUBLANE
    tpb = tt // SUBLANE
    full = lambda shape: pl.BlockSpec(shape, lambda i: (0,) * len(shape))
    tok = pl.BlockSpec((tt, aw), lambda i: (i, 0))
    pack = pl.BlockSpec((tt, SCAN_FIELDS * aw), lambda i: (i, 0))
    pack_shape = jax.ShapeDtypeStruct((n, SCAN_FIELDS * aw), F32)
    outs = pl.pallas_call(
        functools.partial(_rwkv_prep_kernel, aw, gp, gw, seq_len // tt),
        grid=(n // tt,),
        in_specs=[
            pl.BlockSpec((tt, ap), lambda i: (i, 0)),
            pl.BlockSpec((SUBLANE, ap), lambda i: (jnp.maximum(i * tpb - 1, 0), 0)),
            pl.BlockSpec((SUBLANE, ap), lambda i: (jnp.minimum((i + 1) * tpb, n8 - 1), 0)),
            full((1, ap)), full((2, aw)), full((2, LANE, aw)), full((2, aw)), full((2, LANE, aw)),
            full((2, gp, aw)), full((1, aw)), full((1, aw)), full((1, aw)),
        ],
        out_specs=[pack, pack, tok, tok],
        out_shape=[pack_shape, pack_shape] + [jax.ShapeDtypeStruct((n, aw), F32)] * 2,
        compiler_params=_cparams("parallel"),
        name="rwkv_prep",
    )(u, u, u, rw["mu"], rw["w0"], rw["w2"], rw["a0"], rw["a2"], rw["g2"],
      rw["k_k"], rw["k_a"], rw["r_k"])
    return outs


def _scan_chunks(rev, toks, sts, consts):
    tri, strict, incl, blk, eye, head0, t2, s2 = consts
    c = SCAN_CHUNK
    ident = (t2 == s2).astype(F32)
    zero = jnp.zeros((), BF16)
    cat = lambda *xs: jnp.concatenate(xs, axis=0)

    def each(f, *lists):
        return [f(*args) for args in zip(*lists)]

    def bd(x):
        xb = x.astype(BF16)
        return jnp.where(blk, cat(xb, xb), zero)

    lw, kk, bb, kd, r, v = (list(t) for t in zip(*toks))
    big_l = each(lambda a: _dot_exact_lhs(tri, a), lw)
    l_tot = each(lambda a: a[0:1, :] if rev else a[c - 1:c, :], big_l)
    lhs = each(lambda k_, r_, l_, w_: cat(-k_ * jnp.exp(l_ - w_), r_ * jnp.exp(l_)).astype(BF16),
               kk, r, big_l, lw)
    gi = each(lambda l_: jnp.exp(-l_), big_l)
    bt = each(lambda b_, g_: (b_ * g_).astype(BF16), bb, gi)
    kt = each(lambda k_, g_: (k_ * g_).astype(BF16), kd, gi)
    rhs = each(lambda b_, k_: cat(jnp.where(head0, b_, zero), jnp.where(head0, zero, b_),
                                  jnp.where(head0, k_, zero), jnp.where(head0, zero, k_)), bt, kt)
    p = each(_dot_nt, lhs, rhs)
    nab = each(lambda p_: jnp.where(strict, p_[0:c, 0:LANE], 0.0).astype(BF16), p)
    nrb = each(lambda p_: jnp.where(incl, p_[c:2 * c, 0:LANE], 0.0).astype(BF16), p)
    nkk = each(lambda p_: cat(jnp.where(strict, p_[0:c, LANE:2 * LANE], 0.0),
                              jnp.where(incl, p_[c:2 * c, LANE:2 * LANE], 0.0)).astype(BF16), p)
    z = each(lambda l_, s_: _dot(l_, s_.astype(BF16)), lhs, sts)
    w = each(lambda n_, v_: _dot(n_, bd(v_)), nkk, v)
    x = each(lambda z_, w_: z_[0:c] + w_[0:c], z, w)
    blk8 = (t2 >> 3) == (s2 >> 3)
    n0 = each(lambda n_: jnp.where(blk8, n_, zero), nab)
    inv = each(lambda n_: ident + n_.astype(F32), n0)
    pw = each(lambda n_: _dot(n_, bd(n_)), n0)
    inv = each(lambda i_, p_: i_ + _dot(i_.astype(BF16), bd(p_)), inv, pw)
    pw = each(lambda p_: _dot(p_.astype(BF16), bd(p_)), pw)
    inv = each(lambda i_, p_: i_ + _dot(i_.astype(BF16), bd(p_)), inv, pw)
    for lb in range(3, HEAD_SHIFT):
        new = ((t2 >> (lb + 1)) == (s2 >> (lb + 1))) & ((t2 >> lb) != (s2 >> lb))
        half = each(lambda i_, n_: _dot(i_.astype(BF16), bd(jnp.where(new, n_, zero))), inv, nab)
        inv = each(lambda i_, h_: i_ + _dot(h_.astype(BF16), bd(i_)), inv, half)
    x = each(lambda i_, x_: _dot(i_.astype(BF16), bd(x_)), inv, x)
    y = each(lambda z_, w_, n_, x_: z_[c:2 * c] + w_[c:2 * c] + _dot(n_, bd(x_)), z, w, nrb, x)
    gr = each(lambda t_, l_: jnp.exp(t_ - l_), l_tot, big_l)
    upd = each(lambda b_, k_, g_, x_, v_: _dot(cat(b_ * g_, k_ * g_).T.astype(BF16),
                                               cat(x_, v_).astype(BF16)), bb, kd, gr, x, v)
    g_col = each(lambda t_: jnp.sum(jnp.where(eye, jnp.exp(t_), 0.0), axis=1, keepdims=True), l_tot)
    st_new = each(lambda s_, g_, u_: s_ * g_ + jnp.where(blk, u_, 0.0), sts, g_col, upd)
    return y, st_new


def _scan_kernel(rev, npar, n_chunks, tok_ref, s0_ref, y_ref, st_out_ref, st_scr):
    ci = pl.program_id(2)

    @pl.when(ci == 0)
    def _():
        st_scr[...] = s0_ref[...]

    c = SCAN_CHUNK
    ti = _iota((c, c), 0)
    si = _iota((c, c), 1)
    tri = ((si >= ti) if rev else (si <= ti)).astype(BF16)
    t2 = _iota((c, LANE), 0)
    s2 = _iota((c, LANE), 1) & (c - 1)
    strict = (s2 > t2) if rev else (s2 < t2)
    incl = (s2 >= t2) if rev else (s2 <= t2)
    bi = _iota((LANE, LANE), 0)
    bj = _iota((LANE, LANE), 1)
    blk = (bi >> HEAD_SHIFT) == (bj >> HEAD_SHIFT)
    eye = bi == bj
    head0 = _iota((c, LANE), 1) < A_HEAD
    consts = (tri, strict, incl, blk, eye, head0, t2, s2)
    lanes = [slice(p * LANE, (p + 1) * LANE) for p in range(npar)]
    gw = npar * LANE
    toks = [tuple(tok_ref[:, f * gw + p * LANE:f * gw + (p + 1) * LANE] for f in range(SCAN_FIELDS))
            for p in range(npar)]
    ys, sts = _scan_chunks(rev, toks, [st_scr[p] for p in range(npar)], consts)
    for p, sl in enumerate(lanes):
        y_ref[:, sl] = ys[p]
        st_scr[p] = sts[p]

    @pl.when(ci == n_chunks - 1)
    def _():
        st_out_ref[...] = st_scr[...]


def rwkv_scan(packed, s0, rev):
    bsz, t, width = packed.shape
    aw = width // SCAN_FIELDS
    n_pairs = aw // LANE
    npar = scan_group_width(aw) // LANE
    n_chunks = t // SCAN_CHUNK
    if rev:
        tmap = lambda b, g, c: (b, n_chunks - 1 - c, g)
    else:
        tmap = lambda b, g, c: (b, c, g)
    tok = pl.BlockSpec((None, SCAN_CHUNK, npar * LANE), tmap)
    tok_in = pl.BlockSpec((None, SCAN_CHUNK, SCAN_FIELDS * npar * LANE), tmap)
    st = pl.BlockSpec((None, npar, LANE, LANE), lambda b, g, c: (b, g, 0, 0))
    return pl.pallas_call(
        functools.partial(_scan_kernel, rev, npar, n_chunks),
        grid=(bsz, n_pairs // npar, n_chunks),
        in_specs=[tok_in, st],
        out_specs=[tok, st],
        out_shape=[jax.ShapeDtypeStruct((bsz, t, aw), F32),
                   jax.ShapeDtypeStruct(s0.shape, F32)],
        scratch_shapes=[pltpu.VMEM((npar, LANE, LANE), F32)],
        compiler_params=_cparams("parallel", "parallel", "arbitrary"),
        name="rwkv_scan_bwd" if rev else "rwkv_scan_fwd",
    )(packed, s0)


def _rwkv_post_kernel(yf_ref, yb_ref, bonus_ref, g_ref, lnw_ref, lnb_ref, o_ref):
    y = yf_ref[...] + yb_ref[...]
    inv = 1.0 / A_HEAD
    mu = _seg_sum(y) * inv
    yc = y - mu
    var = _seg_sum(yc * yc) * inv
    yn = yc * lax.rsqrt(var + GN_EPS) * lnw_ref[...] + lnb_ref[...]
    o_ref[...] = ((yn + bonus_ref[...]) * g_ref[...]).astype(o_ref.dtype)


def rwkv_post(yf, yb, bonus, g, ln_w, ln_b, tt=256):
    n, aw = yf.shape
    tok = pl.BlockSpec((tt, aw), lambda i: (i, 0))
    par = pl.BlockSpec((1, aw), lambda i: (0, 0))
    return pl.pallas_call(
        _rwkv_post_kernel,
        grid=(n // tt,),
        in_specs=[tok, tok, tok, tok, par, par],
        out_specs=tok,
        out_shape=jax.ShapeDtypeStruct((n, aw), BF16),
        compiler_params=_cparams("parallel"),
        name="rwkv_post",
    )(yf, yb, bonus, g, ln_w, ln_b)


def round_up(x, m):
    return (x + m - 1) // m * m


ATT_SCALE = HEAD_DIM ** -0.5


def _ctx_attn_kernel(q_ref, k_ref, v_ref, sink_ref, o_ref):
    s = _dot3_nt(q_ref[...] * ATT_SCALE, k_ref[...])
    sk = sink_ref[0:1, 0:1]
    m = jnp.maximum(jnp.max(s, axis=-1, keepdims=True), sk)
    e = jnp.exp(s - m)
    den = jnp.sum(e, axis=-1, keepdims=True) + jnp.exp(sk - m)
    o = _dot(e.astype(BF16), v_ref[...].astype(BF16))
    o_ref[...] = (o / den).astype(o_ref.dtype)


def ctx_attention(u, bsz, t, q0, k0, v0, n_heads, group, sink):
    sink = jnp.full((n_heads,), NEG, F32) if sink is None else sink.astype(F32)
    sink = jnp.broadcast_to(sink[:, None, None], (n_heads, 1, LANE))
    return pl.pallas_call(
        _ctx_attn_kernel,
        grid=(bsz, n_heads),
        in_specs=[
            pl.BlockSpec((t, HEAD_DIM), lambda b, h: (b, q0 + h)),
            pl.BlockSpec((t, HEAD_DIM), lambda b, h: (b, k0 + h // group)),
            pl.BlockSpec((t, HEAD_DIM), lambda b, h: (b, v0 + h // group)),
            pl.BlockSpec((None, 1, LANE), lambda b, h: (h, 0, 0)),
        ],
        out_specs=pl.BlockSpec((t, HEAD_DIM), lambda b, h: (b, h)),
        out_shape=jax.ShapeDtypeStruct((bsz * t, n_heads * HEAD_DIM), BF16),
        compiler_params=_cparams("parallel", "parallel"),
        name="ctx_attention",
    )(u, u, u, sink)


def rope_tables(n_tok):
    half = HEAD_DIM // 2
    quarter = half // 2
    tok = jnp.arange(n_tok)
    row = (tok // GRID_W).astype(F32)
    col = (tok % GRID_W).astype(F32)
    inv = ROPE_THETA ** (-jnp.arange(quarter, dtype=F32) / quarter)
    ang_r = row[:, None] * inv[None]
    ang_c = col[:, None] * inv[None]
    cos = jnp.concatenate([jnp.cos(ang_r)] * 2 + [jnp.cos(ang_c)] * 2, axis=1)
    sr, sc = jnp.sin(ang_r), jnp.sin(ang_c)
    z = jnp.zeros_like(sr)
    sin_a = jnp.concatenate([-sr, z, -sc, z], axis=1)
    sin_b = jnp.concatenate([z, sr, z, sc], axis=1)
    return cos, sin_a, sin_b


def _rope_kernel(n_heads, x_ref, cos_ref, sa_ref, sb_ref, o_ref):
    q = HEAD_DIM // 4
    cos, sa, sb = cos_ref[...], sa_ref[...], sb_ref[...]
    for h in range(n_heads):
        sl = slice(h * HEAD_DIM, (h + 1) * HEAD_DIM)
        x = x_ref[:, sl]
        o_ref[:, sl] = x * cos + pltpu.roll(x, HEAD_DIM - q, 1) * sa + pltpu.roll(x, q, 1) * sb


def rope(u, bsz, n_tok, col0, n_heads, tt=256):
    cos, sin_a, sin_b = rope_tables(n_tok)
    nt = n_tok // tt
    hb = max(k for k in range(1, n_heads + 1) if n_heads % k == 0 and col0 % k == 0)
    width = hb * HEAD_DIM
    cb = col0 // hb
    tab = pl.BlockSpec((tt, HEAD_DIM), lambda b, i, j: (i, 0))
    return pl.pallas_call(
        functools.partial(_rope_kernel, hb),
        grid=(bsz, nt, n_heads // hb),
        in_specs=[pl.BlockSpec((tt, width), lambda b, i, j: (b * nt + i, cb + j)), tab, tab, tab],
        out_specs=pl.BlockSpec((tt, width), lambda b, i, j: (b * nt + i, j)),
        out_shape=jax.ShapeDtypeStruct((bsz * n_tok, n_heads * HEAD_DIM), F32),
        compiler_params=_cparams("parallel", "parallel", "parallel"),
        name="rope",
    )(u, cos, sin_a, sin_b)


def _win_attn_kernel(group, n_tok, sink_ref, q_ref, kp_ref, kc_ref, kn_ref, vp_ref, vc_ref, vn_ref,
                     ck_ref, cv_ref, o_ref):
    kv = pl.program_id(1)
    i = pl.program_id(2)
    rows = group * QBLK
    q = jnp.concatenate([q_ref[:, g * HEAD_DIM:(g + 1) * HEAD_DIM] for g in range(group)], axis=0)
    q = q * ATT_SCALE
    k_loc = jnp.concatenate([kp_ref[...], kc_ref[...], kn_ref[...]], axis=0)
    v_loc = jnp.concatenate([vp_ref[...], vc_ref[...], vn_ref[...]], axis=0).astype(BF16)
    s_loc = _dot3_nt(q, k_loc)
    qi = _iota(s_loc.shape, 0) & (QBLK - 1)
    kj = _iota(s_loc.shape, 1)
    pos = i * QBLK - WIN + kj
    ok = (kj - qi >= 0) & (kj - qi <= 2 * WIN) & (pos >= 0) & (pos < n_tok)
    s_loc = jnp.where(ok, s_loc, NEG)
    s_ctx = _dot3_nt(q, ck_ref[...])
    rg = _iota((rows, 1), 0) >> int(np.log2(QBLK))
    sk = jnp.zeros((rows, 1), F32)
    for g in range(group):
        sk = jnp.where(rg == g, sink_ref[kv * group + g], sk)
    m = jnp.maximum(jnp.maximum(jnp.max(s_loc, axis=-1, keepdims=True),
                                jnp.max(s_ctx, axis=-1, keepdims=True)), sk)
    e_loc = jnp.exp(s_loc - m)
    e_ctx = jnp.exp(s_ctx - m)
    den = (jnp.sum(e_loc, axis=-1, keepdims=True) + jnp.sum(e_ctx, axis=-1, keepdims=True)
           + jnp.exp(sk - m))
    o = (_dot(e_loc.astype(BF16), v_loc) + _dot(e_ctx.astype(BF16), cv_ref[...].astype(BF16))) / den
    for g in range(group):
        o_ref[:, g * HEAD_DIM:(g + 1) * HEAD_DIM] = o[g * QBLK:(g + 1) * QBLK].astype(o_ref.dtype)


def window_attention(qk_rot, u, vb0, bsz, n_tok, n_kv, group, ck, cv, sink):
    assert WIN == QBLK
    nb = n_tok // QBLK
    nq = n_kv * group
    past = ck.shape[2]
    prv = lambda i: jnp.maximum(i - 1, 0)
    nxt = lambda i: jnp.minimum(i + 1, nb - 1)
    blk = lambda f, c0: pl.BlockSpec((QBLK, HEAD_DIM), lambda b, kv, i, s: (b * nb + f(i), c0 + kv))
    same = lambda i: i
    cache = pl.BlockSpec((None, None, past, HEAD_DIM), lambda b, kv, i, s: (b, kv, 0, 0))
    grid_spec = pltpu.PrefetchScalarGridSpec(
        num_scalar_prefetch=1,
        grid=(bsz, n_kv, nb),
        in_specs=[
            pl.BlockSpec((QBLK, group * HEAD_DIM), lambda b, kv, i, s: (b * nb + i, kv)),
            blk(prv, nq), blk(same, nq), blk(nxt, nq),
            blk(prv, vb0), blk(same, vb0), blk(nxt, vb0),
            cache, cache,
        ],
        out_specs=pl.BlockSpec((QBLK, group * HEAD_DIM), lambda b, kv, i, s: (b * nb + i, kv)),
    )
    return pl.pallas_call(
        functools.partial(_win_attn_kernel, group, n_tok),
        grid_spec=grid_spec,
        out_shape=jax.ShapeDtypeStruct((bsz * n_tok, nq * HEAD_DIM), BF16),
        compiler_params=_cparams("parallel", "parallel", "parallel"),
        name="window_attention",
    )(sink.astype(F32), qk_rot, qk_rot, qk_rot, qk_rot, u, u, u, ck, cv)


NB_QROWS = 4
NB_QTOK = NB_QROWS * GRID_W
NB_KTOK = 3 * NB_QTOK


def nb_tables(rpb, rows):
    kh = min(NB_ROWS, rows)
    ql = np.arange(NB_QTOK)
    kl = np.arange(NB_KTOK)
    r_rel, c = ql // GRID_W, ql % GRID_W
    kr_rel, kc = kl // GRID_W - NB_QROWS, kl % GRID_W
    n_heads = rpb.shape[0]
    gcol = np.arange(GRID_W)
    col_idx = np.clip(gcol[None, :] - gcol[:, None] + NB_COLS - 1, 0, 2 * NB_COLS - 2)
    t_col = jnp.take(rpb.astype(F32), jnp.asarray(col_idx.reshape(-1)), axis=2)
    t_col = t_col.reshape(n_heads, 2 * NB_ROWS - 1, GRID_W * GRID_W)
    qr = np.arange(NB_QROWS)
    krr = np.arange(3 * NB_QROWS) - NB_QROWS
    row_idx = np.clip(krr[None, :] - qr[:, None] + NB_ROWS - 1, 0, 2 * NB_ROWS - 2)
    bias = jnp.take(t_col, jnp.asarray(row_idx.reshape(-1)), axis=1)
    bias = bias.reshape(n_heads, NB_QROWS, 3 * NB_QROWS, GRID_W, GRID_W)
    bias = jnp.transpose(bias, (0, 1, 3, 2, 4)).reshape(n_heads, NB_QTOK, NB_KTOK)
    win_start = np.clip(c - NB_COLS // 2, 0, GRID_W - NB_COLS)
    col_ok = (kc[None, :] >= win_start[:, None]) & (kc[None, :] < win_start[:, None] + NB_COLS)
    masks = []
    for j in range(rows // NB_QROWS):
        r = j * NB_QROWS + r_rel
        kr = j * NB_QROWS + kr_rel
        row_start = np.clip(r - kh // 2, 0, rows - kh)
        row_ok = (kr[None, :] >= row_start[:, None]) & (kr[None, :] < row_start[:, None] + kh)
        masks.append(row_ok & col_ok)
    return bias, jnp.asarray(np.stack(masks).astype(np.float32))


def _nb_attn_kernel(q_ref, kp_ref, kc_ref, kn_ref, vp_ref, vc_ref, vn_ref, ck_ref, cv_ref,
                    bias_ref, mask_ref, o_ref):
    q = q_ref[...] * ATT_SCALE
    k_loc = jnp.concatenate([kp_ref[...], kc_ref[...], kn_ref[...]], axis=0)
    v_loc = jnp.concatenate([vp_ref[...], vc_ref[...], vn_ref[...]], axis=0).astype(BF16)
    s_loc = jnp.where(mask_ref[...] > 0.0, _dot3_nt(q, k_loc) + bias_ref[...], NEG)
    s_ctx = _dot3_nt(q, ck_ref[...])
    m = jnp.maximum(jnp.max(s_loc, axis=-1, keepdims=True), jnp.max(s_ctx, axis=-1, keepdims=True))
    e_loc = jnp.exp(s_loc - m)
    e_ctx = jnp.exp(s_ctx - m)
    den = jnp.sum(e_loc, axis=-1, keepdims=True) + jnp.sum(e_ctx, axis=-1, keepdims=True)
    o = _dot(e_loc.astype(BF16), v_loc) + _dot(e_ctx.astype(BF16), cv_ref[...].astype(BF16))
    o_ref[...] = (o / den).astype(o_ref.dtype)


def neighbourhood_attention(u, cols, bsz, n_tok, n_heads, ck, cv, rpb):
    q0, k0, v0 = cols
    rows = n_tok // GRID_W
    assert rows % NB_QROWS == 0 and rows >= NB_ROWS
    nj = rows // NB_QROWS
    past = ck.shape[2]
    bias, mask = nb_tables(rpb, rows)
    prv = lambda j: jnp.maximum(j - 1, 0)
    nxt = lambda j: jnp.minimum(j + 1, nj - 1)
    same = lambda j: j
    blk = lambda f, c0: pl.BlockSpec((NB_QTOK, HEAD_DIM), lambda b, h, j: (b * nj + f(j), c0 + h))
    cache = pl.BlockSpec((None, None, past, HEAD_DIM), lambda b, h, j: (b, h, 0, 0))
    return pl.pallas_call(
        _nb_attn_kernel,
        grid=(bsz, n_heads, nj),
        in_specs=[
            blk(same, q0),
            blk(prv, k0), blk(same, k0), blk(nxt, k0),
            blk(prv, v0), blk(same, v0), blk(nxt, v0),
            cache, cache,
            pl.BlockSpec((None, NB_QTOK, NB_KTOK), lambda b, h, j: (h, 0, 0)),
            pl.BlockSpec((None, NB_QTOK, NB_KTOK), lambda b, h, j: (j, 0, 0)),
        ],
        out_specs=pl.BlockSpec((NB_QTOK, HEAD_DIM), lambda b, h, j: (b * nj + j, h)),
        out_shape=jax.ShapeDtypeStruct((bsz * n_tok, n_heads * HEAD_DIM), BF16),
        compiler_params=_cparams("parallel", "parallel", "parallel"),
        name="neighbourhood_attention",
    )(u, u, u, u, u, u, u, ck, cv, bias, mask)


def _merge_kernel(ya_ref, yb_ref, yc_ref, wa_ref, wb_ref, wc_ref, ga_ref, gb_ref, gc_ref, o_ref):
    m = _sigmoid(ga_ref[...]) * _dot(ya_ref[...], wa_ref[...])
    m = m + _sigmoid(gb_ref[...]) * _dot(yb_ref[...], wb_ref[...])
    m = m + _sigmoid(gc_ref[...]) * _dot(yc_ref[...], wc_ref[...])
    o_ref[...] = m.astype(o_ref.dtype)


def merge_branches(ya, yb, yc, wa, wb, wc, u, gate_col0, tm=512, tn=512):
    n, aw = ya.shape
    bw, cw = wb.shape[0], wc.shape[0]
    d = wa.shape[1]
    tm = _pick(n, (tm, 256, 128))
    tn = _pick(d, (tn, 256, 128))
    assert gate_col0 % tn == 0
    g0 = gate_col0 // tn
    nd = d // tn
    gate = lambda gi: pl.BlockSpec((tm, tn), lambda i, j: (i, g0 + gi * nd + j))
    return pl.pallas_call(
        _merge_kernel,
        grid=(n // tm, nd),
        in_specs=[
            pl.BlockSpec((tm, aw), lambda i, j: (i, 0)),
            pl.BlockSpec((tm, bw), lambda i, j: (i, 0)),
            pl.BlockSpec((tm, cw), lambda i, j: (i, 0)),
            pl.BlockSpec((aw, tn), lambda i, j: (0, j)),
            pl.BlockSpec((bw, tn), lambda i, j: (0, j)),
            pl.BlockSpec((cw, tn), lambda i, j: (0, j)),
            gate(0), gate(1), gate(2),
        ],
        out_specs=pl.BlockSpec((tm, tn), lambda i, j: (i, j)),
        out_shape=jax.ShapeDtypeStruct((n, d), BF16),
        compiler_params=_cparams("parallel", "parallel"),
        name="merge_branches",
    )(ya, yb, yc, wa, wb, wc, u, u, u)


def _post_norm(alpha, x, gate, y, g, b):
    r = alpha * x + gate * y
    return _ln_rows(r) * g + b


def _outproj_ln_kernel(alpha, nk, m_ref, w_ref, x_ref, gate_ref, g_ref, b_ref, o_ref):
    k = pl.program_id(1)

    @pl.when(k == 0)
    def _():
        o_ref[...] = _dot(m_ref[...], w_ref[...])

    @pl.when(k > 0)
    def _():
        o_ref[...] += _dot(m_ref[...], w_ref[...])

    @pl.when(k == nk - 1)
    def _():
        rows = o_ref.shape[0]
        step = min(rows, 128)
        for r0 in range(0, rows, step):
            sl = slice(r0, r0 + step)
            o_ref[sl, :] = _post_norm(alpha, x_ref[sl, :], gate_ref[...], o_ref[sl, :],
                                      g_ref[...], b_ref[...])


def _tile_mod_spec(mod, d, tm, seq_len):
    if mod.shape[0] == 1:
        return pl.BlockSpec((None, 1, d), lambda i, *_: (0, 0, 0))
    return pl.BlockSpec((None, 1, d), lambda i, *_: (i * tm // seq_len, 0, 0))


def outproj_postnorm(m, w_out, x, gate, ln_g, ln_b, alpha, seq_len, tm=512, tk=512):
    n, d = x.shape
    kdim = m.shape[1]
    tm = _pick(n if gate.shape[0] == 1 else seq_len, (tm, 256, 128))
    tk = _pick(kdim, (tk, 256, 128))
    nk = kdim // tk
    vec = pl.BlockSpec((1, d), lambda i, k: (0, 0))
    return pl.pallas_call(
        functools.partial(_outproj_ln_kernel, alpha, nk),
        grid=(n // tm, nk),
        in_specs=[
            pl.BlockSpec((tm, tk), lambda i, k: (i, k)),
            pl.BlockSpec((tk, d), lambda i, k: (k, 0)),
            pl.BlockSpec((tm, d), lambda i, k: (i, 0)),
            _tile_mod_spec(gate, d, tm, seq_len),
            vec, vec,
        ],
        out_specs=pl.BlockSpec((tm, d), lambda i, k: (i, 0)),
        out_shape=jax.ShapeDtypeStruct((n, d), F32),
        compiler_params=_cparams("parallel", "arbitrary"),
        name="outproj_postnorm",
    )(m, w_out, x, gate, ln_g, ln_b)


def _select_kernel(cap, n, aff_ref, mask_ref):
    bits = lax.bitcast_convert_type(aff_ref[...], jnp.int32)
    n_exp = bits.shape[0]
    capf = jnp.float32(cap)

    def count(pred):
        return jnp.sum(pred.astype(F32), axis=1, keepdims=True)

    def value_step(_, carry):
        lo, hi = carry
        mid = lo + ((hi - lo + 1) >> 1)
        ok = count(bits >= mid) >= capf
        return jnp.where(ok, mid, lo), jnp.where(ok, hi, mid - 1)

    lo0 = jnp.zeros((n_exp, 1), jnp.int32)
    hi0 = jnp.full((n_exp, 1), 0x7F800000, jnp.int32)
    thr, _ = lax.fori_loop(0, 32, value_step, (lo0, hi0))
    gt = bits > thr
    eq = bits == thr
    need = capf - count(gt)
    idx = _iota(bits.shape, 1)

    def index_step(_, carry):
        lo, hi = carry
        mid = (lo + hi) >> 1
        ok = count(eq & (idx < mid)) >= need
        return jnp.where(ok, lo, mid), jnp.where(ok, mid, hi)

    _, bound = lax.fori_loop(0, int(np.ceil(np.log2(n))) + 1, index_step,
                             (jnp.zeros((n_exp, 1), jnp.int32), jnp.full((n_exp, 1), n, jnp.int32)))
    mask_ref[...] = (gt | (eq & (idx < bound))).astype(jnp.int32)


def expert_choice_mask(aff_t, cap):
    n_exp, n = aff_t.shape
    return pl.pallas_call(
        functools.partial(_select_kernel, cap, n),
        out_shape=jax.ShapeDtypeStruct((n_exp, n), jnp.int32),
        compiler_params=pltpu.CompilerParams(vmem_limit_bytes=VMEM_LIMIT_BYTES),
        name="expert_choice_mask",
    )(aff_t)


def _count_le(sorted_vals, x):
    return jnp.sum((sorted_vals[None, :] <= x[:, None]).astype(jnp.int32), axis=1)


def _visit_list(nvis, first_blk, n_work):
    na, nb = nvis.shape
    flat = nvis.reshape(-1)
    off_end = jnp.cumsum(flat)
    off_start = off_end - flat
    total = off_end[-1]
    w = jnp.minimum(jnp.arange(n_work, dtype=jnp.int32), total - 1)
    idx = jnp.minimum(_count_le(off_end, w), na * nb - 1)
    blk = first_blk.reshape(-1)[idx] + (w - off_start[idx])
    valid = (jnp.arange(n_work) < total).astype(jnp.int32)
    return idx // nb, idx % nb, blk.astype(jnp.int32), valid


MOE_GROUP = 4
NO_SLOT = 1 << 28


def _group_items(key, valid, n_keys, group, n_steps):
    n_items = key.shape[0]
    cnt = jnp.zeros((n_keys,), jnp.int32).at[key].add(valid)
    start = jnp.cumsum(cnt) - cnt
    per_key = (cnt + group - 1) // group
    s_end = jnp.cumsum(per_key)
    s_start = s_end - per_key
    total = s_end[-1]
    step = jnp.arange(n_steps, dtype=jnp.int32)
    real = step < total
    sc = jnp.minimum(step, total - 1)
    k = jnp.minimum(_count_le(s_end, sc), n_keys - 1)
    j = sc - s_start[k]
    within = j[None, :] * group + jnp.arange(group, dtype=jnp.int32)[:, None]
    ok = (within < cnt[k][None, :]) & real[None, :]
    item = jnp.clip(start[k][None, :] + jnp.minimum(within, cnt[k][None, :] - 1), 0, n_items - 1)
    first = ((j == 0) & real).astype(jnp.int32)
    last = ((j == per_key[k] - 1) & real).astype(jnp.int32)
    return item, ok.astype(jnp.int32), first, last


def routing_plan(mask, cap):
    n_exp, n = mask.shape
    nsb = cap // SLOT_BLK
    pos = jnp.cumsum(mask, axis=1) - mask
    posm = jnp.where(mask > 0, pos, -1).astype(jnp.int32)

    def tile_visits(tile):
        cnt = mask.reshape(n_exp, n // tile, tile).sum(-1)
        cend = jnp.cumsum(cnt, axis=1)
        lo = jnp.minimum((cend - cnt) // SLOT_BLK, nsb - 1)
        return jnp.where(cnt > 0, (cend - 1) // SLOT_BLK - lo + 1, 0), lo

    n_blk = n_exp * nsb
    grp = MOE_GROUP
    nvis, sb_lo = tile_visits(ROUTE_TILE)
    n_work = n_exp * (nsb + n // ROUTE_TILE)
    e, t, sb, valid = _visit_list(nvis, sb_lo, n_work)
    blk = e * nsb + sb
    item, ok, first, last = _group_items(blk, valid, n_blk, grp, n_work // grp + n_blk + 1)
    blk_g = blk[item[0]]
    dispatch = (blk_g // nsb, blk_g, first, last, t[item], jnp.where(ok > 0, sb[item] * SLOT_BLK, NO_SLOT))
    nvis, sb_lo = tile_visits(TOK_TILE)
    nt = n // TOK_TILE
    n_work = n_exp * (nsb + nt)
    nvis_t = nvis.T.at[:, 0].max(1)
    t2, e2, sb2, valid2 = _visit_list(nvis_t, sb_lo.T, n_work + nt)
    item, ok, first, last = _group_items(t2, valid2, nt, grp, (n_work + nt) // grp + nt + 1)
    combine = (t2[item[0]], first, last, e2[item], e2[item] * nsb + sb2[item],
               jnp.where(ok > 0, sb2[item] * SLOT_BLK, NO_SLOT))
    return posm, dispatch, combine


def _route_kernel(group, e_ref, blk_ref, first_ref, last_ref, tile_ref, base_ref, *refs):
    posm_refs, aff_refs = refs[:group], refs[group:2 * group]
    o_ref, acc_ref = refs[2 * group:]
    w = pl.program_id(0)
    rows = _iota((SLOT_BLK, ROUTE_TILE), 0)
    tot = None
    for q in range(group):
        onehot = ((rows + base_ref[q, w]) == posm_refs[q][...]).astype(BF16)
        part = _dot_exact_lhs(onehot, aff_refs[q][...])
        tot = part if tot is None else tot + part

    @pl.when(first_ref[w] == 1)
    def _():
        acc_ref[...] = tot

    @pl.when(first_ref[w] == 0)
    def _():
        acc_ref[...] += tot

    @pl.when(last_ref[w] == 1)
    def _():
        o_ref[...] = acc_ref[...]


def moe_route(aff, posm, plan, cap):
    n = aff.shape[0]
    n_exp = posm.shape[0]
    grp = MOE_GROUP
    n_steps = plan[0].shape[0]
    pos_spec = lambda q: pl.BlockSpec((None, 1, ROUTE_TILE), lambda w, e, b, f, l, t, s: (e[w], 0, t[q, w]))
    aff_spec = lambda q: pl.BlockSpec((ROUTE_TILE, LANE), lambda w, e, b, f, l, t, s: (t[q, w], 0))
    grid_spec = pltpu.PrefetchScalarGridSpec(
        num_scalar_prefetch=6,
        grid=(n_steps,),
        in_specs=[pos_spec(q) for q in range(grp)] + [aff_spec(q) for q in range(grp)],
        out_specs=pl.BlockSpec((SLOT_BLK, LANE), lambda w, e, b, f, l, t, s: (b[w], 0)),
        scratch_shapes=[pltpu.VMEM((SLOT_BLK, LANE), F32)],
    )
    posm3 = posm.reshape(n_exp, 1, n)
    return pl.pallas_call(
        functools.partial(_route_kernel, grp),
        grid_spec=grid_spec,
        out_shape=jax.ShapeDtypeStruct((n_exp * cap, LANE), F32),
        compiler_params=_cparams("arbitrary"),
        name="moe_route",
    )(*plan, *([posm3] * grp), *([aff] * grp))


GATHER_ROWS = 256


def _gather_kernel(idx_ref, tok_hbm, o_ref, buf, sem):
    base = pl.program_id(0) * GATHER_ROWS

    def row_copy(r):
        return pltpu.make_async_copy(tok_hbm.at[pl.ds(idx_ref[base + r], 1), :],
                                     buf.at[pl.ds(r, 1), :], sem)

    def start(r, carry):
        row_copy(r).start()
        return carry

    def wait(r, carry):
        row_copy(r).wait()
        return carry

    lax.fori_loop(0, GATHER_ROWS, start, 0, unroll=8)
    lax.fori_loop(0, GATHER_ROWS, wait, 0, unroll=8)
    o_ref[...] = buf[...].astype(o_ref.dtype)


def moe_gather(tokens, idx):
    n, d = tokens.shape
    rows = idx.shape[0]
    assert rows % GATHER_ROWS == 0
    grid_spec = pltpu.PrefetchScalarGridSpec(
        num_scalar_prefetch=1,
        grid=(rows // GATHER_ROWS,),
        in_specs=[pl.BlockSpec(memory_space=pl.ANY)],
        out_specs=pl.BlockSpec((GATHER_ROWS, d), lambda i, idx_ref: (i, 0)),
        scratch_shapes=[pltpu.VMEM((GATHER_ROWS, d), F32), pltpu.SemaphoreType.DMA(())],
    )
    return pl.pallas_call(
        _gather_kernel,
        grid_spec=grid_spec,
        out_shape=jax.ShapeDtypeStruct((rows, d), BF16),
        compiler_params=_cparams("arbitrary"),
        name="moe_gather",
    )(idx, tokens)


def _ffn_up_kernel(x_ref, w1_ref, w3_ref, o_ref):
    x = x_ref[...]
    a = _dot(x, w1_ref[...])
    o_ref[...] = (a * _sigmoid(a) * _dot(x, w3_ref[...])).astype(o_ref.dtype)


def _ffn_down_kernel(h_ref, w2_ref, aff_ref, o_ref):
    aff = aff_ref[...]
    gval = jnp.sum(jnp.where(_iota(aff.shape, 1) == pl.program_id(0), aff, 0.0), axis=1, keepdims=True)
    o_ref[...] = (_dot(h_ref[...], w2_ref[...]) * gval).astype(o_ref.dtype)


def expert_ffn(xe, aff_rows, w1, w3, w2, layer, cap, tn=512):
    _, n_exp, d, ff = w1.shape
    tm = _pick(cap, (1024, 512, 256, 128))
    nm = cap // tm
    tf = _pick(ff, (tn, 256, 128))
    hid = pl.pallas_call(
        _ffn_up_kernel,
        grid=(n_exp, nm, ff // tf),
        in_specs=[
            pl.BlockSpec((tm, d), lambda e, i, j: (e * nm + i, 0)),
            pl.BlockSpec((None, None, d, tf), lambda e, i, j: (layer, e, 0, j)),
            pl.BlockSpec((None, None, d, tf), lambda e, i, j: (layer, e, 0, j)),
        ],
        out_specs=pl.BlockSpec((tm, tf), lambda e, i, j: (e * nm + i, j)),
        out_shape=jax.ShapeDtypeStruct((n_exp * cap, ff), BF16),
        compiler_params=_cparams("parallel", "parallel", "parallel"),
        name="expert_ffn_up",
    )(xe, w1, w3)
    td = _pick(d, (tn, 256, 128))
    return pl.pallas_call(
        _ffn_down_kernel,
        grid=(n_exp, nm, d // td),
        in_specs=[
            pl.BlockSpec((tm, ff), lambda e, i, j: (e * nm + i, 0)),
            pl.BlockSpec((None, None, ff, td), lambda e, i, j: (layer, e, 0, j)),
            pl.BlockSpec((tm, LANE), lambda e, i, j: (e * nm + i, 0)),
        ],
        out_specs=pl.BlockSpec((tm, td), lambda e, i, j: (e * nm + i, j)),
        out_shape=jax.ShapeDtypeStruct((n_exp * cap, d), BF16),
        compiler_params=_cparams("parallel", "parallel", "parallel"),
        name="expert_ffn_down",
    )(hid, w2, aff_rows)


def _combine_kernel(alpha, group, tile_ref, first_ref, last_ref, e_ref, blk_ref, base_ref,
                    posm_ref, *refs):
    ye_refs = refs[:group]
    x_ref, gate_ref, g_ref, b_ref, o_ref, acc_ref = refs[group:]
    w = pl.program_id(0)
    posm = posm_ref[...].astype(F32)
    lane = _iota(posm.shape, 1)
    cols = _iota((TOK_TILE, SLOT_BLK), 1)
    tot = None
    for q in range(group):
        col = jnp.sum(jnp.where(lane == e_ref[q, w], posm, 0.0), axis=1, keepdims=True)
        onehot = ((cols + base_ref[q, w]).astype(F32) == col).astype(BF16)
        part = _dot(onehot, ye_refs[q][...])
        tot = part if tot is None else tot + part

    @pl.when(first_ref[w] == 1)
    def _():
        acc_ref[...] = tot

    @pl.when(first_ref[w] == 0)
    def _():
        acc_ref[...] += tot

    @pl.when(last_ref[w] == 1)
    def _():
        o_ref[...] = _post_norm(alpha, x_ref[...], gate_ref[...], acc_ref[...], g_ref[...], b_ref[...])


def moe_combine_postnorm(ye, posm_t, plan, x, gate, ln_g, ln_b, alpha, seq_len):
    n, d = x.shape
    n_exp = posm_t.shape[1]
    grp = MOE_GROUP
    n_steps = plan[0].shape[0]
    assert seq_len % TOK_TILE == 0
    tile = lambda w, t, *_: (t[w], 0)
    if gate.shape[0] == 1:
        gate_map = lambda w, t, *_: (0, 0, 0)
    else:
        gate_map = lambda w, t, *_: (t[w] * TOK_TILE // seq_len, 0, 0)
    vec = pl.BlockSpec((1, d), lambda w, *_: (0, 0))
    ye_spec = lambda q: pl.BlockSpec((SLOT_BLK, d), lambda w, t, f, l, e, b, s: (b[q, w], 0))
    grid_spec = pltpu.PrefetchScalarGridSpec(
        num_scalar_prefetch=6,
        grid=(n_steps,),
        in_specs=([pl.BlockSpec((TOK_TILE, n_exp), tile)] + [ye_spec(q) for q in range(grp)]
                  + [pl.BlockSpec((TOK_TILE, d), tile), pl.BlockSpec((None, 1, d), gate_map), vec, vec]),
        out_specs=pl.BlockSpec((TOK_TILE, d), tile),
        scratch_shapes=[pltpu.VMEM((TOK_TILE, d), F32)],
    )
    return pl.pallas_call(
        functools.partial(_combine_kernel, alpha, grp),
        grid_spec=grid_spec,
        out_shape=jax.ShapeDtypeStruct((n, d), F32),
        compiler_params=_cparams("arbitrary"),
        name="moe_combine_postnorm",
    )(*plan, posm_t, *([ye] * grp), x, gate, ln_g, ln_b)


def ec_moe_postnorm(x, mods_shift, mods_scale, gate, router_w, w1, w3, w2, layer, ln_g, ln_b, alpha):
    bsz, t, d = x.shape
    n = bsz * t
    n_exp = router_w.shape[1]
    cap = max(1, CAP_FACTOR * n // n_exp)
    assert cap % SLOT_BLK == 0 and n % TOK_TILE == 0 and n % ROUTE_TILE == 0
    h, aff = ln_modulate_router(x, mods_shift, mods_scale, router_w)
    h = h.reshape(n, d)
    aff = aff.reshape(n, LANE)
    mask = expert_choice_mask(aff[:, :n_exp].T, cap)
    posm, dispatch, combine = routing_plan(mask, cap)
    aff_rows = moe_route(aff, posm, dispatch, cap)
    slot_token = jnp.clip(aff_rows[:, n_exp].astype(jnp.int32), 0, n - 1)
    xe = moe_gather(h, slot_token)
    ye = expert_ffn(xe, aff_rows, w1, w3, w2, layer, cap)
    out = moe_combine_postnorm(ye, posm.T, combine, x.reshape(n, d), gate, ln_g, ln_b, alpha, t)
    return out.reshape(bsz, t, d)


def prep_rwkv_params(rw, aw):
    a_cols = rw["mu"].shape[0]
    ap = round_up(a_cols, LANE)
    gp = ap - 3 * aw - 2 * LANE
    assert rw["w2"].shape[:2] == (2, LANE // 2) and rw["a2"].shape[:2] == (2, LANE // 2)
    assert gp >= rw["g2"].shape[0] and aw % LANE == 0
    row = lambda a: a.reshape(1, aw)
    return dict(
        mu=jnp.pad(rw["mu"], (0, ap - a_cols)).reshape(1, ap),
        w0=rw["w0"], w2=_hi_lo(rw["w2"].reshape(LANE, aw)), a0=rw["a0"],
        a2=_hi_lo(rw["a2"].reshape(LANE, aw)),
        g2=_hi_lo(jnp.pad(rw["g2"], ((0, gp - rw["g2"].shape[0]), (0, 0)))),
        k_k=row(rw["k_k"]), k_a=row(rw["k_a"]), r_k=row(rw["r_k"]),
        ln_w=row(rw["ln_w"]), ln_b=row(rw["ln_b"]), ap=ap, aw=aw)


def states_to_pairs(s):
    bsz, h = s.shape[:2]
    st = jnp.swapaxes(s, -1, -2).reshape(bsz, h // 2, 2, A_HEAD, A_HEAD)
    z = jnp.zeros_like(st[:, :, 0])
    top = jnp.concatenate([st[:, :, 0], z], axis=-1)
    bot = jnp.concatenate([z, st[:, :, 1]], axis=-1)
    return jnp.concatenate([top, bot], axis=-2)


def pairs_to_states(sp):
    bsz = sp.shape[0]
    st = jnp.stack([sp[:, :, :A_HEAD, :A_HEAD], sp[:, :, A_HEAD:, A_HEAD:]], axis=2)
    return jnp.swapaxes(st.reshape(bsz, -1, A_HEAD, A_HEAD), -1, -2)


def rwkv_branch(u, bsz, t, rwp, s0f, s0b):
    aw, ap = rwp["aw"], rwp["ap"]
    n = bsz * t
    pack_f, pack_b, bonus, g = rwkv_prep(u, t, rwp, aw, ap)
    sh = lambda a: a.reshape(bsz, t, SCAN_FIELDS * aw)
    yf, sf = rwkv_scan(sh(pack_f), states_to_pairs(s0f), False)
    yb, sb = rwkv_scan(sh(pack_b), states_to_pairs(s0b), True)
    ya = rwkv_post(yf.reshape(n, aw), yb.reshape(n, aw), bonus, g, rwp["ln_w"], rwp["ln_b"])
    return ya, pairs_to_states(sf), pairs_to_states(sb)


def _mod_rows(m):
    return [m[:, i][:, None, :] for i in range(6)]


def kernel(x_prompt, x_sample, cache_win_k, cache_win_v, cache_nb_k, cache_nb_v, state_rwkv_fwd, state_rwkv_bwd, c, c_ctx, ada_w, ada_b, w_in, rwkv_mu, rwkv_w0, rwkv_w2, rwkv_a0, rwkv_a2, rwkv_g2, rwkv_kk, rwkv_ka, rwkv_rk, rwkv_lnx_w, rwkv_lnx_b, win_sink, nb_rpb, w_br_a, w_br_b, w_br_c, w_out, ln1_g, ln1_b, ln2_g, ln2_b, router_w, exp_w1, exp_w3, exp_w2):
    depth, d, in_w = w_in.shape
    bsz, seq, _ = x_prompt.shape
    dbsz, dseq, _ = x_sample.shape
    aw = rwkv_w0.shape[-1]
    a_heads = aw // A_HEAD
    a_cols = rwkv_mu.shape[-1]
    ap = round_up(a_cols, LANE)
    n_b = win_sink.shape[-1]
    n_kv = cache_win_k.shape[3]
    group = n_b // n_kv
    n_c = nb_rpb.shape[1]
    bw, cw = n_b * HEAD_DIM, n_c * HEAD_DIM
    alpha = float((2 * depth) ** 0.25)
    assert in_w == a_cols + bw + 2 * n_kv * HEAD_DIM + 3 * cw + 3 * d

    qb0 = ap // LANE
    kb0 = qb0 + n_b
    vb0 = kb0 + n_kv
    qn0 = vb0 + n_kv
    kn0 = qn0 + n_c
    vn0 = kn0 + n_c
    gate_col0 = (vn0 + n_c) * LANE

    rows = 1 + dbsz
    cvecs = jnp.pad(jnp.concatenate([c_ctx[None], c], axis=0), ((0, round_up(rows, 16) - rows), (0, 0)))
    mods = ada_mods(cvecs, ada_w, ada_b).reshape(depth, -1, 6, d)

    xp, xs = x_prompt, x_sample
    np_tok, ns_tok = bsz * seq, dbsz * dseq
    win_k, win_v, nb_k, nb_v, st_f, st_b = [], [], [], [], [], []
    zero_state = jnp.zeros((bsz, a_heads, A_HEAD, A_HEAD), F32)
    w_cat = jnp.concatenate(
        [w_in[:, :, :a_cols].astype(BF16), jnp.zeros((depth, d, ap - a_cols), BF16),
         w_in[:, :, a_cols:].astype(BF16)], axis=2)
    w1, w3, w2 = exp_w1.astype(BF16), exp_w3.astype(BF16), exp_w2.astype(BF16)
    for l in range(depth):
        rwp = prep_rwkv_params(
            {"mu": rwkv_mu[l], "w0": rwkv_w0[l], "w2": rwkv_w2[l], "a0": rwkv_a0[l], "a2": rwkv_a2[l],
             "g2": rwkv_g2[l], "k_k": rwkv_kk[l], "k_a": rwkv_ka[l], "r_k": rwkv_rk[l],
             "ln_w": rwkv_lnx_w[l], "ln_b": rwkv_lnx_b[l]}, aw)
        wa, wb, wc = w_br_a[l].astype(BF16), w_br_b[l].astype(BF16), w_br_c[l].astype(BF16)
        wo = w_out[l].astype(BF16)
        g1, b1 = ln1_g[l].reshape(1, d), ln1_b[l].reshape(1, d)
        g2, b2 = ln2_g[l].reshape(1, d), ln2_b[l].reshape(1, d)

        sh1, sc1, gt1, sh2, sc2, gt2 = _mod_rows(mods[l, 0:1])
        h = ln_modulate(xp, sh1, sc1).reshape(np_tok, d)
        u = matmul(h, w_cat, l, name="in_proj")
        ya, s_f, s_b = rwkv_branch(u, bsz, seq, rwp, zero_state, zero_state)
        yb = ctx_attention(u, bsz, seq, qb0, kb0, vb0, n_b, group, win_sink[l])
        yc = ctx_attention(u, bsz, seq, qn0, kn0, vn0, n_c, 1, None)
        m = merge_branches(ya, yb, yc, wa, wb, wc, u, gate_col0)
        x1 = outproj_postnorm(m, wo, xp.reshape(np_tok, d), gt1, g1, b1, alpha, seq)
        xp = ec_moe_postnorm(x1.reshape(bsz, seq, d), sh2, sc2, gt2, router_w[l], w1, w3, w2, l, g2, b2, alpha)
        cols = lambda c0, nh: u[:, c0 * LANE:(c0 + nh) * LANE].reshape(bsz, seq, nh, HEAD_DIM)
        win_k.append(cols(kb0, n_kv))
        win_v.append(cols(vb0, n_kv))
        nb_k.append(cols(kn0, n_c))
        nb_v.append(cols(vn0, n_c))
        st_f.append(s_f)
        st_b.append(s_b)

        sh1, sc1, gt1, sh2, sc2, gt2 = _mod_rows(mods[l, 1:1 + dbsz])
        h = ln_modulate(xs, sh1, sc1).reshape(ns_tok, d)
        u = matmul(h, w_cat, l, name="in_proj")
        ya, _, _ = rwkv_branch(u, dbsz, dseq, rwp, state_rwkv_fwd[:, l], state_rwkv_bwd[:, l])
        qk_rot = rope(u, dbsz, dseq, qb0, n_b + n_kv)
        heads_first = lambda a: jnp.swapaxes(a[:, l], 1, 2)
        yb = window_attention(qk_rot, u, vb0, dbsz, dseq, n_kv, group,
                              heads_first(cache_win_k), heads_first(cache_win_v), win_sink[l])
        yc = neighbourhood_attention(u, (qn0, kn0, vn0), dbsz, dseq, n_c,
                                     heads_first(cache_nb_k), heads_first(cache_nb_v), nb_rpb[l])
        m = merge_branches(ya, yb, yc, wa, wb, wc, u, gate_col0)
        x1 = outproj_postnorm(m, wo, xs.reshape(ns_tok, d), gt1, g1, b1, alpha, dseq)
        xs = ec_moe_postnorm(x1.reshape(dbsz, dseq, d), sh2, sc2, gt2, router_w[l], w1, w3, w2, l, g2, b2, alpha)

    stack = lambda xs_: jnp.stack(xs_, axis=1)
    return (xp, xs, stack(win_k), stack(win_v), stack(nb_k), stack(nb_v), stack(st_f), stack(st_b))
```

```python
import functools

import numpy as np
import jax
import jax.numpy as jnp
from jax import lax
from jax.experimental import pallas as pl
from jax.experimental.pallas import tpu as pltpu

F32 = jnp.float32
BF16 = jnp.bfloat16

HEAD_DIM = 128
A_HEAD = 64
GRID_W = 64
WIN = 128
QBLK = 128
NB_ROWS = 8
NB_COLS = 16
CAP_FACTOR = 2
ROPE_THETA = 10000.0
LN_EPS = 1e-6
GN_EPS = 64e-5
NEG = -1e30

LANE = 128
SUBLANE = 8
VMEM_LIMIT_BYTES = 56 * 1024 * 1024

SCAN_CHUNK = 64
SLOT_BLK = 128
TOK_TILE = 256
ROUTE_TILE = 256


def _cparams(*sem):
    return pltpu.CompilerParams(dimension_semantics=sem, vmem_limit_bytes=VMEM_LIMIT_BYTES)


def _dot(a, b):
    return jnp.dot(a, b, preferred_element_type=F32)


def _dot_nt(a, b):
    return lax.dot_general(a, b, (((1,), (1,)), ((), ())), preferred_element_type=F32)


def _split2(x):
    hi = x.astype(BF16)
    lo = (x - hi.astype(F32)).astype(BF16)
    return hi, lo


def _split3(x):
    hi = x.astype(BF16)
    r1 = x - hi.astype(F32)
    mid = r1.astype(BF16)
    lo = (r1 - mid.astype(F32)).astype(BF16)
    return hi, mid, lo


def _dot3(a, b):
    ah, al = _split2(a)
    bh, bl = _split2(b)
    return _dot(ah, bh) + (_dot(ah, bl) + _dot(al, bh))


def _hi_lo(w):
    hi, lo = _split2(w)
    return jnp.stack([hi, lo])


def _dot3_split_rhs(a, b_ref):
    ah, al = _split2(a)
    bh = b_ref[0]
    return _dot(ah, bh) + (_dot(ah, b_ref[1]) + _dot(al, bh))


def _dot3_nt(a, b):
    ah, al = _split2(a)
    bh, bl = _split2(b)
    return _dot_nt(ah, bh) + (_dot_nt(ah, bl) + _dot_nt(al, bh))


def _dot_exact_lhs(a_bf16, b):
    bh, bm, bl = _split3(b)
    return _dot(a_bf16, bh) + (_dot(a_bf16, bm) + _dot(a_bf16, bl))


def _sigmoid(x):
    return 1.0 / (1.0 + jnp.exp(-x))


def _iota(shape, dim):
    return lax.broadcasted_iota(jnp.int32, shape, dim)


def _ada_kernel(c_ref, w_ref, b_ref, o_ref):
    c = c_ref[...]
    a = c * _sigmoid(c)
    o_ref[...] = _dot3(a, w_ref[...]) + b_ref[...]


def ada_mods(cvecs, ada_w, ada_b, tn=512):
    depth, d, n6 = ada_w.shape
    rows = cvecs.shape[0]
    return pl.pallas_call(
        _ada_kernel,
        grid=(depth, n6 // tn),
        in_specs=[
            pl.BlockSpec((rows, d), lambda l, j: (0, 0)),
            pl.BlockSpec((None, d, tn), lambda l, j: (l, 0, j)),
            pl.BlockSpec((None, 1, tn), lambda l, j: (l, 0, j)),
        ],
        out_specs=pl.BlockSpec((None, rows, tn), lambda l, j: (l, 0, j)),
        out_shape=jax.ShapeDtypeStruct((depth, rows, n6), F32),
        compiler_params=_cparams("parallel", "parallel"),
        name="ada_mods",
    )(cvecs, ada_w, ada_b.reshape(depth, 1, n6))


def _ln_rows(x):
    mu = jnp.mean(x, axis=-1, keepdims=True)
    xc = x - mu
    var = jnp.mean(xc * xc, axis=-1, keepdims=True)
    return xc * lax.rsqrt(var + LN_EPS)


def _lnmod_kernel(x_ref, sh_ref, sc_ref, o_ref):
    y = _ln_rows(x_ref[...])
    o_ref[...] = (y * (1.0 + sc_ref[...]) + sh_ref[...]).astype(o_ref.dtype)


def _mod_spec(mod, d):
    if mod.shape[0] == 1:
        return pl.BlockSpec((None, 1, d), lambda b, i: (0, 0, 0))
    return pl.BlockSpec((None, 1, d), lambda b, i: (b, 0, 0))


def ln_modulate(x, shift, scale, tt=256):
    bsz, t, d = x.shape
    return pl.pallas_call(
        _lnmod_kernel,
        grid=(bsz, t // tt),
        in_specs=[
            pl.BlockSpec((None, tt, d), lambda b, i: (b, i, 0)),
            _mod_spec(shift, d),
            _mod_spec(scale, d),
        ],
        out_specs=pl.BlockSpec((None, tt, d), lambda b, i: (b, i, 0)),
        out_shape=jax.ShapeDtypeStruct((bsz, t, d), BF16),
        compiler_params=_cparams("parallel", "parallel"),
        name="ln_modulate",
    )(x, shift, scale)


def _lnmod_router_kernel(n_exp, seq_len, x_ref, sh_ref, sc_ref, rw_ref, h_ref, aff_ref):
    y = _ln_rows(x_ref[...])
    h = y * (1.0 + sc_ref[...]) + sh_ref[...]
    h_ref[...] = h
    logits = _dot3(h, rw_ref[...])
    lane = _iota(logits.shape, 1)
    logits = jnp.where(lane < n_exp, logits, NEG)
    m = jnp.max(logits, axis=-1, keepdims=True)
    e = jnp.exp(logits - m)
    aff = e / jnp.sum(e, axis=-1, keepdims=True)
    tt = aff.shape[0]
    tok = pl.program_id(0) * seq_len + pl.program_id(1) * tt + _iota(aff.shape, 0)
    aff_ref[...] = jnp.where(lane == n_exp, tok.astype(F32), aff)


def ln_modulate_router(x, shift, scale, router_w, tt=256):
    bsz, t, d = x.shape
    n_exp = router_w.shape[1]
    assert n_exp < LANE and bsz * t < (1 << 24)
    rw = jnp.pad(router_w, ((0, 0), (0, LANE - n_exp)))
    return pl.pallas_call(
        functools.partial(_lnmod_router_kernel, n_exp, t),
        grid=(bsz, t // tt),
        in_specs=[
            pl.BlockSpec((None, tt, d), lambda b, i: (b, i, 0)),
            _mod_spec(shift, d),
            _mod_spec(scale, d),
            pl.BlockSpec((d, LANE), lambda b, i: (0, 0)),
        ],
        out_specs=[
            pl.BlockSpec((None, tt, d), lambda b, i: (b, i, 0)),
            pl.BlockSpec((None, tt, LANE), lambda b, i: (b, i, 0)),
        ],
        out_shape=[
            jax.ShapeDtypeStruct((bsz, t, d), F32),
            jax.ShapeDtypeStruct((bsz, t, LANE), F32),
        ],
        compiler_params=_cparams("parallel", "parallel"),
        name="ln_modulate_router",
    )(x, shift, scale, rw)


def _mm_kernel(a_ref, b_ref, o_ref):
    o_ref[...] = _dot(a_ref[...], b_ref[...]).astype(o_ref.dtype)


def _pick(n, pref):
    for c in pref:
        if n % c == 0:
            return c
    return n


def matmul(a, b, layer, out_dtype=F32, tm=None, tn=None, name="matmul"):
    m, k = a.shape
    n = b.shape[2]
    tm = tm or _pick(m, (1024, 512, 256, 128))
    tn = tn or _pick(n, (768, 512, 256, 128))
    return pl.pallas_call(
        _mm_kernel,
        grid=(m // tm, n // tn),
        in_specs=[
            pl.BlockSpec((tm, k), lambda i, j: (i, 0)),
            pl.BlockSpec((None, k, tn), lambda i, j: (layer, 0, j)),
        ],
        out_specs=pl.BlockSpec((tm, tn), lambda i, j: (i, j)),
        out_shape=jax.ShapeDtypeStruct((m, n), out_dtype),
        compiler_params=_cparams("parallel", "parallel"),
        name=name,
    )(a, b)


HEAD_SHIFT = 6
assert (1 << HEAD_SHIFT) == A_HEAD and 2 * A_HEAD == LANE and SCAN_CHUNK == A_HEAD


def _head_block_ones():
    i = _iota((LANE, LANE), 0) >> HEAD_SHIFT
    j = _iota((LANE, LANE), 1) >> HEAD_SHIFT
    return (i == j).astype(BF16)


def _seg_sum(x):
    bd = _head_block_ones()
    outs = []
    for c in range(x.shape[1] // LANE):
        hi, mid, lo = _split3(x[:, c * LANE:(c + 1) * LANE])
        outs.append(_dot(hi, bd) + (_dot(mid, bd) + _dot(lo, bd)))
    return jnp.concatenate(outs, axis=1)


SCAN_FIELDS = 6


def _rwkv_prep_kernel(aw, gp, gw, tiles_per_seq,
                      x_ref, xp_ref, xn_ref, mu_ref, w0_ref, w2_ref, a0_ref, a2_ref, g2_ref,
                      kkp_ref, kap_ref, rkp_ref,
                      packf_o, packb_o, bonus_o, g_o):
    i = pl.program_id(0)
    x = x_ref[...]
    tt = x.shape[0]
    row = _iota(x.shape, 0)
    pos = i % tiles_per_seq
    prev_row = jnp.where(pos == 0, 0.0, xp_ref[SUBLANE - 1:SUBLANE, :])
    next_row = jnp.where(pos == tiles_per_seq - 1, 0.0, xn_ref[0:1, :])
    prev = jnp.where(row == 0, prev_row, pltpu.roll(x, 1, 0))
    nxt = jnp.where(row == tt - 1, next_row, pltpu.roll(x, tt - 1, 0))
    xs = x + mu_ref[...] * (0.5 * (prev + nxt) - x)

    r = xs[:, 0:aw]
    k = xs[:, aw:2 * aw]
    v = xs[:, 2 * aw:3 * aw]
    o = 3 * aw
    wlo = jnp.tanh(xs[:, o:o + LANE])
    alo = xs[:, o + LANE:o + 2 * LANE]
    glo = _sigmoid(xs[:, o + 2 * LANE:o + 2 * LANE + gp])
    lane = _iota((1, LANE), 1)
    exp_mhalf = float(np.exp(-0.5))

    g_o[...] = _dot3_split_rhs(glo, g2_ref)

    kk = k * kkp_ref[...]
    nrm = jnp.sqrt(_seg_sum(kk * kk))
    kk = kk / jnp.maximum(nrm, 1e-12)
    bonus_o[...] = _seg_sum(r * k * rkp_ref[...]) * v

    for d, pack_o in enumerate((packf_o, packb_o)):
        sel = ((lane >> HEAD_SHIFT) == d).astype(F32)
        wl = w0_ref[d:d + 1, :] + _dot3_split_rhs(wlo * sel, w2_ref)
        lw = -_sigmoid(wl) * exp_mhalf
        a = _sigmoid(a0_ref[d:d + 1, :] + _dot3_split_rhs(alo * sel, a2_ref))
        fields = (lw, kk, kk * a, k * (1.0 + (a - 1.0) * kap_ref[...]), r, v)
        for g in range(aw // gw):
            for f, arr in enumerate(fields):
                c0 = (g * SCAN_FIELDS + f) * gw
                pack_o[:, c0:c0 + gw] = arr[:, g * gw:(g + 1) * gw]


def scan_group_width(aw, max_pairs=6):
    n_pairs = aw // LANE
    return LANE * max(p for p in range(1, max_pairs + 1) if n_pairs % p == 0)


def rwkv_prep(u, seq_len, rw, aw, ap, tt=128):
    n = u.shape[0]
    gw = scan_group_width(aw)
    gp = ap - 3 * aw - 2 * LANE
    n8 = n // SUBLANE
    tpb = tt // SUBLANE
    full = lambda shape: pl.BlockSpec(shape, lambda i: (0,) * len(shape))
    tok = pl.BlockSpec((tt, aw), lambda i: (i, 0))
    pack = pl.BlockSpec((tt, SCAN_FIELDS * aw), lambda i: (i, 0))
    pack_shape = jax.ShapeDtypeStruct((n, SCAN_FIELDS * aw), F32)
    outs = pl.pallas_call(
        functools.partial(_rwkv_prep_kernel, aw, gp, gw, seq_len // tt),
        grid=(n // tt,),
        in_specs=[
            pl.BlockSpec((tt, ap), lambda i: (i, 0)),
            pl.BlockSpec((SUBLANE, ap), lambda i: (jnp.maximum(i * tpb - 1, 0), 0)),
            pl.BlockSpec((SUBLANE, ap), lambda i: (jnp.minimum((i + 1) * tpb, n8 - 1), 0)),
            full((1, ap)), full((2, aw)), full((2, LANE, aw)), full((2, aw)), full((2, LANE, aw)),
            full((2, gp, aw)), full((1, aw)), full((1, aw)), full((1, aw)),
        ],
        out_specs=[pack, pack, tok, tok],
        out_shape=[pack_shape, pack_shape] + [jax.ShapeDtypeStruct((n, aw), F32)] * 2,
        compiler_params=_cparams("parallel"),
        name="rwkv_prep",
    )(u, u, u, rw["mu"], rw["w0"], rw["w2"], rw["a0"], rw["a2"], rw["g2"],
      rw["k_k"], rw["k_a"], rw["r_k"])
    return outs


def _scan_chunks(rev, toks, sts, consts):
    tri, strict, incl, blk, eye, head0, t2, s2 = consts
    c = SCAN_CHUNK
    ident = (t2 == s2).astype(F32)
    zero = jnp.zeros((), BF16)
    cat = lambda *xs: jnp.concatenate(xs, axis=0)

    def each(f, *lists):
        return [f(*args) for args in zip(*lists)]

    def bd(x):
        xb = x.astype(BF16)
        return jnp.where(blk, cat(xb, xb), zero)

    lw, kk, bb, kd, r, v = (list(t) for t in zip(*toks))
    big_l = each(lambda a: _dot_exact_lhs(tri, a), lw)
    l_tot = each(lambda a: a[0:1, :] if rev else a[c - 1:c, :], big_l)
    lhs = each(lambda k_, r_, l_, w_: cat(-k_ * jnp.exp(l_ - w_), r_ * jnp.exp(l_)).astype(BF16),
               kk, r, big_l, lw)
    gi = each(lambda l_: jnp.exp(-l_), big_l)
    bt = each(lambda b_, g_: (b_ * g_).astype(BF16), bb, gi)
    kt = each(lambda k_, g_: (k_ * g_).astype(BF16), kd, gi)
    rhs = each(lambda b_, k_: cat(jnp.where(head0, b_, zero), jnp.where(head0, zero, b_),
                                  jnp.where(head0, k_, zero), jnp.where(head0, zero, k_)), bt, kt)
    p = each(_dot_nt, lhs, rhs)
    nab = each(lambda p_: jnp.where(strict, p_[0:c, 0:LANE], 0.0).astype(BF16), p)
    nrb = each(lambda p_: jnp.where(incl, p_[c:2 * c, 0:LANE], 0.0).astype(BF16), p)
    nkk = each(lambda p_: cat(jnp.where(strict, p_[0:c, LANE:2 * LANE], 0.0),
                              jnp.where(incl, p_[c:2 * c, LANE:2 * LANE], 0.0)).astype(BF16), p)
    z = each(lambda l_, s_: _dot(l_, s_.astype(BF16)), lhs, sts)
    w = each(lambda n_, v_: _dot(n_, bd(v_)), nkk, v)
    x = each(lambda z_, w_: z_[0:c] + w_[0:c], z, w)
    blk8 = (t2 >> 3) == (s2 >> 3)
    n0 = each(lambda n_: jnp.where(blk8, n_, zero), nab)
    inv = each(lambda n_: ident + n_.astype(F32), n0)
    pw = each(lambda n_: _dot(n_, bd(n_)), n0)
    inv = each(lambda i_, p_: i_ + _dot(i_.astype(BF16), bd(p_)), inv, pw)
    pw = each(lambda p_: _dot(p_.astype(BF16), bd(p_)), pw)
    inv = each(lambda i_, p_: i_ + _dot(i_.astype(BF16), bd(p_)), inv, pw)
    for lb in range(3, HEAD_SHIFT):
        new = ((t2 >> (lb + 1)) == (s2 >> (lb + 1))) & ((t2 >> lb) != (s2 >> lb))
        half = each(lambda i_, n_: _dot(i_.astype(BF16), bd(jnp.where(new, n_, zero))), inv, nab)
        inv = each(lambda i_, h_: i_ + _dot(h_.astype(BF16), bd(i_)), inv, half)
    x = each(lambda i_, x_: _dot(i_.astype(BF16), bd(x_)), inv, x)
    y = each(lambda z_, w_, n_, x_: z_[c:2 * c] + w_[c:2 * c] + _dot(n_, bd(x_)), z, w, nrb, x)
    gr = each(lambda t_, l_: jnp.exp(t_ - l_), l_tot, big_l)
    upd = each(lambda b_, k_, g_, x_, v_: _dot(cat(b_ * g_, k_ * g_).T.astype(BF16),
                                               cat(x_, v_).astype(BF16)), bb, kd, gr, x, v)
    g_col = each(lambda t_: jnp.sum(jnp.where(eye, jnp.exp(t_), 0.0), axis=1, keepdims=True), l_tot)
    st_new = each(lambda s_, g_, u_: s_ * g_ + jnp.where(blk, u_, 0.0), sts, g_col, upd)
    return y, st_new


def _scan_kernel(rev, npar, n_chunks, tok_ref, s0_ref, y_ref, st_out_ref, st_scr):
    ci = pl.program_id(2)

    @pl.when(ci == 0)
    def _():
        st_scr[...] = s0_ref[...]

    c = SCAN_CHUNK
    ti = _iota((c, c), 0)
    si = _iota((c, c), 1)
    tri = ((si >= ti) if rev else (si <= ti)).astype(BF16)
    t2 = _iota((c, LANE), 0)
    s2 = _iota((c, LANE), 1) & (c - 1)
    strict = (s2 > t2) if rev else (s2 < t2)
    incl = (s2 >= t2) if rev else (s2 <= t2)
    bi = _iota((LANE, LANE), 0)
    bj = _iota((LANE, LANE), 1)
    blk = (bi >> HEAD_SHIFT) == (bj >> HEAD_SHIFT)
    eye = bi == bj
    head0 = _iota((c, LANE), 1) < A_HEAD
    consts = (tri, strict, incl, blk, eye, head0, t2, s2)
    lanes = [slice(p * LANE, (p + 1) * LANE) for p in range(npar)]
    gw = npar * LANE
    toks = [tuple(tok_ref[:, f * gw + p * LANE:f * gw + (p + 1) * LANE] for f in range(SCAN_FIELDS))
            for p in range(npar)]
    ys, sts = _scan_chunks(rev, toks, [st_scr[p] for p in range(npar)], consts)
    for p, sl in enumerate(lanes):
        y_ref[:, sl] = ys[p]
        st_scr[p] = sts[p]

    @pl.when(ci == n_chunks - 1)
    def _():
        st_out_ref[...] = st_scr[...]


def rwkv_scan(packed, s0, rev):
    bsz, t, width = packed.shape
    aw = width // SCAN_FIELDS
    n_pairs = aw // LANE
    npar = scan_group_width(aw) // LANE
    n_chunks = t // SCAN_CHUNK
    if rev:
        tmap = lambda b, g, c: (b, n_chunks - 1 - c, g)
    else:
        tmap = lambda b, g, c: (b, c, g)
    tok = pl.BlockSpec((None, SCAN_CHUNK, npar * LANE), tmap)
    tok_in = pl.BlockSpec((None, SCAN_CHUNK, SCAN_FIELDS * npar * LANE), tmap)
    st = pl.BlockSpec((None, npar, LANE, LANE), lambda b, g, c: (b, g, 0, 0))
    return pl.pallas_call(
        functools.partial(_scan_kernel, rev, npar, n_chunks),
        grid=(bsz, n_pairs // npar, n_chunks),
        in_specs=[tok_in, st],
        out_specs=[tok, st],
        out_shape=[jax.ShapeDtypeStruct((bsz, t, aw), F32),
                   jax.ShapeDtypeStruct(s0.shape, F32)],
        scratch_shapes=[pltpu.VMEM((npar, LANE, LANE), F32)],
        compiler_params=_cparams("parallel", "parallel", "arbitrary"),
        name="rwkv_scan_bwd" if rev else "rwkv_scan_fwd",
    )(packed, s0)


def _rwkv_post_kernel(yf_ref, yb_ref, bonus_ref, g_ref, lnw_ref, lnb_ref, o_ref):
    y = yf_ref[...] + yb_ref[...]
    inv = 1.0 / A_HEAD
    mu = _seg_sum(y) * inv
    yc = y - mu
    var = _seg_sum(yc * yc) * inv
    yn = yc * lax.rsqrt(var + GN_EPS) * lnw_ref[...] + lnb_ref[...]
    o_ref[...] = ((yn + bonus_ref[...]) * g_ref[...]).astype(o_ref.dtype)


def rwkv_post(yf, yb, bonus, g, ln_w, ln_b, tt=256):
    n, aw = yf.shape
    tok = pl.BlockSpec((tt, aw), lambda i: (i, 0))
    par = pl.BlockSpec((1, aw), lambda i: (0, 0))
    return pl.pallas_call(
        _rwkv_post_kernel,
        grid=(n // tt,),
        in_specs=[tok, tok, tok, tok, par, par],
        out_specs=tok,
        out_shape=jax.ShapeDtypeStruct((n, aw), BF16),
        compiler_params=_cparams("parallel"),
        name="rwkv_post",
    )(yf, yb, bonus, g, ln_w, ln_b)


def round_up(x, m):
    return (x + m - 1) // m * m


ATT_SCALE = HEAD_DIM ** -0.5


def _ctx_attn_kernel(q_ref, k_ref, v_ref, sink_ref, o_ref):
    s = _dot3_nt(q_ref[...] * ATT_SCALE, k_ref[...])
    sk = sink_ref[0:1, 0:1]
    m = jnp.maximum(jnp.max(s, axis=-1, keepdims=True), sk)
    e = jnp.exp(s - m)
    den = jnp.sum(e, axis=-1, keepdims=True) + jnp.exp(sk - m)
    o = _dot(e.astype(BF16), v_ref[...].astype(BF16))
    o_ref[...] = (o / den).astype(o_ref.dtype)


def ctx_attention(u, bsz, t, q0, k0, v0, n_heads, group, sink):
    sink = jnp.full((n_heads,), NEG, F32) if sink is None else sink.astype(F32)
    sink = jnp.broadcast_to(sink[:, None, None], (n_heads, 1, LANE))
    return pl.pallas_call(
        _ctx_attn_kernel,
        grid=(bsz, n_heads),
        in_specs=[
            pl.BlockSpec((t, HEAD_DIM), lambda b, h: (b, q0 + h)),
            pl.BlockSpec((t, HEAD_DIM), lambda b, h: (b, k0 + h // group)),
            pl.BlockSpec((t, HEAD_DIM), lambda b, h: (b, v0 + h // group)),
            pl.BlockSpec((None, 1, LANE), lambda b, h: (h, 0, 0)),
        ],
        out_specs=pl.BlockSpec((t, HEAD_DIM), lambda b, h: (b, h)),
        out_shape=jax.ShapeDtypeStruct((bsz * t, n_heads * HEAD_DIM), BF16),
        compiler_params=_cparams("parallel", "parallel"),
        name="ctx_attention",
    )(u, u, u, sink)


def rope_tables(n_tok):
    half = HEAD_DIM // 2
    quarter = half // 2
    tok = jnp.arange(n_tok)
    row = (tok // GRID_W).astype(F32)
    col = (tok % GRID_W).astype(F32)
    inv = ROPE_THETA ** (-jnp.arange(quarter, dtype=F32) / quarter)
    ang_r = row[:, None] * inv[None]
    ang_c = col[:, None] * inv[None]
    cos = jnp.concatenate([jnp.cos(ang_r)] * 2 + [jnp.cos(ang_c)] * 2, axis=1)
    sr, sc = jnp.sin(ang_r), jnp.sin(ang_c)
    z = jnp.zeros_like(sr)
    sin_a = jnp.concatenate([-sr, z, -sc, z], axis=1)
    sin_b = jnp.concatenate([z, sr, z, sc], axis=1)
    return cos, sin_a, sin_b


def _rope_kernel(n_heads, x_ref, cos_ref, sa_ref, sb_ref, o_ref):
    q = HEAD_DIM // 4
    cos, sa, sb = cos_ref[...], sa_ref[...], sb_ref[...]
    for h in range(n_heads):
        sl = slice(h * HEAD_DIM, (h + 1) * HEAD_DIM)
        x = x_ref[:, sl]
        o_ref[:, sl] = x * cos + pltpu.roll(x, HEAD_DIM - q, 1) * sa + pltpu.roll(x, q, 1) * sb


def rope(u, bsz, n_tok, col0, n_heads, tt=256):
    cos, sin_a, sin_b = rope_tables(n_tok)
    nt = n_tok // tt
    hb = max(k for k in range(1, n_heads + 1) if n_heads % k == 0 and col0 % k == 0)
    width = hb * HEAD_DIM
    cb = col0 // hb
    tab = pl.BlockSpec((tt, HEAD_DIM), lambda b, i, j: (i, 0))
    return pl.pallas_call(
        functools.partial(_rope_kernel, hb),
        grid=(bsz, nt, n_heads // hb),
        in_specs=[pl.BlockSpec((tt, width), lambda b, i, j: (b * nt + i, cb + j)), tab, tab, tab],
        out_specs=pl.BlockSpec((tt, width), lambda b, i, j: (b * nt + i, j)),
        out_shape=jax.ShapeDtypeStruct((bsz * n_tok, n_heads * HEAD_DIM), F32),
        compiler_params=_cparams("parallel", "parallel", "parallel"),
        name="rope",
    )(u, cos, sin_a, sin_b)


def _win_attn_kernel(group, n_tok, sink_ref, q_ref, kp_ref, kc_ref, kn_ref, vp_ref, vc_ref, vn_ref,
                     ck_ref, cv_ref, o_ref):
    kv = pl.program_id(1)
    i = pl.program_id(2)
    rows = group * QBLK
    q = jnp.concatenate([q_ref[:, g * HEAD_DIM:(g + 1) * HEAD_DIM] for g in range(group)], axis=0)
    q = q * ATT_SCALE
    k_loc = jnp.concatenate([kp_ref[...], kc_ref[...], kn_ref[...]], axis=0)
    v_loc = jnp.concatenate([vp_ref[...], vc_ref[...], vn_ref[...]], axis=0).astype(BF16)
    s_loc = _dot3_nt(q, k_loc)
    qi = _iota(s_loc.shape, 0) & (QBLK - 1)
    kj = _iota(s_loc.shape, 1)
    pos = i * QBLK - WIN + kj
    ok = (kj - qi >= 0) & (kj - qi <= 2 * WIN) & (pos >= 0) & (pos < n_tok)
    s_loc = jnp.where(ok, s_loc, NEG)
    s_ctx = _dot3_nt(q, ck_ref[...])
    rg = _iota((rows, 1), 0) >> int(np.log2(QBLK))
    sk = jnp.zeros((rows, 1), F32)
    for g in range(group):
        sk = jnp.where(rg == g, sink_ref[kv * group + g], sk)
    m = jnp.maximum(jnp.maximum(jnp.max(s_loc, axis=-1, keepdims=True),
                                jnp.max(s_ctx, axis=-1, keepdims=True)), sk)
    e_loc = jnp.exp(s_loc - m)
    e_ctx = jnp.exp(s_ctx - m)
    den = (jnp.sum(e_loc, axis=-1, keepdims=True) + jnp.sum(e_ctx, axis=-1, keepdims=True)
           + jnp.exp(sk - m))
    o = (_dot(e_loc.astype(BF16), v_loc) + _dot(e_ctx.astype(BF16), cv_ref[...].astype(BF16))) / den
    for g in range(group):
        o_ref[:, g * HEAD_DIM:(g + 1) * HEAD_DIM] = o[g * QBLK:(g + 1) * QBLK].astype(o_ref.dtype)


def window_attention(qk_rot, u, vb0, bsz, n_tok, n_kv, group, ck, cv, sink):
    assert WIN == QBLK
    nb = n_tok // QBLK
    nq = n_kv * group
    past = ck.shape[2]
    prv = lambda i: jnp.maximum(i - 1, 0)
    nxt = lambda i: jnp.minimum(i + 1, nb - 1)
    blk = lambda f, c0: pl.BlockSpec((QBLK, HEAD_DIM), lambda b, kv, i, s: (b * nb + f(i), c0 + kv))
    same = lambda i: i
    cache = pl.BlockSpec((None, None, past, HEAD_DIM), lambda b, kv, i, s: (b, kv, 0, 0))
    grid_spec = pltpu.PrefetchScalarGridSpec(
        num_scalar_prefetch=1,
        grid=(bsz, n_kv, nb),
        in_specs=[
            pl.BlockSpec((QBLK, group * HEAD_DIM), lambda b, kv, i, s: (b * nb + i, kv)),
            blk(prv, nq), blk(same, nq), blk(nxt, nq),
            blk(prv, vb0), blk(same, vb0), blk(nxt, vb0),
            cache, cache,
        ],
        out_specs=pl.BlockSpec((QBLK, group * HEAD_DIM), lambda b, kv, i, s: (b * nb + i, kv)),
    )
    return pl.pallas_call(
        functools.partial(_win_attn_kernel, group, n_tok),
        grid_spec=grid_spec,
        out_shape=jax.ShapeDtypeStruct((bsz * n_tok, nq * HEAD_DIM), BF16),
        compiler_params=_cparams("parallel", "parallel", "parallel"),
        name="window_attention",
    )(sink.astype(F32), qk_rot, qk_rot, qk_rot, qk_rot, u, u, u, ck, cv)


NB_QROWS = 4
NB_QTOK = NB_QROWS * GRID_W
NB_KTOK = 3 * NB_QTOK


def nb_tables(rpb, rows):
    kh = min(NB_ROWS, rows)
    ql = np.arange(NB_QTOK)
    kl = np.arange(NB_KTOK)
    r_rel, c = ql // GRID_W, ql % GRID_W
    kr_rel, kc = kl // GRID_W - NB_QROWS, kl % GRID_W
    n_heads = rpb.shape[0]
    gcol = np.arange(GRID_W)
    col_idx = np.clip(gcol[None, :] - gcol[:, None] + NB_COLS - 1, 0, 2 * NB_COLS - 2)
    t_col = jnp.take(rpb.astype(F32), jnp.asarray(col_idx.reshape(-1)), axis=2)
    t_col = t_col.reshape(n_heads, 2 * NB_ROWS - 1, GRID_W * GRID_W)
    qr = np.arange(NB_QROWS)
    krr = np.arange(3 * NB_QROWS) - NB_QROWS
    row_idx = np.clip(krr[None, :] - qr[:, None] + NB_ROWS - 1, 0, 2 * NB_ROWS - 2)
    bias = jnp.take(t_col, jnp.asarray(row_idx.reshape(-1)), axis=1)
    bias = bias.reshape(n_heads, NB_QROWS, 3 * NB_QROWS, GRID_W, GRID_W)
    bias = jnp.transpose(bias, (0, 1, 3, 2, 4)).reshape(n_heads, NB_QTOK, NB_KTOK)
    win_start = np.clip(c - NB_COLS // 2, 0, GRID_W - NB_COLS)
    col_ok = (kc[None, :] >= win_start[:, None]) & (kc[None, :] < win_start[:, None] + NB_COLS)
    masks = []
    for j in range(rows // NB_QROWS):
        r = j * NB_QROWS + r_rel
        kr = j * NB_QROWS + kr_rel
        row_start = np.clip(r - kh // 2, 0, rows - kh)
        row_ok = (kr[None, :] >= row_start[:, None]) & (kr[None, :] < row_start[:, None] + kh)
        masks.append(row_ok & col_ok)
    return bias, jnp.asarray(np.stack(masks).astype(np.float32))


def _nb_attn_kernel(q_ref, kp_ref, kc_ref, kn_ref, vp_ref, vc_ref, vn_ref, ck_ref, cv_ref,
                    bias_ref, mask_ref, o_ref):
    q = q_ref[...] * ATT_SCALE
    k_loc = jnp.concatenate([kp_ref[...], kc_ref[...], kn_ref[...]], axis=0)
    v_loc = jnp.concatenate([vp_ref[...], vc_ref[...], vn_ref[...]], axis=0).astype(BF16)
    s_loc = jnp.where(mask_ref[...] > 0.0, _dot3_nt(q, k_loc) + bias_ref[...], NEG)
    s_ctx = _dot3_nt(q, ck_ref[...])
    m = jnp.maximum(jnp.max(s_loc, axis=-1, keepdims=True), jnp.max(s_ctx, axis=-1, keepdims=True))
    e_loc = jnp.exp(s_loc - m)
    e_ctx = jnp.exp(s_ctx - m)
    den = jnp.sum(e_loc, axis=-1, keepdims=True) + jnp.sum(e_ctx, axis=-1, keepdims=True)
    o = _dot(e_loc.astype(BF16), v_loc) + _dot(e_ctx.astype(BF16), cv_ref[...].astype(BF16))
    o_ref[...] = (o / den).astype(o_ref.dtype)


def neighbourhood_attention(u, cols, bsz, n_tok, n_heads, ck, cv, rpb):
    q0, k0, v0 = cols
    rows = n_tok // GRID_W
    assert rows % NB_QROWS == 0 and rows >= NB_ROWS
    nj = rows // NB_QROWS
    past = ck.shape[2]
    bias, mask = nb_tables(rpb, rows)
    prv = lambda j: jnp.maximum(j - 1, 0)
    nxt = lambda j: jnp.minimum(j + 1, nj - 1)
    same = lambda j: j
    blk = lambda f, c0: pl.BlockSpec((NB_QTOK, HEAD_DIM), lambda b, h, j: (b * nj + f(j), c0 + h))
    cache = pl.BlockSpec((None, None, past, HEAD_DIM), lambda b, h, j: (b, h, 0, 0))
    return pl.pallas_call(
        _nb_attn_kernel,
        grid=(bsz, n_heads, nj),
        in_specs=[
            blk(same, q0),
            blk(prv, k0), blk(same, k0), blk(nxt, k0),
            blk(prv, v0), blk(same, v0), blk(nxt, v0),
            cache, cache,
            pl.BlockSpec((None, NB_QTOK, NB_KTOK), lambda b, h, j: (h, 0, 0)),
            pl.BlockSpec((None, NB_QTOK, NB_KTOK), lambda b, h, j: (j, 0, 0)),
        ],
        out_specs=pl.BlockSpec((NB_QTOK, HEAD_DIM), lambda b, h, j: (b * nj + j, h)),
        out_shape=jax.ShapeDtypeStruct((bsz * n_tok, n_heads * HEAD_DIM), BF16),
        compiler_params=_cparams("parallel", "parallel", "parallel"),
        name="neighbourhood_attention",
    )(u, u, u, u, u, u, u, ck, cv, bias, mask)


def _merge_kernel(ya_ref, yb_ref, yc_ref, wa_ref, wb_ref, wc_ref, ga_ref, gb_ref, gc_ref, o_ref):
    m = _sigmoid(ga_ref[...]) * _dot(ya_ref[...], wa_ref[...])
    m = m + _sigmoid(gb_ref[...]) * _dot(yb_ref[...], wb_ref[...])
    m = m + _sigmoid(gc_ref[...]) * _dot(yc_ref[...], wc_ref[...])
    o_ref[...] = m.astype(o_ref.dtype)


def merge_branches(ya, yb, yc, wa, wb, wc, u, gate_col0, tm=512, tn=512):
    n, aw = ya.shape
    bw, cw = wb.shape[0], wc.shape[0]
    d = wa.shape[1]
    tm = _pick(n, (tm, 256, 128))
    tn = _pick(d, (tn, 256, 128))
    assert gate_col0 % tn == 0
    g0 = gate_col0 // tn
    nd = d // tn
    gate = lambda gi: pl.BlockSpec((tm, tn), lambda i, j: (i, g0 + gi * nd + j))
    return pl.pallas_call(
        _merge_kernel,
        grid=(n // tm, nd),
        in_specs=[
            pl.BlockSpec((tm, aw), lambda i, j: (i, 0)),
            pl.BlockSpec((tm, bw), lambda i, j: (i, 0)),
            pl.BlockSpec((tm, cw), lambda i, j: (i, 0)),
            pl.BlockSpec((aw, tn), lambda i, j: (0, j)),
            pl.BlockSpec((bw, tn), lambda i, j: (0, j)),
            pl.BlockSpec((cw, tn), lambda i, j: (0, j)),
            gate(0), gate(1), gate(2),
        ],
        out_specs=pl.BlockSpec((tm, tn), lambda i, j: (i, j)),
        out_shape=jax.ShapeDtypeStruct((n, d), BF16),
        compiler_params=_cparams("parallel", "parallel"),
        name="merge_branches",
    )(ya, yb, yc, wa, wb, wc, u, u, u)


def _post_norm(alpha, x, gate, y, g, b):
    r = alpha * x + gate * y
    return _ln_rows(r) * g + b


def _outproj_ln_kernel(alpha, nk, m_ref, w_ref, x_ref, gate_ref, g_ref, b_ref, o_ref):
    k = pl.program_id(1)

    @pl.when(k == 0)
    def _():
        o_ref[...] = _dot(m_ref[...], w_ref[...])

    @pl.when(k > 0)
    def _():
        o_ref[...] += _dot(m_ref[...], w_ref[...])

    @pl.when(k == nk - 1)
    def _():
        rows = o_ref.shape[0]
        step = min(rows, 128)
        for r0 in range(0, rows, step):
            sl = slice(r0, r0 + step)
            o_ref[sl, :] = _post_norm(alpha, x_ref[sl, :], gate_ref[...], o_ref[sl, :],
                                      g_ref[...], b_ref[...])


def _tile_mod_spec(mod, d, tm, seq_len):
    if mod.shape[0] == 1:
        return pl.BlockSpec((None, 1, d), lambda i, *_: (0, 0, 0))
    return pl.BlockSpec((None, 1, d), lambda i, *_: (i * tm // seq_len, 0, 0))


def outproj_postnorm(m, w_out, x, gate, ln_g, ln_b, alpha, seq_len, tm=512, tk=512):
    n, d = x.shape
    kdim = m.shape[1]
    tm = _pick(n if gate.shape[0] == 1 else seq_len, (tm, 256, 128))
    tk = _pick(kdim, (tk, 256, 128))
    nk = kdim // tk
    vec = pl.BlockSpec((1, d), lambda i, k: (0, 0))
    return pl.pallas_call(
        functools.partial(_outproj_ln_kernel, alpha, nk),
        grid=(n // tm, nk),
        in_specs=[
            pl.BlockSpec((tm, tk), lambda i, k: (i, k)),
            pl.BlockSpec((tk, d), lambda i, k: (k, 0)),
            pl.BlockSpec((tm, d), lambda i, k: (i, 0)),
            _tile_mod_spec(gate, d, tm, seq_len),
            vec, vec,
        ],
        out_specs=pl.BlockSpec((tm, d), lambda i, k: (i, 0)),
        out_shape=jax.ShapeDtypeStruct((n, d), F32),
        compiler_params=_cparams("parallel", "arbitrary"),
        name="outproj_postnorm",
    )(m, w_out, x, gate, ln_g, ln_b)


def _select_kernel(cap, n, aff_ref, mask_ref):
    bits = lax.bitcast_convert_type(aff_ref[...], jnp.int32)
    n_exp = bits.shape[0]
    capf = jnp.float32(cap)

    def count(pred):
        return jnp.sum(pred.astype(F32), axis=1, keepdims=True)

    def value_step(_, carry):
        lo, hi = carry
        mid = lo + ((hi - lo + 1) >> 1)
        ok = count(bits >= mid) >= capf
        return jnp.where(ok, mid, lo), jnp.where(ok, hi, mid - 1)

    lo0 = jnp.zeros((n_exp, 1), jnp.int32)
    hi0 = jnp.full((n_exp, 1), 0x7F800000, jnp.int32)
    thr, _ = lax.fori_loop(0, 32, value_step, (lo0, hi0))
    gt = bits > thr
    eq = bits == thr
    need = capf - count(gt)
    idx = _iota(bits.shape, 1)

    def index_step(_, carry):
        lo, hi = carry
        mid = (lo + hi) >> 1
        ok = count(eq & (idx < mid)) >= need
        return jnp.where(ok, lo, mid), jnp.where(ok, mid, hi)

    _, bound = lax.fori_loop(0, int(np.ceil(np.log2(n))) + 1, index_step,
                             (jnp.zeros((n_exp, 1), jnp.int32), jnp.full((n_exp, 1), n, jnp.int32)))
    mask_ref[...] = (gt | (eq & (idx < bound))).astype(jnp.int32)


def expert_choice_mask(aff_t, cap):
    n_exp, n = aff_t.shape
    return pl.pallas_call(
        functools.partial(_select_kernel, cap, n),
        out_shape=jax.ShapeDtypeStruct((n_exp, n), jnp.int32),
        compiler_params=pltpu.CompilerParams(vmem_limit_bytes=VMEM_LIMIT_BYTES),
        name="expert_choice_mask",
    )(aff_t)


def _count_le(sorted_vals, x):
    return jnp.sum((sorted_vals[None, :] <= x[:, None]).astype(jnp.int32), axis=1)


def _visit_list(nvis, first_blk, n_work):
    na, nb = nvis.shape
    flat = nvis.reshape(-1)
    off_end = jnp.cumsum(flat)
    off_start = off_end - flat
    total = off_end[-1]
    w = jnp.minimum(jnp.arange(n_work, dtype=jnp.int32), total - 1)
    idx = jnp.minimum(_count_le(off_end, w), na * nb - 1)
    blk = first_blk.reshape(-1)[idx] + (w - off_start[idx])
    valid = (jnp.arange(n_work) < total).astype(jnp.int32)
    return idx // nb, idx % nb, blk.astype(jnp.int32), valid


MOE_GROUP = 4
NO_SLOT = 1 << 28


def _group_items(key, valid, n_keys, group, n_steps):
    n_items = key.shape[0]
    cnt = jnp.zeros((n_keys,), jnp.int32).at[key].add(valid)
    start = jnp.cumsum(cnt) - cnt
    per_key = (cnt + group - 1) // group
    s_end = jnp.cumsum(per_key)
    s_start = s_end - per_key
    total = s_end[-1]
    step = jnp.arange(n_steps, dtype=jnp.int32)
    real = step < total
    sc = jnp.minimum(step, total - 1)
    k = jnp.minimum(_count_le(s_end, sc), n_keys - 1)
    j = sc - s_start[k]
    within = j[None, :] * group + jnp.arange(group, dtype=jnp.int32)[:, None]
    ok = (within < cnt[k][None, :]) & real[None, :]
    item = jnp.clip(start[k][None, :] + jnp.minimum(within, cnt[k][None, :] - 1), 0, n_items - 1)
    first = ((j == 0) & real).astype(jnp.int32)
    last = ((j == per_key[k] - 1) & real).astype(jnp.int32)
    return item, ok.astype(jnp.int32), first, last


def routing_plan(mask, cap):
    n_exp, n = mask.shape
    nsb = cap // SLOT_BLK
    pos = jnp.cumsum(mask, axis=1) - mask
    posm = jnp.where(mask > 0, pos, -1).astype(jnp.int32)

    def tile_visits(tile):
        cnt = mask.reshape(n_exp, n // tile, tile).sum(-1)
        cend = jnp.cumsum(cnt, axis=1)
        lo = jnp.minimum((cend - cnt) // SLOT_BLK, nsb - 1)
        return jnp.where(cnt > 0, (cend - 1) // SLOT_BLK - lo + 1, 0), lo

    n_blk = n_exp * nsb
    grp = MOE_GROUP
    nvis, sb_lo = tile_visits(ROUTE_TILE)
    n_work = n_exp * (nsb + n // ROUTE_TILE)
    e, t, sb, valid = _visit_list(nvis, sb_lo, n_work)
    blk = e * nsb + sb
    item, ok, first, last = _group_items(blk, valid, n_blk, grp, n_work // grp + n_blk + 1)
    blk_g = blk[item[0]]
    dispatch = (blk_g // nsb, blk_g, first, last, t[item], jnp.where(ok > 0, sb[item] * SLOT_BLK, NO_SLOT))
    nvis, sb_lo = tile_visits(TOK_TILE)
    nt = n // TOK_TILE
    n_work = n_exp * (nsb + nt)
    nvis_t = nvis.T.at[:, 0].max(1)
    t2, e2, sb2, valid2 = _visit_list(nvis_t, sb_lo.T, n_work + nt)
    item, ok, first, last = _group_items(t2, valid2, nt, grp, (n_work + nt) // grp + nt + 1)
    combine = (t2[item[0]], first, last, e2[item], e2[item] * nsb + sb2[item],
               jnp.where(ok > 0, sb2[item] * SLOT_BLK, NO_SLOT))
    return posm, dispatch, combine


def _route_kernel(group, e_ref, blk_ref, first_ref, last_ref, tile_ref, base_ref, *refs):
    posm_refs, aff_refs = refs[:group], refs[group:2 * group]
    o_ref, acc_ref = refs[2 * group:]
    w = pl.program_id(0)
    rows = _iota((SLOT_BLK, ROUTE_TILE), 0)
    tot = None
    for q in range(group):
        onehot = ((rows + base_ref[q, w]) == posm_refs[q][...]).astype(BF16)
        part = _dot_exact_lhs(onehot, aff_refs[q][...])
        tot = part if tot is None else tot + part

    @pl.when(first_ref[w] == 1)
    def _():
        acc_ref[...] = tot

    @pl.when(first_ref[w] == 0)
    def _():
        acc_ref[...] += tot

    @pl.when(last_ref[w] == 1)
    def _():
        o_ref[...] = acc_ref[...]


def moe_route(aff, posm, plan, cap):
    n = aff.shape[0]
    n_exp = posm.shape[0]
    grp = MOE_GROUP
    n_steps = plan[0].shape[0]
    pos_spec = lambda q: pl.BlockSpec((None, 1, ROUTE_TILE), lambda w, e, b, f, l, t, s: (e[w], 0, t[q, w]))
    aff_spec = lambda q: pl.BlockSpec((ROUTE_TILE, LANE), lambda w, e, b, f, l, t, s: (t[q, w], 0))
    grid_spec = pltpu.PrefetchScalarGridSpec(
        num_scalar_prefetch=6,
        grid=(n_steps,),
        in_specs=[pos_spec(q) for q in range(grp)] + [aff_spec(q) for q in range(grp)],
        out_specs=pl.BlockSpec((SLOT_BLK, LANE), lambda w, e, b, f, l, t, s: (b[w], 0)),
        scratch_shapes=[pltpu.VMEM((SLOT_BLK, LANE), F32)],
    )
    posm3 = posm.reshape(n_exp, 1, n)
    return pl.pallas_call(
        functools.partial(_route_kernel, grp),
        grid_spec=grid_spec,
        out_shape=jax.ShapeDtypeStruct((n_exp * cap, LANE), F32),
        compiler_params=_cparams("arbitrary"),
        name="moe_route",
    )(*plan, *([posm3] * grp), *([aff] * grp))


GATHER_ROWS = 256


def _gather_kernel(idx_ref, tok_hbm, o_ref, buf, sem):
    base = pl.program_id(0) * GATHER_ROWS

    def row_copy(r):
        return pltpu.make_async_copy(tok_hbm.at[pl.ds(idx_ref[base + r], 1), :],
                                     buf.at[pl.ds(r, 1), :], sem)

    def start(r, carry):
        row_copy(r).start()
        return carry

    def wait(r, carry):
        row_copy(r).wait()
        return carry

    lax.fori_loop(0, GATHER_ROWS, start, 0, unroll=8)
    lax.fori_loop(0, GATHER_ROWS, wait, 0, unroll=8)
    o_ref[...] = buf[...].astype(o_ref.dtype)


def moe_gather(tokens, idx):
    n, d = tokens.shape
    rows = idx.shape[0]
    assert rows % GATHER_ROWS == 0
    grid_spec = pltpu.PrefetchScalarGridSpec(
        num_scalar_prefetch=1,
        grid=(rows // GATHER_ROWS,),
        in_specs=[pl.BlockSpec(memory_space=pl.ANY)],
        out_specs=pl.BlockSpec((GATHER_ROWS, d), lambda i, idx_ref: (i, 0)),
        scratch_shapes=[pltpu.VMEM((GATHER_ROWS, d), F32), pltpu.SemaphoreType.DMA(())],
    )
    return pl.pallas_call(
        _gather_kernel,
        grid_spec=grid_spec,
        out_shape=jax.ShapeDtypeStruct((rows, d), BF16),
        compiler_params=_cparams("arbitrary"),
        name="moe_gather",
    )(idx, tokens)


def _ffn_up_kernel(x_ref, w1_ref, w3_ref, o_ref):
    x = x_ref[...]
    a = _dot(x, w1_ref[...])
    o_ref[...] = (a * _sigmoid(a) * _dot(x, w3_ref[...])).astype(o_ref.dtype)


def _ffn_down_kernel(h_ref, w2_ref, aff_ref, o_ref):
    aff = aff_ref[...]
    gval = jnp.sum(jnp.where(_iota(aff.shape, 1) == pl.program_id(0), aff, 0.0), axis=1, keepdims=True)
    o_ref[...] = (_dot(h_ref[...], w2_ref[...]) * gval).astype(o_ref.dtype)


def expert_ffn(xe, aff_rows, w1, w3, w2, layer, cap, tn=512):
    _, n_exp, d, ff = w1.shape
    tm = _pick(cap, (1024, 512, 256, 128))
    nm = cap // tm
    tf = _pick(ff, (tn, 256, 128))
    hid = pl.pallas_call(
        _ffn_up_kernel,
        grid=(n_exp, nm, ff // tf),
        in_specs=[
            pl.BlockSpec((tm, d), lambda e, i, j: (e * nm + i, 0)),
            pl.BlockSpec((None, None, d, tf), lambda e, i, j: (layer, e, 0, j)),
            pl.BlockSpec((None, None, d, tf), lambda e, i, j: (layer, e, 0, j)),
        ],
        out_specs=pl.BlockSpec((tm, tf), lambda e, i, j: (e * nm + i, j)),
        out_shape=jax.ShapeDtypeStruct((n_exp * cap, ff), BF16),
        compiler_params=_cparams("parallel", "parallel", "parallel"),
        name="expert_ffn_up",
    )(xe, w1, w3)
    td = _pick(d, (tn, 256, 128))
    return pl.pallas_call(
        _ffn_down_kernel,
        grid=(n_exp, nm, d // td),
        in_specs=[
            pl.BlockSpec((tm, ff), lambda e, i, j: (e * nm + i, 0)),
            pl.BlockSpec((None, None, ff, td), lambda e, i, j: (layer, e, 0, j)),
            pl.BlockSpec((tm, LANE), lambda e, i, j: (e * nm + i, 0)),
        ],
        out_specs=pl.BlockSpec((tm, td), lambda e, i, j: (e * nm + i, j)),
        out_shape=jax.ShapeDtypeStruct((n_exp * cap, d), BF16),
        compiler_params=_cparams("parallel", "parallel", "parallel"),
        name="expert_ffn_down",
    )(hid, w2, aff_rows)


def _combine_kernel(alpha, group, tile_ref, first_ref, last_ref, e_ref, blk_ref, base_ref,
                    posm_ref, *refs):
    ye_refs = refs[:group]
    x_ref, gate_ref, g_ref, b_ref, o_ref, acc_ref = refs[group:]
    w = pl.program_id(0)
    posm = posm_ref[...].astype(F32)
    lane = _iota(posm.shape, 1)
    cols = _iota((TOK_TILE, SLOT_BLK), 1)
    onehots = []
    for q in range(group):
        col = jnp.sum(jnp.where(lane == e_ref[q, w], posm, 0.0), axis=1, keepdims=True)
        onehots.append(((cols + base_ref[q, w]).astype(F32) == col).astype(BF16))
    tot = _dot(jnp.concatenate(onehots, axis=1),
               jnp.concatenate([r[...] for r in ye_refs], axis=0))

    @pl.when(first_ref[w] == 1)
    def _():
        acc_ref[...] = tot

    @pl.when(first_ref[w] == 0)
    def _():
        acc_ref[...] += tot

    @pl.when(last_ref[w] == 1)
    def _():
        o_ref[...] = _post_norm(alpha, x_ref[...], gate_ref[...], acc_ref[...], g_ref[...], b_ref[...])


def moe_combine_postnorm(ye, posm_t, plan, x, gate, ln_g, ln_b, alpha, seq_len):
    n, d = x.shape
    n_exp = posm_t.shape[1]
    grp = MOE_GROUP
    n_steps = plan[0].shape[0]
    assert seq_len % TOK_TILE == 0
    tile = lambda w, t, *_: (t[w], 0)
    if gate.shape[0] == 1:
        gate_map = lambda w, t, *_: (0, 0, 0)
    else:
        gate_map = lambda w, t, *_: (t[w] * TOK_TILE // seq_len, 0, 0)
    vec = pl.BlockSpec((1, d), lambda w, *_: (0, 0))
    ye_spec = lambda q: pl.BlockSpec((SLOT_BLK, d), lambda w, t, f, l, e, b, s: (b[q, w], 0))
    grid_spec = pltpu.PrefetchScalarGridSpec(
        num_scalar_prefetch=6,
        grid=(n_steps,),
        in_specs=([pl.BlockSpec((TOK_TILE, n_exp), tile)] + [ye_spec(q) for q in range(grp)]
                  + [pl.BlockSpec((TOK_TILE, d), tile), pl.BlockSpec((None, 1, d), gate_map), vec, vec]),
        out_specs=pl.BlockSpec((TOK_TILE, d), tile),
        scratch_shapes=[pltpu.VMEM((TOK_TILE, d), F32)],
    )
    return pl.pallas_call(
        functools.partial(_combine_kernel, alpha, grp),
        grid_spec=grid_spec,
        out_shape=jax.ShapeDtypeStruct((n, d), F32),
        compiler_params=_cparams("arbitrary"),
        name="moe_combine_postnorm",
    )(*plan, posm_t, *([ye] * grp), x, gate, ln_g, ln_b)


def ec_moe_postnorm(x, mods_shift, mods_scale, gate, router_w, w1, w3, w2, layer, ln_g, ln_b, alpha):
    bsz, t, d = x.shape
    n = bsz * t
    n_exp = router_w.shape[1]
    cap = max(1, CAP_FACTOR * n // n_exp)
    assert cap % SLOT_BLK == 0 and n % TOK_TILE == 0 and n % ROUTE_TILE == 0
    h, aff = ln_modulate_router(x, mods_shift, mods_scale, router_w)
    h = h.reshape(n, d)
    aff = aff.reshape(n, LANE)
    mask = expert_choice_mask(aff[:, :n_exp].T, cap)
    posm, dispatch, combine = routing_plan(mask, cap)
    aff_rows = moe_route(aff, posm, dispatch, cap)
    slot_token = jnp.clip(aff_rows[:, n_exp].astype(jnp.int32), 0, n - 1)
    xe = moe_gather(h, slot_token)
    ye = expert_ffn(xe, aff_rows, w1, w3, w2, layer, cap)
    out = moe_combine_postnorm(ye, posm.T, combine, x.reshape(n, d), gate, ln_g, ln_b, alpha, t)
    return out.reshape(bsz, t, d)


def prep_rwkv_params(rw, aw):
    a_cols = rw["mu"].shape[0]
    ap = round_up(a_cols, LANE)
    gp = ap - 3 * aw - 2 * LANE
    assert rw["w2"].shape[:2] == (2, LANE // 2) and rw["a2"].shape[:2] == (2, LANE // 2)
    assert gp >= rw["g2"].shape[0] and aw % LANE == 0
    row = lambda a: a.reshape(1, aw)
    return dict(
        mu=jnp.pad(rw["mu"], (0, ap - a_cols)).reshape(1, ap),
        w0=rw["w0"], w2=_hi_lo(rw["w2"].reshape(LANE, aw)), a0=rw["a0"],
        a2=_hi_lo(rw["a2"].reshape(LANE, aw)),
        g2=_hi_lo(jnp.pad(rw["g2"], ((0, gp - rw["g2"].shape[0]), (0, 0)))),
        k_k=row(rw["k_k"]), k_a=row(rw["k_a"]), r_k=row(rw["r_k"]),
        ln_w=row(rw["ln_w"]), ln_b=row(rw["ln_b"]), ap=ap, aw=aw)


def states_to_pairs(s):
    bsz, h = s.shape[:2]
    st = jnp.swapaxes(s, -1, -2).reshape(bsz, h // 2, 2, A_HEAD, A_HEAD)
    z = jnp.zeros_like(st[:, :, 0])
    top = jnp.concatenate([st[:, :, 0], z], axis=-1)
    bot = jnp.concatenate([z, st[:, :, 1]], axis=-1)
    return jnp.concatenate([top, bot], axis=-2)


def pairs_to_states(sp):
    bsz = sp.shape[0]
    st = jnp.stack([sp[:, :, :A_HEAD, :A_HEAD], sp[:, :, A_HEAD:, A_HEAD:]], axis=2)
    return jnp.swapaxes(st.reshape(bsz, -1, A_HEAD, A_HEAD), -1, -2)


def rwkv_branch(u, bsz, t, rwp, s0f, s0b):
    aw, ap = rwp["aw"], rwp["ap"]
    n = bsz * t
    pack_f, pack_b, bonus, g = rwkv_prep(u, t, rwp, aw, ap)
    sh = lambda a: a.reshape(bsz, t, SCAN_FIELDS * aw)
    yf, sf = rwkv_scan(sh(pack_f), states_to_pairs(s0f), False)
    yb, sb = rwkv_scan(sh(pack_b), states_to_pairs(s0b), True)
    ya = rwkv_post(yf.reshape(n, aw), yb.reshape(n, aw), bonus, g, rwp["ln_w"], rwp["ln_b"])
    return ya, pairs_to_states(sf), pairs_to_states(sb)


def _mod_rows(m):
    return [m[:, i][:, None, :] for i in range(6)]


def kernel(x_prompt, x_sample, cache_win_k, cache_win_v, cache_nb_k, cache_nb_v, state_rwkv_fwd, state_rwkv_bwd, c, c_ctx, ada_w, ada_b, w_in, rwkv_mu, rwkv_w0, rwkv_w2, rwkv_a0, rwkv_a2, rwkv_g2, rwkv_kk, rwkv_ka, rwkv_rk, rwkv_lnx_w, rwkv_lnx_b, win_sink, nb_rpb, w_br_a, w_br_b, w_br_c, w_out, ln1_g, ln1_b, ln2_g, ln2_b, router_w, exp_w1, exp_w3, exp_w2):
    depth, d, in_w = w_in.shape
    bsz, seq, _ = x_prompt.shape
    dbsz, dseq, _ = x_sample.shape
    aw = rwkv_w0.shape[-1]
    a_heads = aw // A_HEAD
    a_cols = rwkv_mu.shape[-1]
    ap = round_up(a_cols, LANE)
    n_b = win_sink.shape[-1]
    n_kv = cache_win_k.shape[3]
    group = n_b // n_kv
    n_c = nb_rpb.shape[1]
    bw, cw = n_b * HEAD_DIM, n_c * HEAD_DIM
    alpha = float((2 * depth) ** 0.25)
    assert in_w == a_cols + bw + 2 * n_kv * HEAD_DIM + 3 * cw + 3 * d

    qb0 = ap // LANE
    kb0 = qb0 + n_b
    vb0 = kb0 + n_kv
    qn0 = vb0 + n_kv
    kn0 = qn0 + n_c
    vn0 = kn0 + n_c
    gate_col0 = (vn0 + n_c) * LANE

    rows = 1 + dbsz
    cvecs = jnp.pad(jnp.concatenate([c_ctx[None], c], axis=0), ((0, round_up(rows, 16) - rows), (0, 0)))
    mods = ada_mods(cvecs, ada_w, ada_b).reshape(depth, -1, 6, d)

    xp, xs = x_prompt, x_sample
    np_tok, ns_tok = bsz * seq, dbsz * dseq
    win_k, win_v, nb_k, nb_v, st_f, st_b = [], [], [], [], [], []
    zero_state = jnp.zeros((bsz, a_heads, A_HEAD, A_HEAD), F32)
    w_cat = jnp.concatenate(
        [w_in[:, :, :a_cols].astype(BF16), jnp.zeros((depth, d, ap - a_cols), BF16),
         w_in[:, :, a_cols:].astype(BF16)], axis=2)
    w1, w3, w2 = exp_w1.astype(BF16), exp_w3.astype(BF16), exp_w2.astype(BF16)
    for l in range(depth):
        rwp = prep_rwkv_params(
            {"mu": rwkv_mu[l], "w0": rwkv_w0[l], "w2": rwkv_w2[l], "a0": rwkv_a0[l], "a2": rwkv_a2[l],
             "g2": rwkv_g2[l], "k_k": rwkv_kk[l], "k_a": rwkv_ka[l], "r_k": rwkv_rk[l],
             "ln_w": rwkv_lnx_w[l], "ln_b": rwkv_lnx_b[l]}, aw)
        wa, wb, wc = w_br_a[l].astype(BF16), w_br_b[l].astype(BF16), w_br_c[l].astype(BF16)
        wo = w_out[l].astype(BF16)
        g1, b1 = ln1_g[l].reshape(1, d), ln1_b[l].reshape(1, d)
        g2, b2 = ln2_g[l].reshape(1, d), ln2_b[l].reshape(1, d)

        sh1, sc1, gt1, sh2, sc2, gt2 = _mod_rows(mods[l, 0:1])
        h = ln_modulate(xp, sh1, sc1).reshape(np_tok, d)
        u = matmul(h, w_cat, l, name="in_proj")
        ya, s_f, s_b = rwkv_branch(u, bsz, seq, rwp, zero_state, zero_state)
        yb = ctx_attention(u, bsz, seq, qb0, kb0, vb0, n_b, group, win_sink[l])
        yc = ctx_attention(u, bsz, seq, qn0, kn0, vn0, n_c, 1, None)
        m = merge_branches(ya, yb, yc, wa, wb, wc, u, gate_col0)
        x1 = outproj_postnorm(m, wo, xp.reshape(np_tok, d), gt1, g1, b1, alpha, seq)
        xp = ec_moe_postnorm(x1.reshape(bsz, seq, d), sh2, sc2, gt2, router_w[l], w1, w3, w2, l, g2, b2, alpha)
        cols = lambda c0, nh: u[:, c0 * LANE:(c0 + nh) * LANE].reshape(bsz, seq, nh, HEAD_DIM)
        win_k.append(cols(kb0, n_kv))
        win_v.append(cols(vb0, n_kv))
        nb_k.append(cols(kn0, n_c))
        nb_v.append(cols(vn0, n_c))
        st_f.append(s_f)
        st_b.append(s_b)

        sh1, sc1, gt1, sh2, sc2, gt2 = _mod_rows(mods[l, 1:1 + dbsz])
        h = ln_modulate(xs, sh1, sc1).reshape(ns_tok, d)
        u = matmul(h, w_cat, l, name="in_proj")
        ya, _, _ = rwkv_branch(u, dbsz, dseq, rwp, state_rwkv_fwd[:, l], state_rwkv_bwd[:, l])
        qk_rot = rope(u, dbsz, dseq, qb0, n_b + n_kv)
        heads_first = lambda a: jnp.swapaxes(a[:, l], 1, 2)
        yb = window_attention(qk_rot, u, vb0, dbsz, dseq, n_kv, group,
                              heads_first(cache_win_k), heads_first(cache_win_v), win_sink[l])
        yc = neighbourhood_attention(u, (qn0, kn0, vn0), dbsz, dseq, n_c,
                                     heads_first(cache_nb_k), heads_first(cache_nb_v), nb_rpb[l])
        m = merge_branches(ya, yb, yc, wa, wb, wc, u, gate_col0)
        x1 = outproj_postnorm(m, wo, xs.reshape(ns_tok, d), gt1, g1, b1, alpha, dseq)
        xs = ec_moe_postnorm(x1.reshape(dbsz, dseq, d), sh2, sc2, gt2, router_w[l], w1, w3, w2, l, g2, b2, alpha)

    stack = lambda xs_: jnp.stack(xs_, axis=1)
    return (xp, xs, stack(win_k), stack(win_v), stack(nb_k), stack(nb_v), stack(st_f), stack(st_b))
```

```python
import functools

import numpy as np
import jax
import jax.numpy as jnp
from jax import lax
from jax.experimental import pallas as pl
from jax.experimental.pallas import tpu as pltpu

F32 = jnp.float32
BF16 = jnp.bfloat16

HEAD_DIM = 128
A_HEAD = 64
GRID_W = 64
WIN = 128
QBLK = 128
NB_ROWS = 8
NB_COLS = 16
CAP_FACTOR = 2
ROPE_THETA = 10000.0
LN_EPS = 1e-6
GN_EPS = 64e-5
NEG = -1e30

LANE = 128
SUBLANE = 8
VMEM_LIMIT_BYTES = 56 * 1024 * 1024

SCAN_CHUNK = 64
SLOT_BLK = 128
TOK_TILE = 256
ROUTE_TILE = 256


def _cparams(*sem):
    return pltpu.CompilerParams(dimension_semantics=sem, vmem_limit_bytes=VMEM_LIMIT_BYTES)


def _dot(a, b):
    return jnp.dot(a, b, preferred_element_type=F32)


def _dot_nt(a, b):
    return lax.dot_general(a, b, (((1,), (1,)), ((), ())), preferred_element_type=F32)


def _split2(x):
    hi = x.astype(BF16)
    lo = (x - hi.astype(F32)).astype(BF16)
    return hi, lo


def _split3(x):
    hi = x.astype(BF16)
    r1 = x - hi.astype(F32)
    mid = r1.astype(BF16)
    lo = (r1 - mid.astype(F32)).astype(BF16)
    return hi, mid, lo


def _dot3(a, b):
    ah, al = _split2(a)
    bh, bl = _split2(b)
    return _dot(ah, bh) + (_dot(ah, bl) + _dot(al, bh))


def _hi_lo(w):
    hi, lo = _split2(w)
    return jnp.stack([hi, lo])


def _dot3_split_rhs(a, b_ref):
    ah, al = _split2(a)
    bh = b_ref[0]
    return _dot(ah, bh) + (_dot(ah, b_ref[1]) + _dot(al, bh))


def _dot3_nt(a, b):
    ah, al = _split2(a)
    bh, bl = _split2(b)
    return _dot_nt(ah, bh) + (_dot_nt(ah, bl) + _dot_nt(al, bh))


def _dot_exact_lhs(a_bf16, b):
    bh, bm, bl = _split3(b)
    return _dot(a_bf16, bh) + (_dot(a_bf16, bm) + _dot(a_bf16, bl))


def _sigmoid(x):
    return 1.0 / (1.0 + jnp.exp(-x))


def _iota(shape, dim):
    return lax.broadcasted_iota(jnp.int32, shape, dim)


def _ada_kernel(c_ref, w_ref, b_ref, o_ref):
    c = c_ref[...]
    a = c * _sigmoid(c)
    o_ref[...] = _dot3(a, w_ref[...]) + b_ref[...]


def ada_mods(cvecs, ada_w, ada_b, tn=512):
    depth, d, n6 = ada_w.shape
    rows = cvecs.shape[0]
    return pl.pallas_call(
        _ada_kernel,
        grid=(depth, n6 // tn),
        in_specs=[
            pl.BlockSpec((rows, d), lambda l, j: (0, 0)),
            pl.BlockSpec((None, d, tn), lambda l, j: (l, 0, j)),
            pl.BlockSpec((None, 1, tn), lambda l, j: (l, 0, j)),
        ],
        out_specs=pl.BlockSpec((None, rows, tn), lambda l, j: (l, 0, j)),
        out_shape=jax.ShapeDtypeStruct((depth, rows, n6), F32),
        compiler_params=_cparams("parallel", "parallel"),
        name="ada_mods",
    )(cvecs, ada_w, ada_b.reshape(depth, 1, n6))


def _ln_rows(x):
    mu = jnp.mean(x, axis=-1, keepdims=True)
    xc = x - mu
    var = jnp.mean(xc * xc, axis=-1, keepdims=True)
    return xc * lax.rsqrt(var + LN_EPS)


def _lnmod_kernel(x_ref, sh_ref, sc_ref, o_ref):
    y = _ln_rows(x_ref[...])
    o_ref[...] = (y * (1.0 + sc_ref[...]) + sh_ref[...]).astype(o_ref.dtype)


def _mod_spec(mod, d):
    if mod.shape[0] == 1:
        return pl.BlockSpec((None, 1, d), lambda b, i: (0, 0, 0))
    return pl.BlockSpec((None, 1, d), lambda b, i: (b, 0, 0))


def ln_modulate(x, shift, scale, tt=256):
    bsz, t, d = x.shape
    return pl.pallas_call(
        _lnmod_kernel,
        grid=(bsz, t // tt),
        in_specs=[
            pl.BlockSpec((None, tt, d), lambda b, i: (b, i, 0)),
            _mod_spec(shift, d),
            _mod_spec(scale, d),
        ],
        out_specs=pl.BlockSpec((None, tt, d), lambda b, i: (b, i, 0)),
        out_shape=jax.ShapeDtypeStruct((bsz, t, d), BF16),
        compiler_params=_cparams("parallel", "parallel"),
        name="ln_modulate",
    )(x, shift, scale)


def _lnmod_router_kernel(n_exp, seq_len, x_ref, sh_ref, sc_ref, rw_ref, h_ref, aff_ref):
    y = _ln_rows(x_ref[...])
    h = y * (1.0 + sc_ref[...]) + sh_ref[...]
    h_ref[...] = h
    logits = _dot3(h, rw_ref[...])
    lane = _iota(logits.shape, 1)
    logits = jnp.where(lane < n_exp, logits, NEG)
    m = jnp.max(logits, axis=-1, keepdims=True)
    e = jnp.exp(logits - m)
    aff = e / jnp.sum(e, axis=-1, keepdims=True)
    tt = aff.shape[0]
    tok = pl.program_id(0) * seq_len + pl.program_id(1) * tt + _iota(aff.shape, 0)
    aff_ref[...] = jnp.where(lane == n_exp, tok.astype(F32), aff)


def ln_modulate_router(x, shift, scale, router_w, tt=256):
    bsz, t, d = x.shape
    n_exp = router_w.shape[1]
    assert n_exp < LANE and bsz * t < (1 << 24)
    rw = jnp.pad(router_w, ((0, 0), (0, LANE - n_exp)))
    return pl.pallas_call(
        functools.partial(_lnmod_router_kernel, n_exp, t),
        grid=(bsz, t // tt),
        in_specs=[
            pl.BlockSpec((None, tt, d), lambda b, i: (b, i, 0)),
            _mod_spec(shift, d),
            _mod_spec(scale, d),
            pl.BlockSpec((d, LANE), lambda b, i: (0, 0)),
        ],
        out_specs=[
            pl.BlockSpec((None, tt, d), lambda b, i: (b, i, 0)),
            pl.BlockSpec((None, tt, LANE), lambda b, i: (b, i, 0)),
        ],
        out_shape=[
            jax.ShapeDtypeStruct((bsz, t, d), F32),
            jax.ShapeDtypeStruct((bsz, t, LANE), F32),
        ],
        compiler_params=_cparams("parallel", "parallel"),
        name="ln_modulate_router",
    )(x, shift, scale, rw)


def _mm_kernel(a_ref, b_ref, o_ref):
    o_ref[...] = _dot(a_ref[...], b_ref[...]).astype(o_ref.dtype)


def _pick(n, pref):
    for c in pref:
        if n % c == 0:
            return c
    return n


def matmul(a, b, layer, out_dtype=F32, tm=None, tn=None, name="matmul"):
    m, k = a.shape
    n = b.shape[2]
    tm = tm or _pick(m, (1024, 512, 256, 128))
    tn = tn or _pick(n, (768, 512, 256, 128))
    return pl.pallas_call(
        _mm_kernel,
        grid=(m // tm, n // tn),
        in_specs=[
            pl.BlockSpec((tm, k), lambda i, j: (i, 0)),
            pl.BlockSpec((None, k, tn), lambda i, j: (layer, 0, j)),
        ],
        out_specs=pl.BlockSpec((tm, tn), lambda i, j: (i, j)),
        out_shape=jax.ShapeDtypeStruct((m, n), out_dtype),
        compiler_params=_cparams("parallel", "parallel"),
        name=name,
    )(a, b)


HEAD_SHIFT = 6
assert (1 << HEAD_SHIFT) == A_HEAD and 2 * A_HEAD == LANE and SCAN_CHUNK == A_HEAD


def _head_block_ones():
    i = _iota((LANE, LANE), 0) >> HEAD_SHIFT
    j = _iota((LANE, LANE), 1) >> HEAD_SHIFT
    return (i == j).astype(BF16)


def _seg_sum(x):
    bd = _head_block_ones()
    outs = []
    for c in range(x.shape[1] // LANE):
        hi, mid, lo = _split3(x[:, c * LANE:(c + 1) * LANE])
        outs.append(_dot(hi, bd) + (_dot(mid, bd) + _dot(lo, bd)))
    return jnp.concatenate(outs, axis=1)


SCAN_FIELDS = 6


def _rwkv_prep_kernel(aw, gp, gw, tiles_per_seq,
                      x_ref, xp_ref, xn_ref, mu_ref, w0_ref, w2_ref, a0_ref, a2_ref, g2_ref,
                      kkp_ref, kap_ref, rkp_ref,
                      packf_o, packb_o, bonus_o, g_o):
    i = pl.program_id(0)
    x = x_ref[...]
    tt = x.shape[0]
    row = _iota(x.shape, 0)
    pos = i % tiles_per_seq
    prev_row = jnp.where(pos == 0, 0.0, xp_ref[SUBLANE - 1:SUBLANE, :])
    next_row = jnp.where(pos == tiles_per_seq - 1, 0.0, xn_ref[0:1, :])
    prev = jnp.where(row == 0, prev_row, pltpu.roll(x, 1, 0))
    nxt = jnp.where(row == tt - 1, next_row, pltpu.roll(x, tt - 1, 0))
    xs = x + mu_ref[...] * (0.5 * (prev + nxt) - x)

    r = xs[:, 0:aw]
    k = xs[:, aw:2 * aw]
    v = xs[:, 2 * aw:3 * aw]
    o = 3 * aw
    wlo = jnp.tanh(xs[:, o:o + LANE])
    alo = xs[:, o + LANE:o + 2 * LANE]
    glo = _sigmoid(xs[:, o + 2 * LANE:o + 2 * LANE + gp])
    lane = _iota((1, LANE), 1)
    exp_mhalf = float(np.exp(-0.5))

    g_o[...] = _dot3_split_rhs(glo, g2_ref)

    kk = k * kkp_ref[...]
    nrm = jnp.sqrt(_seg_sum(kk * kk))
    kk = kk / jnp.maximum(nrm, 1e-12)
    bonus_o[...] = _seg_sum(r * k * rkp_ref[...]) * v

    for d, pack_o in enumerate((packf_o, packb_o)):
        sel = ((lane >> HEAD_SHIFT) == d).astype(F32)
        wl = w0_ref[d:d + 1, :] + _dot3_split_rhs(wlo * sel, w2_ref)
        lw = -_sigmoid(wl) * exp_mhalf
        a = _sigmoid(a0_ref[d:d + 1, :] + _dot3_split_rhs(alo * sel, a2_ref))
        fields = (lw, kk, kk * a, k * (1.0 + (a - 1.0) * kap_ref[...]), r, v)
        for g in range(aw // gw):
            for f, arr in enumerate(fields):
                c0 = (g * SCAN_FIELDS + f) * gw
                pack_o[:, c0:c0 + gw] = arr[:, g * gw:(g + 1) * gw]


def scan_group_width(aw, max_pairs=6):
    n_pairs = aw // LANE
    return LANE * max(p for p in range(1, max_pairs + 1) if n_pairs % p == 0)


def rwkv_prep(u, seq_len, rw, aw, ap, tt=128):
    n = u.shape[0]
    gw = scan_group_width(aw)
    gp = ap - 3 * aw - 2 * LANE
    n8 = n // SUBLANE
    tpb = tt // SUBLANE
    full = lambda shape: pl.BlockSpec(shape, lambda i: (0,) * len(shape))
    tok = pl.BlockSpec((tt, aw), lambda i: (i, 0))
    pack = pl.BlockSpec((tt, SCAN_FIELDS * aw), lambda i: (i, 0))
    pack_shape = jax.ShapeDtypeStruct((n, SCAN_FIELDS * aw), F32)
    outs = pl.pallas_call(
        functools.partial(_rwkv_prep_kernel, aw, gp, gw, seq_len // tt),
        grid=(n // tt,),
        in_specs=[
            pl.BlockSpec((tt, ap), lambda i: (i, 0)),
            pl.BlockSpec((SUBLANE, ap), lambda i: (jnp.maximum(i * tpb - 1, 0), 0)),
            pl.BlockSpec((SUBLANE, ap), lambda i: (jnp.minimum((i + 1) * tpb, n8 - 1), 0)),
            full((1, ap)), full((2, aw)), full((2, LANE, aw)), full((2, aw)), full((2, LANE, aw)),
            full((2, gp, aw)), full((1, aw)), full((1, aw)), full((1, aw)),
        ],
        out_specs=[pack, pack, tok, tok],
        out_shape=[pack_shape, pack_shape] + [jax.ShapeDtypeStruct((n, aw), F32)] * 2,
        compiler_params=_cparams("parallel"),
        name="rwkv_prep",
    )(u, u, u, rw["mu"], rw["w0"], rw["w2"], rw["a0"], rw["a2"], rw["g2"],
      rw["k_k"], rw["k_a"], rw["r_k"])
    return outs


def _scan_chunks(rev, toks, sts, consts):
    tri, strict, incl, blk, eye, head0, t2, s2 = consts
    c = SCAN_CHUNK
    ident = (t2 == s2).astype(F32)
    zero = jnp.zeros((), BF16)
    cat = lambda *xs: jnp.concatenate(xs, axis=0)

    def each(f, *lists):
        return [f(*args) for args in zip(*lists)]

    def bd(x):
        xb = x.astype(BF16)
        return jnp.where(blk, cat(xb, xb), zero)

    lw, kk, bb, kd, r, v = (list(t) for t in zip(*toks))
    big_l = each(lambda a: _dot_exact_lhs(tri, a), lw)
    l_tot = each(lambda a: a[0:1, :] if rev else a[c - 1:c, :], big_l)
    lhs = each(lambda k_, r_, l_, w_: cat(-k_ * jnp.exp(l_ - w_), r_ * jnp.exp(l_)).astype(BF16),
               kk, r, big_l, lw)
    gi = each(lambda l_: jnp.exp(-l_), big_l)
    bt = each(lambda b_, g_: (b_ * g_).astype(BF16), bb, gi)
    kt = each(lambda k_, g_: (k_ * g_).astype(BF16), kd, gi)
    rhs = each(lambda b_, k_: cat(jnp.where(head0, b_, zero), jnp.where(head0, zero, b_),
                                  jnp.where(head0, k_, zero), jnp.where(head0, zero, k_)), bt, kt)
    p = each(_dot_nt, lhs, rhs)
    nab = each(lambda p_: jnp.where(strict, p_[0:c, 0:LANE], 0.0).astype(BF16), p)
    nrb = each(lambda p_: jnp.where(incl, p_[c:2 * c, 0:LANE], 0.0).astype(BF16), p)
    nkk = each(lambda p_: cat(jnp.where(strict, p_[0:c, LANE:2 * LANE], 0.0),
                              jnp.where(incl, p_[c:2 * c, LANE:2 * LANE], 0.0)).astype(BF16), p)
    z = each(lambda l_, s_: _dot(l_, s_.astype(BF16)), lhs, sts)
    w = each(lambda n_, v_: _dot(n_, bd(v_)), nkk, v)
    x = each(lambda z_, w_: z_[0:c] + w_[0:c], z, w)
    blk8 = (t2 >> 3) == (s2 >> 3)
    n0 = each(lambda n_: jnp.where(blk8, n_, zero), nab)
    inv = each(lambda n_: ident + n_.astype(F32), n0)
    pw = each(lambda n_: _dot(n_, bd(n_)), n0)
    inv = each(lambda i_, p_: i_ + _dot(i_.astype(BF16), bd(p_)), inv, pw)
    pw = each(lambda p_: _dot(p_.astype(BF16), bd(p_)), pw)
    inv = each(lambda i_, p_: i_ + _dot(i_.astype(BF16), bd(p_)), inv, pw)
    for lb in range(3, HEAD_SHIFT):
        new = ((t2 >> (lb + 1)) == (s2 >> (lb + 1))) & ((t2 >> lb) != (s2 >> lb))
        half = each(lambda i_, n_: _dot(i_.astype(BF16), bd(jnp.where(new, n_, zero))), inv, nab)
        inv = each(lambda i_, h_: i_ + _dot(h_.astype(BF16), bd(i_)), inv, half)
    x = each(lambda i_, x_: _dot(i_.astype(BF16), bd(x_)), inv, x)
    y = each(lambda z_, w_, n_, x_: z_[c:2 * c] + w_[c:2 * c] + _dot(n_, bd(x_)), z, w, nrb, x)
    gr = each(lambda t_, l_: jnp.exp(t_ - l_), l_tot, big_l)
    upd = each(lambda b_, k_, g_, x_, v_: _dot(cat(b_ * g_, k_ * g_).T.astype(BF16),
                                               cat(x_, v_).astype(BF16)), bb, kd, gr, x, v)
    g_col = each(lambda t_: jnp.sum(jnp.where(eye, jnp.exp(t_), 0.0), axis=1, keepdims=True), l_tot)
    st_new = each(lambda s_, g_, u_: s_ * g_ + jnp.where(blk, u_, 0.0), sts, g_col, upd)
    return y, st_new


def _scan_kernel(rev, npar, n_chunks, tok_ref, s0_ref, y_ref, st_out_ref, st_scr):
    ci = pl.program_id(2)

    @pl.when(ci == 0)
    def _():
        st_scr[...] = s0_ref[...]

    c = SCAN_CHUNK
    ti = _iota((c, c), 0)
    si = _iota((c, c), 1)
    tri = ((si >= ti) if rev else (si <= ti)).astype(BF16)
    t2 = _iota((c, LANE), 0)
    s2 = _iota((c, LANE), 1) & (c - 1)
    strict = (s2 > t2) if rev else (s2 < t2)
    incl = (s2 >= t2) if rev else (s2 <= t2)
    bi = _iota((LANE, LANE), 0)
    bj = _iota((LANE, LANE), 1)
    blk = (bi >> HEAD_SHIFT) == (bj >> HEAD_SHIFT)
    eye = bi == bj
    head0 = _iota((c, LANE), 1) < A_HEAD
    consts = (tri, strict, incl, blk, eye, head0, t2, s2)
    lanes = [slice(p * LANE, (p + 1) * LANE) for p in range(npar)]
    gw = npar * LANE
    toks = [tuple(tok_ref[:, f * gw + p * LANE:f * gw + (p + 1) * LANE] for f in range(SCAN_FIELDS))
            for p in range(npar)]
    ys, sts = _scan_chunks(rev, toks, [st_scr[p] for p in range(npar)], consts)
    for p, sl in enumerate(lanes):
        y_ref[:, sl] = ys[p]
        st_scr[p] = sts[p]

    @pl.when(ci == n_chunks - 1)
    def _():
        st_out_ref[...] = st_scr[...]


def rwkv_scan(packed, s0, rev):
    bsz, t, width = packed.shape
    aw = width // SCAN_FIELDS
    n_pairs = aw // LANE
    npar = scan_group_width(aw) // LANE
    n_chunks = t // SCAN_CHUNK
    if rev:
        tmap = lambda b, g, c: (b, n_chunks - 1 - c, g)
    else:
        tmap = lambda b, g, c: (b, c, g)
    tok = pl.BlockSpec((None, SCAN_CHUNK, npar * LANE), tmap)
    tok_in = pl.BlockSpec((None, SCAN_CHUNK, SCAN_FIELDS * npar * LANE), tmap)
    st = pl.BlockSpec((None, npar, LANE, LANE), lambda b, g, c: (b, g, 0, 0))
    return pl.pallas_call(
        functools.partial(_scan_kernel, rev, npar, n_chunks),
        grid=(bsz, n_pairs // npar, n_chunks),
        in_specs=[tok_in, st],
        out_specs=[tok, st],
        out_shape=[jax.ShapeDtypeStruct((bsz, t, aw), F32),
                   jax.ShapeDtypeStruct(s0.shape, F32)],
        scratch_shapes=[pltpu.VMEM((npar, LANE, LANE), F32)],
        compiler_params=_cparams("parallel", "parallel", "arbitrary"),
        name="rwkv_scan_bwd" if rev else "rwkv_scan_fwd",
    )(packed, s0)


def _rwkv_post_kernel(yf_ref, yb_ref, bonus_ref, g_ref, lnw_ref, lnb_ref, o_ref):
    y = yf_ref[...] + yb_ref[...]
    inv = 1.0 / A_HEAD
    mu = _seg_sum(y) * inv
    yc = y - mu
    var = _seg_sum(yc * yc) * inv
    yn = yc * lax.rsqrt(var + GN_EPS) * lnw_ref[...] + lnb_ref[...]
    o_ref[...] = ((yn + bonus_ref[...]) * g_ref[...]).astype(o_ref.dtype)


def rwkv_post(yf, yb, bonus, g, ln_w, ln_b, tt=256):
    n, aw = yf.shape
    tok = pl.BlockSpec((tt, aw), lambda i: (i, 0))
    par = pl.BlockSpec((1, aw), lambda i: (0, 0))
    return pl.pallas_call(
        _rwkv_post_kernel,
        grid=(n // tt,),
        in_specs=[tok, tok, tok, tok, par, par],
        out_specs=tok,
        out_shape=jax.ShapeDtypeStruct((n, aw), BF16),
        compiler_params=_cparams("parallel"),
        name="rwkv_post",
    )(yf, yb, bonus, g, ln_w, ln_b)


def round_up(x, m):
    return (x + m - 1) // m * m


ATT_SCALE = HEAD_DIM ** -0.5


def _ctx_attn_kernel(q_ref, k_ref, v_ref, sink_ref, o_ref):
    s = _dot3_nt(q_ref[...] * ATT_SCALE, k_ref[...])
    sk = sink_ref[0:1, 0:1]
    m = jnp.maximum(jnp.max(s, axis=-1, keepdims=True), sk)
    e = jnp.exp(s - m)
    den = jnp.sum(e, axis=-1, keepdims=True) + jnp.exp(sk - m)
    o = _dot(e.astype(BF16), v_ref[...].astype(BF16))
    o_ref[...] = (o / den).astype(o_ref.dtype)


def ctx_attention(u, bsz, t, q0, k0, v0, n_heads, group, sink):
    sink = jnp.full((n_heads,), NEG, F32) if sink is None else sink.astype(F32)
    sink = jnp.broadcast_to(sink[:, None, None], (n_heads, 1, LANE))
    return pl.pallas_call(
        _ctx_attn_kernel,
        grid=(bsz, n_heads),
        in_specs=[
            pl.BlockSpec((t, HEAD_DIM), lambda b, h: (b, q0 + h)),
            pl.BlockSpec((t, HEAD_DIM), lambda b, h: (b, k0 + h // group)),
            pl.BlockSpec((t, HEAD_DIM), lambda b, h: (b, v0 + h // group)),
            pl.BlockSpec((None, 1, LANE), lambda b, h: (h, 0, 0)),
        ],
        out_specs=pl.BlockSpec((t, HEAD_DIM), lambda b, h: (b, h)),
        out_shape=jax.ShapeDtypeStruct((bsz * t, n_heads * HEAD_DIM), BF16),
        compiler_params=_cparams("parallel", "parallel"),
        name="ctx_attention",
    )(u, u, u, sink)


def rope_tables(n_tok):
    half = HEAD_DIM // 2
    quarter = half // 2
    tok = jnp.arange(n_tok)
    row = (tok // GRID_W).astype(F32)
    col = (tok % GRID_W).astype(F32)
    inv = ROPE_THETA ** (-jnp.arange(quarter, dtype=F32) / quarter)
    ang_r = row[:, None] * inv[None]
    ang_c = col[:, None] * inv[None]
    cos = jnp.concatenate([jnp.cos(ang_r)] * 2 + [jnp.cos(ang_c)] * 2, axis=1)
    sr, sc = jnp.sin(ang_r), jnp.sin(ang_c)
    z = jnp.zeros_like(sr)
    sin_a = jnp.concatenate([-sr, z, -sc, z], axis=1)
    sin_b = jnp.concatenate([z, sr, z, sc], axis=1)
    return cos, sin_a, sin_b


def _rope_kernel(n_heads, x_ref, cos_ref, sa_ref, sb_ref, o_ref):
    q = HEAD_DIM // 4
    cos, sa, sb = cos_ref[...], sa_ref[...], sb_ref[...]
    for h in range(n_heads):
        sl = slice(h * HEAD_DIM, (h + 1) * HEAD_DIM)
        x = x_ref[:, sl]
        o_ref[:, sl] = x * cos + pltpu.roll(x, HEAD_DIM - q, 1) * sa + pltpu.roll(x, q, 1) * sb


def rope(u, bsz, n_tok, col0, n_heads, tt=256):
    cos, sin_a, sin_b = rope_tables(n_tok)
    nt = n_tok // tt
    hb = max(k for k in range(1, n_heads + 1) if n_heads % k == 0 and col0 % k == 0)
    width = hb * HEAD_DIM
    cb = col0 // hb
    tab = pl.BlockSpec((tt, HEAD_DIM), lambda b, i, j: (i, 0))
    return pl.pallas_call(
        functools.partial(_rope_kernel, hb),
        grid=(bsz, nt, n_heads // hb),
        in_specs=[pl.BlockSpec((tt, width), lambda b, i, j: (b * nt + i, cb + j)), tab, tab, tab],
        out_specs=pl.BlockSpec((tt, width), lambda b, i, j: (b * nt + i, j)),
        out_shape=jax.ShapeDtypeStruct((bsz * n_tok, n_heads * HEAD_DIM), F32),
        compiler_params=_cparams("parallel", "parallel", "parallel"),
        name="rope",
    )(u, cos, sin_a, sin_b)


def _win_attn_kernel(group, n_tok, sink_ref, q_ref, kp_ref, kc_ref, kn_ref, vp_ref, vc_ref, vn_ref,
                     ck_ref, cv_ref, o_ref):
    kv = pl.program_id(1)
    i = pl.program_id(2)
    rows = group * QBLK
    q = jnp.concatenate([q_ref[:, g * HEAD_DIM:(g + 1) * HEAD_DIM] for g in range(group)], axis=0)
    q = q * ATT_SCALE
    k_loc = jnp.concatenate([kp_ref[...], kc_ref[...], kn_ref[...]], axis=0)
    v_loc = jnp.concatenate([vp_ref[...], vc_ref[...], vn_ref[...]], axis=0).astype(BF16)
    s_loc = _dot3_nt(q, k_loc)
    qi = _iota(s_loc.shape, 0) & (QBLK - 1)
    kj = _iota(s_loc.shape, 1)
    pos = i * QBLK - WIN + kj
    ok = (kj - qi >= 0) & (kj - qi <= 2 * WIN) & (pos >= 0) & (pos < n_tok)
    s_loc = jnp.where(ok, s_loc, NEG)
    s_ctx = _dot3_nt(q, ck_ref[...])
    rg = _iota((rows, 1), 0) >> int(np.log2(QBLK))
    sk = jnp.zeros((rows, 1), F32)
    for g in range(group):
        sk = jnp.where(rg == g, sink_ref[kv * group + g], sk)
    m = jnp.maximum(jnp.maximum(jnp.max(s_loc, axis=-1, keepdims=True),
                                jnp.max(s_ctx, axis=-1, keepdims=True)), sk)
    e_loc = jnp.exp(s_loc - m)
    e_ctx = jnp.exp(s_ctx - m)
    den = (jnp.sum(e_loc, axis=-1, keepdims=True) + jnp.sum(e_ctx, axis=-1, keepdims=True)
           + jnp.exp(sk - m))
    o = (_dot(e_loc.astype(BF16), v_loc) + _dot(e_ctx.astype(BF16), cv_ref[...].astype(BF16))) / den
    for g in range(group):
        o_ref[:, g * HEAD_DIM:(g + 1) * HEAD_DIM] = o[g * QBLK:(g + 1) * QBLK].astype(o_ref.dtype)


def window_attention(qk_rot, u, vb0, bsz, n_tok, n_kv, group, ck, cv, sink):
    assert WIN == QBLK
    nb = n_tok // QBLK
    nq = n_kv * group
    past = ck.shape[2]
    prv = lambda i: jnp.maximum(i - 1, 0)
    nxt = lambda i: jnp.minimum(i + 1, nb - 1)
    blk = lambda f, c0: pl.BlockSpec((QBLK, HEAD_DIM), lambda b, kv, i, s: (b * nb + f(i), c0 + kv))
    same = lambda i: i
    cache = pl.BlockSpec((None, None, past, HEAD_DIM), lambda b, kv, i, s: (b, kv, 0, 0))
    grid_spec = pltpu.PrefetchScalarGridSpec(
        num_scalar_prefetch=1,
        grid=(bsz, n_kv, nb),
        in_specs=[
            pl.BlockSpec((QBLK, group * HEAD_DIM), lambda b, kv, i, s: (b * nb + i, kv)),
            blk(prv, nq), blk(same, nq), blk(nxt, nq),
            blk(prv, vb0), blk(same, vb0), blk(nxt, vb0),
            cache, cache,
        ],
        out_specs=pl.BlockSpec((QBLK, group * HEAD_DIM), lambda b, kv, i, s: (b * nb + i, kv)),
    )
    return pl.pallas_call(
        functools.partial(_win_attn_kernel, group, n_tok),
        grid_spec=grid_spec,
        out_shape=jax.ShapeDtypeStruct((bsz * n_tok, nq * HEAD_DIM), BF16),
        compiler_params=_cparams("parallel", "parallel", "parallel"),
        name="window_attention",
    )(sink.astype(F32), qk_rot, qk_rot, qk_rot, qk_rot, u, u, u, ck, cv)


NB_QROWS = 4
NB_QTOK = NB_QROWS * GRID_W
NB_KTOK = 3 * NB_QTOK


def nb_tables(rpb, rows):
    kh = min(NB_ROWS, rows)
    ql = np.arange(NB_QTOK)
    kl = np.arange(NB_KTOK)
    r_rel, c = ql // GRID_W, ql % GRID_W
    kr_rel, kc = kl // GRID_W - NB_QROWS, kl % GRID_W
    n_heads = rpb.shape[0]
    gcol = np.arange(GRID_W)
    col_idx = np.clip(gcol[None, :] - gcol[:, None] + NB_COLS - 1, 0, 2 * NB_COLS - 2)
    t_col = jnp.take(rpb.astype(F32), jnp.asarray(col_idx.reshape(-1)), axis=2)
    t_col = t_col.reshape(n_heads, 2 * NB_ROWS - 1, GRID_W * GRID_W)
    qr = np.arange(NB_QROWS)
    krr = np.arange(3 * NB_QROWS) - NB_QROWS
    row_idx = np.clip(krr[None, :] - qr[:, None] + NB_ROWS - 1, 0, 2 * NB_ROWS - 2)
    bias = jnp.take(t_col, jnp.asarray(row_idx.reshape(-1)), axis=1)
    bias = bias.reshape(n_heads, NB_QROWS, 3 * NB_QROWS, GRID_W, GRID_W)
    bias = jnp.transpose(bias, (0, 1, 3, 2, 4)).reshape(n_heads, NB_QTOK, NB_KTOK)
    win_start = np.clip(c - NB_COLS // 2, 0, GRID_W - NB_COLS)
    col_ok = (kc[None, :] >= win_start[:, None]) & (kc[None, :] < win_start[:, None] + NB_COLS)
    masks = []
    for j in range(rows // NB_QROWS):
        r = j * NB_QROWS + r_rel
        kr = j * NB_QROWS + kr_rel
        row_start = np.clip(r - kh // 2, 0, rows - kh)
        row_ok = (kr[None, :] >= row_start[:, None]) & (kr[None, :] < row_start[:, None] + kh)
        masks.append(row_ok & col_ok)
    return bias, jnp.asarray(np.stack(masks).astype(np.float32))


def _nb_attn_kernel(q_ref, kp_ref, kc_ref, kn_ref, vp_ref, vc_ref, vn_ref, ck_ref, cv_ref,
                    bias_ref, mask_ref, o_ref):
    q = q_ref[...] * ATT_SCALE
    k_loc = jnp.concatenate([kp_ref[...], kc_ref[...], kn_ref[...]], axis=0)
    v_loc = jnp.concatenate([vp_ref[...], vc_ref[...], vn_ref[...]], axis=0).astype(BF16)
    s_loc = jnp.where(mask_ref[...] > 0.0, _dot3_nt(q, k_loc) + bias_ref[...], NEG)
    s_ctx = _dot3_nt(q, ck_ref[...])
    m = jnp.maximum(jnp.max(s_loc, axis=-1, keepdims=True), jnp.max(s_ctx, axis=-1, keepdims=True))
    e_loc = jnp.exp(s_loc - m)
    e_ctx = jnp.exp(s_ctx - m)
    den = jnp.sum(e_loc, axis=-1, keepdims=True) + jnp.sum(e_ctx, axis=-1, keepdims=True)
    o = _dot(e_loc.astype(BF16), v_loc) + _dot(e_ctx.astype(BF16), cv_ref[...].astype(BF16))
    o_ref[...] = (o / den).astype(o_ref.dtype)


def neighbourhood_attention(u, cols, bsz, n_tok, n_heads, ck, cv, rpb):
    q0, k0, v0 = cols
    rows = n_tok // GRID_W
    assert rows % NB_QROWS == 0 and rows >= NB_ROWS
    nj = rows // NB_QROWS
    past = ck.shape[2]
    bias, mask = nb_tables(rpb, rows)
    prv = lambda j: jnp.maximum(j - 1, 0)
    nxt = lambda j: jnp.minimum(j + 1, nj - 1)
    same = lambda j: j
    blk = lambda f, c0: pl.BlockSpec((NB_QTOK, HEAD_DIM), lambda b, h, j: (b * nj + f(j), c0 + h))
    cache = pl.BlockSpec((None, None, past, HEAD_DIM), lambda b, h, j: (b, h, 0, 0))
    return pl.pallas_call(
        _nb_attn_kernel,
        grid=(bsz, n_heads, nj),
        in_specs=[
            blk(same, q0),
            blk(prv, k0), blk(same, k0), blk(nxt, k0),
            blk(prv, v0), blk(same, v0), blk(nxt, v0),
            cache, cache,
            pl.BlockSpec((None, NB_QTOK, NB_KTOK), lambda b, h, j: (h, 0, 0)),
            pl.BlockSpec((None, NB_QTOK, NB_KTOK), lambda b, h, j: (j, 0, 0)),
        ],
        out_specs=pl.BlockSpec((NB_QTOK, HEAD_DIM), lambda b, h, j: (b * nj + j, h)),
        out_shape=jax.ShapeDtypeStruct((bsz * n_tok, n_heads * HEAD_DIM), BF16),
        compiler_params=_cparams("parallel", "parallel", "parallel"),
        name="neighbourhood_attention",
    )(u, u, u, u, u, u, u, ck, cv, bias, mask)


def _merge_kernel(ya_ref, yb_ref, yc_ref, wa_ref, wb_ref, wc_ref, ga_ref, gb_ref, gc_ref, o_ref):
    m = _sigmoid(ga_ref[...]) * _dot(ya_ref[...], wa_ref[...])
    m = m + _sigmoid(gb_ref[...]) * _dot(yb_ref[...], wb_ref[...])
    m = m + _sigmoid(gc_ref[...]) * _dot(yc_ref[...], wc_ref[...])
    o_ref[...] = m.astype(o_ref.dtype)


def merge_branches(ya, yb, yc, wa, wb, wc, u, gate_col0, tm=512, tn=512):
    n, aw = ya.shape
    bw, cw = wb.shape[0], wc.shape[0]
    d = wa.shape[1]
    tm = _pick(n, (tm, 256, 128))
    tn = _pick(d, (tn, 256, 128))
    assert gate_col0 % tn == 0
    g0 = gate_col0 // tn
    nd = d // tn
    gate = lambda gi: pl.BlockSpec((tm, tn), lambda i, j: (i, g0 + gi * nd + j))
    return pl.pallas_call(
        _merge_kernel,
        grid=(n // tm, nd),
        in_specs=[
            pl.BlockSpec((tm, aw), lambda i, j: (i, 0)),
            pl.BlockSpec((tm, bw), lambda i, j: (i, 0)),
            pl.BlockSpec((tm, cw), lambda i, j: (i, 0)),
            pl.BlockSpec((aw, tn), lambda i, j: (0, j)),
            pl.BlockSpec((bw, tn), lambda i, j: (0, j)),
            pl.BlockSpec((cw, tn), lambda i, j: (0, j)),
            gate(0), gate(1), gate(2),
        ],
        out_specs=pl.BlockSpec((tm, tn), lambda i, j: (i, j)),
        out_shape=jax.ShapeDtypeStruct((n, d), BF16),
        compiler_params=_cparams("parallel", "parallel"),
        name="merge_branches",
    )(ya, yb, yc, wa, wb, wc, u, u, u)


def _post_norm(alpha, x, gate, y, g, b):
    r = alpha * x + gate * y
    return _ln_rows(r) * g + b


def _outproj_ln_kernel(alpha, nk, m_ref, w_ref, x_ref, gate_ref, g_ref, b_ref, o_ref):
    k = pl.program_id(1)

    @pl.when(k == 0)
    def _():
        o_ref[...] = _dot(m_ref[...], w_ref[...])

    @pl.when(k > 0)
    def _():
        o_ref[...] += _dot(m_ref[...], w_ref[...])

    @pl.when(k == nk - 1)
    def _():
        rows = o_ref.shape[0]
        step = min(rows, 128)
        for r0 in range(0, rows, step):
            sl = slice(r0, r0 + step)
            o_ref[sl, :] = _post_norm(alpha, x_ref[sl, :], gate_ref[...], o_ref[sl, :],
                                      g_ref[...], b_ref[...])


def _tile_mod_spec(mod, d, tm, seq_len):
    if mod.shape[0] == 1:
        return pl.BlockSpec((None, 1, d), lambda i, *_: (0, 0, 0))
    return pl.BlockSpec((None, 1, d), lambda i, *_: (i * tm // seq_len, 0, 0))


def outproj_postnorm(m, w_out, x, gate, ln_g, ln_b, alpha, seq_len, tm=512, tk=512):
    n, d = x.shape
    kdim = m.shape[1]
    tm = _pick(n if gate.shape[0] == 1 else seq_len, (tm, 256, 128))
    tk = _pick(kdim, (tk, 256, 128))
    nk = kdim // tk
    vec = pl.BlockSpec((1, d), lambda i, k: (0, 0))
    return pl.pallas_call(
        functools.partial(_outproj_ln_kernel, alpha, nk),
        grid=(n // tm, nk),
        in_specs=[
            pl.BlockSpec((tm, tk), lambda i, k: (i, k)),
            pl.BlockSpec((tk, d), lambda i, k: (k, 0)),
            pl.BlockSpec((tm, d), lambda i, k: (i, 0)),
            _tile_mod_spec(gate, d, tm, seq_len),
            vec, vec,
        ],
        out_specs=pl.BlockSpec((tm, d), lambda i, k: (i, 0)),
        out_shape=jax.ShapeDtypeStruct((n, d), F32),
        compiler_params=_cparams("parallel", "arbitrary"),
        name="outproj_postnorm",
    )(m, w_out, x, gate, ln_g, ln_b)


def _select_kernel(cap, n, aff_ref, mask_ref):
    bits = lax.bitcast_convert_type(aff_ref[...], jnp.int32)
    n_exp = bits.shape[0]
    capf = jnp.float32(cap)

    def count(pred):
        return jnp.sum(pred.astype(F32), axis=1, keepdims=True)

    def value_step(_, carry):
        lo, hi = carry
        mid = lo + ((hi - lo + 1) >> 1)
        ok = count(bits >= mid) >= capf
        return jnp.where(ok, mid, lo), jnp.where(ok, hi, mid - 1)

    lo0 = jnp.zeros((n_exp, 1), jnp.int32)
    hi0 = jnp.full((n_exp, 1), 0x7F800000, jnp.int32)
    thr, _ = lax.fori_loop(0, 32, value_step, (lo0, hi0))
    gt = bits > thr
    eq = bits == thr
    need = capf - count(gt)
    idx = _iota(bits.shape, 1)

    def index_step(_, carry):
        lo, hi = carry
        mid = (lo + hi) >> 1
        ok = count(eq & (idx < mid)) >= need
        return jnp.where(ok, lo, mid), jnp.where(ok, mid, hi)

    _, bound = lax.fori_loop(0, int(np.ceil(np.log2(n))) + 1, index_step,
                             (jnp.zeros((n_exp, 1), jnp.int32), jnp.full((n_exp, 1), n, jnp.int32)))
    mask_ref[...] = (gt | (eq & (idx < bound))).astype(jnp.int32)


def expert_choice_mask(aff_t, cap):
    n_exp, n = aff_t.shape
    return pl.pallas_call(
        functools.partial(_select_kernel, cap, n),
        out_shape=jax.ShapeDtypeStruct((n_exp, n), jnp.int32),
        compiler_params=pltpu.CompilerParams(vmem_limit_bytes=VMEM_LIMIT_BYTES),
        name="expert_choice_mask",
    )(aff_t)


def _count_le(sorted_vals, x):
    return jnp.sum((sorted_vals[None, :] <= x[:, None]).astype(jnp.int32), axis=1)


def _visit_list(nvis, first_blk, n_work):
    na, nb = nvis.shape
    flat = nvis.reshape(-1)
    off_end = jnp.cumsum(flat)
    off_start = off_end - flat
    total = off_end[-1]
    w = jnp.minimum(jnp.arange(n_work, dtype=jnp.int32), total - 1)
    idx = jnp.minimum(_count_le(off_end, w), na * nb - 1)
    blk = first_blk.reshape(-1)[idx] + (w - off_start[idx])
    valid = (jnp.arange(n_work) < total).astype(jnp.int32)
    return idx // nb, idx % nb, blk.astype(jnp.int32), valid


MOE_GROUP = 8
NO_SLOT = 1 << 28


def _group_items(key, valid, n_keys, group, n_steps):
    n_items = key.shape[0]
    cnt = jnp.zeros((n_keys,), jnp.int32).at[key].add(valid)
    start = jnp.cumsum(cnt) - cnt
    per_key = (cnt + group - 1) // group
    s_end = jnp.cumsum(per_key)
    s_start = s_end - per_key
    total = s_end[-1]
    step = jnp.arange(n_steps, dtype=jnp.int32)
    real = step < total
    sc = jnp.minimum(step, total - 1)
    k = jnp.minimum(_count_le(s_end, sc), n_keys - 1)
    j = sc - s_start[k]
    within = j[None, :] * group + jnp.arange(group, dtype=jnp.int32)[:, None]
    ok = (within < cnt[k][None, :]) & real[None, :]
    item = jnp.clip(start[k][None, :] + jnp.minimum(within, cnt[k][None, :] - 1), 0, n_items - 1)
    first = ((j == 0) & real).astype(jnp.int32)
    last = ((j == per_key[k] - 1) & real).astype(jnp.int32)
    return item, ok.astype(jnp.int32), first, last


def routing_plan(mask, cap):
    n_exp, n = mask.shape
    nsb = cap // SLOT_BLK
    pos = jnp.cumsum(mask, axis=1) - mask
    posm = jnp.where(mask > 0, pos, -1).astype(jnp.int32)

    def tile_visits(tile):
        cnt = mask.reshape(n_exp, n // tile, tile).sum(-1)
        cend = jnp.cumsum(cnt, axis=1)
        lo = jnp.minimum((cend - cnt) // SLOT_BLK, nsb - 1)
        return jnp.where(cnt > 0, (cend - 1) // SLOT_BLK - lo + 1, 0), lo

    n_blk = n_exp * nsb
    grp = MOE_GROUP
    nvis, sb_lo = tile_visits(ROUTE_TILE)
    n_work = n_exp * (nsb + n // ROUTE_TILE)
    e, t, sb, valid = _visit_list(nvis, sb_lo, n_work)
    blk = e * nsb + sb
    item, ok, first, last = _group_items(blk, valid, n_blk, grp, n_work // grp + n_blk + 1)
    blk_g = blk[item[0]]
    dispatch = (blk_g // nsb, blk_g, first, last, t[item], jnp.where(ok > 0, sb[item] * SLOT_BLK, NO_SLOT))
    nvis, sb_lo = tile_visits(TOK_TILE)
    nt = n // TOK_TILE
    n_work = n_exp * (nsb + nt)
    nvis_t = nvis.T.at[:, 0].max(1)
    t2, e2, sb2, valid2 = _visit_list(nvis_t, sb_lo.T, n_work + nt)
    item, ok, first, last = _group_items(t2, valid2, nt, grp, (n_work + nt) // grp + nt + 1)
    combine = (t2[item[0]], first, last, e2[item], e2[item] * nsb + sb2[item],
               jnp.where(ok > 0, sb2[item] * SLOT_BLK, NO_SLOT))
    return posm, dispatch, combine


def _route_kernel(group, e_ref, blk_ref, first_ref, last_ref, tile_ref, base_ref, *refs):
    posm_refs, aff_refs = refs[:group], refs[group:2 * group]
    o_ref, acc_ref = refs[2 * group:]
    w = pl.program_id(0)
    rows = _iota((SLOT_BLK, ROUTE_TILE), 0)

    def total():
        tot = None
        for q in range(group):
            onehot = ((rows + base_ref[q, w]) == posm_refs[q][...]).astype(BF16)
            part = _dot_exact_lhs(onehot, aff_refs[q][...])
            tot = part if tot is None else tot + part
        return tot

    @pl.when(first_ref[w] == 1)
    def _():
        acc_ref[...] = total()

    @pl.when((first_ref[w] == 0) & (base_ref[0, w] != NO_SLOT))
    def _():
        acc_ref[...] += total()

    @pl.when(last_ref[w] == 1)
    def _():
        o_ref[...] = acc_ref[...]


def moe_route(aff, posm, plan, cap):
    n = aff.shape[0]
    n_exp = posm.shape[0]
    grp = MOE_GROUP
    n_steps = plan[0].shape[0]
    pos_spec = lambda q: pl.BlockSpec((None, 1, ROUTE_TILE), lambda w, e, b, f, l, t, s: (e[w], 0, t[q, w]))
    aff_spec = lambda q: pl.BlockSpec((ROUTE_TILE, LANE), lambda w, e, b, f, l, t, s: (t[q, w], 0))
    grid_spec = pltpu.PrefetchScalarGridSpec(
        num_scalar_prefetch=6,
        grid=(n_steps,),
        in_specs=[pos_spec(q) for q in range(grp)] + [aff_spec(q) for q in range(grp)],
        out_specs=pl.BlockSpec((SLOT_BLK, LANE), lambda w, e, b, f, l, t, s: (b[w], 0)),
        scratch_shapes=[pltpu.VMEM((SLOT_BLK, LANE), F32)],
    )
    posm3 = posm.reshape(n_exp, 1, n)
    return pl.pallas_call(
        functools.partial(_route_kernel, grp),
        grid_spec=grid_spec,
        out_shape=jax.ShapeDtypeStruct((n_exp * cap, LANE), F32),
        compiler_params=_cparams("arbitrary"),
        name="moe_route",
    )(*plan, *([posm3] * grp), *([aff] * grp))


GATHER_ROWS = 256


def _gather_kernel(idx_ref, tok_hbm, o_ref, buf, sem):
    base = pl.program_id(0) * GATHER_ROWS

    def row_copy(r):
        return pltpu.make_async_copy(tok_hbm.at[pl.ds(idx_ref[base + r], 1), :],
                                     buf.at[pl.ds(r, 1), :], sem)

    def start(r, carry):
        row_copy(r).start()
        return carry

    def wait(r, carry):
        row_copy(r).wait()
        return carry

    lax.fori_loop(0, GATHER_ROWS, start, 0, unroll=8)
    lax.fori_loop(0, GATHER_ROWS, wait, 0, unroll=8)
    o_ref[...] = buf[...].astype(o_ref.dtype)


def moe_gather(tokens, idx):
    n, d = tokens.shape
    rows = idx.shape[0]
    assert rows % GATHER_ROWS == 0
    grid_spec = pltpu.PrefetchScalarGridSpec(
        num_scalar_prefetch=1,
        grid=(rows // GATHER_ROWS,),
        in_specs=[pl.BlockSpec(memory_space=pl.ANY)],
        out_specs=pl.BlockSpec((GATHER_ROWS, d), lambda i, idx_ref: (i, 0)),
        scratch_shapes=[pltpu.VMEM((GATHER_ROWS, d), F32), pltpu.SemaphoreType.DMA(())],
    )
    return pl.pallas_call(
        _gather_kernel,
        grid_spec=grid_spec,
        out_shape=jax.ShapeDtypeStruct((rows, d), BF16),
        compiler_params=_cparams("arbitrary"),
        name="moe_gather",
    )(idx, tokens)


def _ffn_up_kernel(x_ref, w1_ref, w3_ref, o_ref):
    x = x_ref[...]
    a = _dot(x, w1_ref[...])
    o_ref[...] = (a * _sigmoid(a) * _dot(x, w3_ref[...])).astype(o_ref.dtype)


def _ffn_down_kernel(h_ref, w2_ref, aff_ref, o_ref):
    aff = aff_ref[...]
    gval = jnp.sum(jnp.where(_iota(aff.shape, 1) == pl.program_id(0), aff, 0.0), axis=1, keepdims=True)
    o_ref[...] = (_dot(h_ref[...], w2_ref[...]) * gval).astype(o_ref.dtype)


def expert_ffn(xe, aff_rows, w1, w3, w2, layer, cap, tn=512):
    _, n_exp, d, ff = w1.shape
    tm = _pick(cap, (1024, 512, 256, 128))
    nm = cap // tm
    tf = _pick(ff, (tn, 256, 128))
    hid = pl.pallas_call(
        _ffn_up_kernel,
        grid=(n_exp, nm, ff // tf),
        in_specs=[
            pl.BlockSpec((tm, d), lambda e, i, j: (e * nm + i, 0)),
            pl.BlockSpec((None, None, d, tf), lambda e, i, j: (layer, e, 0, j)),
            pl.BlockSpec((None, None, d, tf), lambda e, i, j: (layer, e, 0, j)),
        ],
        out_specs=pl.BlockSpec((tm, tf), lambda e, i, j: (e * nm + i, j)),
        out_shape=jax.ShapeDtypeStruct((n_exp * cap, ff), BF16),
        compiler_params=_cparams("parallel", "parallel", "parallel"),
        name="expert_ffn_up",
    )(xe, w1, w3)
    td = _pick(d, (tn, 256, 128))
    return pl.pallas_call(
        _ffn_down_kernel,
        grid=(n_exp, nm, d // td),
        in_specs=[
            pl.BlockSpec((tm, ff), lambda e, i, j: (e * nm + i, 0)),
            pl.BlockSpec((None, None, ff, td), lambda e, i, j: (layer, e, 0, j)),
            pl.BlockSpec((tm, LANE), lambda e, i, j: (e * nm + i, 0)),
        ],
        out_specs=pl.BlockSpec((tm, td), lambda e, i, j: (e * nm + i, j)),
        out_shape=jax.ShapeDtypeStruct((n_exp * cap, d), BF16),
        compiler_params=_cparams("parallel", "parallel", "parallel"),
        name="expert_ffn_down",
    )(hid, w2, aff_rows)


def _combine_kernel(alpha, group, tile_ref, first_ref, last_ref, e_ref, blk_ref, base_ref,
                    posm_ref, *refs):
    ye_refs = refs[:group]
    x_ref, gate_ref, g_ref, b_ref, o_ref, acc_ref = refs[group:]
    w = pl.program_id(0)
    posm = posm_ref[...].astype(F32)
    lane = _iota(posm.shape, 1)
    cols = _iota((TOK_TILE, SLOT_BLK), 1)

    def total():
        onehots = []
        for q in range(group):
            col = jnp.sum(jnp.where(lane == e_ref[q, w], posm, 0.0), axis=1, keepdims=True)
            onehots.append(((cols + base_ref[q, w]).astype(F32) == col).astype(BF16))
        return _dot(jnp.concatenate(onehots, axis=1),
                    jnp.concatenate([r[...] for r in ye_refs], axis=0))

    @pl.when(first_ref[w] == 1)
    def _():
        acc_ref[...] = total()

    @pl.when((first_ref[w] == 0) & (base_ref[0, w] != NO_SLOT))
    def _():
        acc_ref[...] += total()

    @pl.when(last_ref[w] == 1)
    def _():
        o_ref[...] = _post_norm(alpha, x_ref[...], gate_ref[...], acc_ref[...], g_ref[...], b_ref[...])


def moe_combine_postnorm(ye, posm_t, plan, x, gate, ln_g, ln_b, alpha, seq_len):
    n, d = x.shape
    n_exp = posm_t.shape[1]
    grp = MOE_GROUP
    n_steps = plan[0].shape[0]
    assert seq_len % TOK_TILE == 0
    tile = lambda w, t, *_: (t[w], 0)
    if gate.shape[0] == 1:
        gate_map = lambda w, t, *_: (0, 0, 0)
    else:
        gate_map = lambda w, t, *_: (t[w] * TOK_TILE // seq_len, 0, 0)
    vec = pl.BlockSpec((1, d), lambda w, *_: (0, 0))
    ye_spec = lambda q: pl.BlockSpec((SLOT_BLK, d), lambda w, t, f, l, e, b, s: (b[q, w], 0))
    grid_spec = pltpu.PrefetchScalarGridSpec(
        num_scalar_prefetch=6,
        grid=(n_steps,),
        in_specs=([pl.BlockSpec((TOK_TILE, n_exp), tile)] + [ye_spec(q) for q in range(grp)]
                  + [pl.BlockSpec((TOK_TILE, d), tile), pl.BlockSpec((None, 1, d), gate_map), vec, vec]),
        out_specs=pl.BlockSpec((TOK_TILE, d), tile),
        scratch_shapes=[pltpu.VMEM((TOK_TILE, d), F32)],
    )
    return pl.pallas_call(
        functools.partial(_combine_kernel, alpha, grp),
        grid_spec=grid_spec,
        out_shape=jax.ShapeDtypeStruct((n, d), F32),
        compiler_params=_cparams("arbitrary"),
        name="moe_combine_postnorm",
    )(*plan, posm_t, *([ye] * grp), x, gate, ln_g, ln_b)


def ec_moe_postnorm(x, mods_shift, mods_scale, gate, router_w, w1, w3, w2, layer, ln_g, ln_b, alpha):
    bsz, t, d = x.shape
    n = bsz * t
    n_exp = router_w.shape[1]
    cap = max(1, CAP_FACTOR * n // n_exp)
    assert cap % SLOT_BLK == 0 and n % TOK_TILE == 0 and n % ROUTE_TILE == 0
    h, aff = ln_modulate_router(x, mods_shift, mods_scale, router_w)
    h = h.reshape(n, d)
    aff = aff.reshape(n, LANE)
    mask = expert_choice_mask(aff[:, :n_exp].T, cap)
    posm, dispatch, combine = routing_plan(mask, cap)
    aff_rows = moe_route(aff, posm, dispatch, cap)
    slot_token = jnp.clip(aff_rows[:, n_exp].astype(jnp.int32), 0, n - 1)
    xe = moe_gather(h, slot_token)
    ye = expert_ffn(xe, aff_rows, w1, w3, w2, layer, cap)
    out = moe_combine_postnorm(ye, posm.T, combine, x.reshape(n, d), gate, ln_g, ln_b, alpha, t)
    return out.reshape(bsz, t, d)


def prep_rwkv_params(rw, aw):
    a_cols = rw["mu"].shape[0]
    ap = round_up(a_cols, LANE)
    gp = ap - 3 * aw - 2 * LANE
    assert rw["w2"].shape[:2] == (2, LANE // 2) and rw["a2"].shape[:2] == (2, LANE // 2)
    assert gp >= rw["g2"].shape[0] and aw % LANE == 0
    row = lambda a: a.reshape(1, aw)
    return dict(
        mu=jnp.pad(rw["mu"], (0, ap - a_cols)).reshape(1, ap),
        w0=rw["w0"], w2=_hi_lo(rw["w2"].reshape(LANE, aw)), a0=rw["a0"],
        a2=_hi_lo(rw["a2"].reshape(LANE, aw)),
        g2=_hi_lo(jnp.pad(rw["g2"], ((0, gp - rw["g2"].shape[0]), (0, 0)))),
        k_k=row(rw["k_k"]), k_a=row(rw["k_a"]), r_k=row(rw["r_k"]),
        ln_w=row(rw["ln_w"]), ln_b=row(rw["ln_b"]), ap=ap, aw=aw)


def states_to_pairs(s):
    bsz, h = s.shape[:2]
    st = jnp.swapaxes(s, -1, -2).reshape(bsz, h // 2, 2, A_HEAD, A_HEAD)
    z = jnp.zeros_like(st[:, :, 0])
    top = jnp.concatenate([st[:, :, 0], z], axis=-1)
    bot = jnp.concatenate([z, st[:, :, 1]], axis=-1)
    return jnp.concatenate([top, bot], axis=-2)


def pairs_to_states(sp):
    bsz = sp.shape[0]
    st = jnp.stack([sp[:, :, :A_HEAD, :A_HEAD], sp[:, :, A_HEAD:, A_HEAD:]], axis=2)
    return jnp.swapaxes(st.reshape(bsz, -1, A_HEAD, A_HEAD), -1, -2)


def rwkv_branch(u, bsz, t, rwp, s0f, s0b):
    aw, ap = rwp["aw"], rwp["ap"]
    n = bsz * t
    pack_f, pack_b, bonus, g = rwkv_prep(u, t, rwp, aw, ap)
    sh = lambda a: a.reshape(bsz, t, SCAN_FIELDS * aw)
    yf, sf = rwkv_scan(sh(pack_f), states_to_pairs(s0f), False)
    yb, sb = rwkv_scan(sh(pack_b), states_to_pairs(s0b), True)
    ya = rwkv_post(yf.reshape(n, aw), yb.reshape(n, aw), bonus, g, rwp["ln_w"], rwp["ln_b"])
    return ya, pairs_to_states(sf), pairs_to_states(sb)


def _mod_rows(m):
    return [m[:, i][:, None, :] for i in range(6)]


def kernel(x_prompt, x_sample, cache_win_k, cache_win_v, cache_nb_k, cache_nb_v, state_rwkv_fwd, state_rwkv_bwd, c, c_ctx, ada_w, ada_b, w_in, rwkv_mu, rwkv_w0, rwkv_w2, rwkv_a0, rwkv_a2, rwkv_g2, rwkv_kk, rwkv_ka, rwkv_rk, rwkv_lnx_w, rwkv_lnx_b, win_sink, nb_rpb, w_br_a, w_br_b, w_br_c, w_out, ln1_g, ln1_b, ln2_g, ln2_b, router_w, exp_w1, exp_w3, exp_w2):
    depth, d, in_w = w_in.shape
    bsz, seq, _ = x_prompt.shape
    dbsz, dseq, _ = x_sample.shape
    aw = rwkv_w0.shape[-1]
    a_heads = aw // A_HEAD
    a_cols = rwkv_mu.shape[-1]
    ap = round_up(a_cols, LANE)
    n_b = win_sink.shape[-1]
    n_kv = cache_win_k.shape[3]
    group = n_b // n_kv
    n_c = nb_rpb.shape[1]
    bw, cw = n_b * HEAD_DIM, n_c * HEAD_DIM
    alpha = float((2 * depth) ** 0.25)
    assert in_w == a_cols + bw + 2 * n_kv * HEAD_DIM + 3 * cw + 3 * d

    qb0 = ap // LANE
    kb0 = qb0 + n_b
    vb0 = kb0 + n_kv
    qn0 = vb0 + n_kv
    kn0 = qn0 + n_c
    vn0 = kn0 + n_c
    gate_col0 = (vn0 + n_c) * LANE

    rows = 1 + dbsz
    cvecs = jnp.pad(jnp.concatenate([c_ctx[None], c], axis=0), ((0, round_up(rows, 16) - rows), (0, 0)))
    mods = ada_mods(cvecs, ada_w, ada_b).reshape(depth, -1, 6, d)

    xp, xs = x_prompt, x_sample
    np_tok, ns_tok = bsz * seq, dbsz * dseq
    win_k, win_v, nb_k, nb_v, st_f, st_b = [], [], [], [], [], []
    zero_state = jnp.zeros((bsz, a_heads, A_HEAD, A_HEAD), F32)
    w_cat = jnp.concatenate(
        [w_in[:, :, :a_cols].astype(BF16), jnp.zeros((depth, d, ap - a_cols), BF16),
         w_in[:, :, a_cols:].astype(BF16)], axis=2)
    w1, w3, w2 = exp_w1.astype(BF16), exp_w3.astype(BF16), exp_w2.astype(BF16)
    for l in range(depth):
        rwp = prep_rwkv_params(
            {"mu": rwkv_mu[l], "w0": rwkv_w0[l], "w2": rwkv_w2[l], "a0": rwkv_a0[l], "a2": rwkv_a2[l],
             "g2": rwkv_g2[l], "k_k": rwkv_kk[l], "k_a": rwkv_ka[l], "r_k": rwkv_rk[l],
             "ln_w": rwkv_lnx_w[l], "ln_b": rwkv_lnx_b[l]}, aw)
        wa, wb, wc = w_br_a[l].astype(BF16), w_br_b[l].astype(BF16), w_br_c[l].astype(BF16)
        wo = w_out[l].astype(BF16)
        g1, b1 = ln1_g[l].reshape(1, d), ln1_b[l].reshape(1, d)
        g2, b2 = ln2_g[l].reshape(1, d), ln2_b[l].reshape(1, d)

        sh1, sc1, gt1, sh2, sc2, gt2 = _mod_rows(mods[l, 0:1])
        h = ln_modulate(xp, sh1, sc1).reshape(np_tok, d)
        u = matmul(h, w_cat, l, name="in_proj")
        ya, s_f, s_b = rwkv_branch(u, bsz, seq, rwp, zero_state, zero_state)
        yb = ctx_attention(u, bsz, seq, qb0, kb0, vb0, n_b, group, win_sink[l])
        yc = ctx_attention(u, bsz, seq, qn0, kn0, vn0, n_c, 1, None)
        m = merge_branches(ya, yb, yc, wa, wb, wc, u, gate_col0)
        x1 = outproj_postnorm(m, wo, xp.reshape(np_tok, d), gt1, g1, b1, alpha, seq)
        xp = ec_moe_postnorm(x1.reshape(bsz, seq, d), sh2, sc2, gt2, router_w[l], w1, w3, w2, l, g2, b2, alpha)
        cols = lambda c0, nh: u[:, c0 * LANE:(c0 + nh) * LANE].reshape(bsz, seq, nh, HEAD_DIM)
        win_k.append(cols(kb0, n_kv))
        win_v.append(cols(vb0, n_kv))
        nb_k.append(cols(kn0, n_c))
        nb_v.append(cols(vn0, n_c))
        st_f.append(s_f)
        st_b.append(s_b)

        sh1, sc1, gt1, sh2, sc2, gt2 = _mod_rows(mods[l, 1:1 + dbsz])
        h = ln_modulate(xs, sh1, sc1).reshape(ns_tok, d)
        u = matmul(h, w_cat, l, name="in_proj")
        ya, _, _ = rwkv_branch(u, dbsz, dseq, rwp, state_rwkv_fwd[:, l], state_rwkv_bwd[:, l])
        qk_rot = rope(u, dbsz, dseq, qb0, n_b + n_kv)
        heads_first = lambda a: jnp.swapaxes(a[:, l], 1, 2)
        yb = window_attention(qk_rot, u, vb0, dbsz, dseq, n_kv, group,
                              heads_first(cache_win_k), heads_first(cache_win_v), win_sink[l])
        yc = neighbourhood_attention(u, (qn0, kn0, vn0), dbsz, dseq, n_c,
                                     heads_first(cache_nb_k), heads_first(cache_nb_v), nb_rpb[l])
        m = merge_branches(ya, yb, yc, wa, wb, wc, u, gate_col0)
        x1 = outproj_postnorm(m, wo, xs.reshape(ns_tok, d), gt1, g1, b1, alpha, dseq)
        xs = ec_moe_postnorm(x1.reshape(dbsz, dseq, d), sh2, sc2, gt2, router_w[l], w1, w3, w2, l, g2, b2, alpha)

    stack = lambda xs_: jnp.stack(xs_, axis=1)
    return (xp, xs, stack(win_k), stack(win_v), stack(nb_k), stack(nb_v), stack(st_f), stack(st_b))
```

```python
import functools

import numpy as np
import jax
import jax.numpy as jnp
from jax import lax
from jax.experimental import pallas as pl
from jax.experimental.pallas import tpu as pltpu

F32 = jnp.float32
BF16 = jnp.bfloat16

HEAD_DIM = 128
A_HEAD = 64
GRID_W = 64
WIN = 128
QBLK = 128
NB_ROWS = 8
NB_COLS = 16
CAP_FACTOR = 2
ROPE_THETA = 10000.0
LN_EPS = 1e-6
GN_EPS = 64e-5
NEG = -1e30

LANE = 128
SUBLANE = 8
VMEM_LIMIT_BYTES = 56 * 1024 * 1024

SCAN_CHUNK = 64
SLOT_BLK = 128
TOK_TILE = 256
ROUTE_TILE = 256


def _cparams(*sem):
    return pltpu.CompilerParams(dimension_semantics=sem, vmem_limit_bytes=VMEM_LIMIT_BYTES)


def _dot(a, b):
    return jnp.dot(a, b, preferred_element_type=F32)


def _dot_nt(a, b):
    return lax.dot_general(a, b, (((1,), (1,)), ((), ())), preferred_element_type=F32)


def _split2(x):
    hi = x.astype(BF16)
    lo = (x - hi.astype(F32)).astype(BF16)
    return hi, lo


def _split3(x):
    hi = x.astype(BF16)
    r1 = x - hi.astype(F32)
    mid = r1.astype(BF16)
    lo = (r1 - mid.astype(F32)).astype(BF16)
    return hi, mid, lo


def _dot3(a, b):
    ah, al = _split2(a)
    bh, bl = _split2(b)
    return _dot(ah, bh) + (_dot(ah, bl) + _dot(al, bh))


def _hi_lo(w):
    hi, lo = _split2(w)
    return jnp.stack([hi, lo])


def _dot3_split_rhs(a, b_ref):
    ah, al = _split2(a)
    bh = b_ref[0]
    return _dot(ah, bh) + (_dot(ah, b_ref[1]) + _dot(al, bh))


def _dot3_nt(a, b):
    ah, al = _split2(a)
    bh, bl = _split2(b)
    return _dot_nt(ah, bh) + (_dot_nt(ah, bl) + _dot_nt(al, bh))


def _dot_exact_lhs(a_bf16, b):
    bh, bm, bl = _split3(b)
    return _dot(a_bf16, bh) + (_dot(a_bf16, bm) + _dot(a_bf16, bl))


def _sigmoid(x):
    return 1.0 / (1.0 + jnp.exp(-x))


def _iota(shape, dim):
    return lax.broadcasted_iota(jnp.int32, shape, dim)


def _ada_kernel(c_ref, w_ref, b_ref, o_ref):
    c = c_ref[...]
    a = c * _sigmoid(c)
    o_ref[...] = _dot3(a, w_ref[...]) + b_ref[...]


def ada_mods(cvecs, ada_w, ada_b, tn=512):
    depth, d, n6 = ada_w.shape
    rows = cvecs.shape[0]
    return pl.pallas_call(
        _ada_kernel,
        grid=(depth, n6 // tn),
        in_specs=[
            pl.BlockSpec((rows, d), lambda l, j: (0, 0)),
            pl.BlockSpec((None, d, tn), lambda l, j: (l, 0, j)),
            pl.BlockSpec((None, 1, tn), lambda l, j: (l, 0, j)),
        ],
        out_specs=pl.BlockSpec((None, rows, tn), lambda l, j: (l, 0, j)),
        out_shape=jax.ShapeDtypeStruct((depth, rows, n6), F32),
        compiler_params=_cparams("parallel", "parallel"),
        name="ada_mods",
    )(cvecs, ada_w, ada_b.reshape(depth, 1, n6))


def _ln_rows(x):
    mu = jnp.mean(x, axis=-1, keepdims=True)
    xc = x - mu
    var = jnp.mean(xc * xc, axis=-1, keepdims=True)
    return xc * lax.rsqrt(var + LN_EPS)


def _lnmod_kernel(x_ref, sh_ref, sc_ref, o_ref):
    y = _ln_rows(x_ref[...])
    o_ref[...] = (y * (1.0 + sc_ref[...]) + sh_ref[...]).astype(o_ref.dtype)


def _mod_spec(mod, d):
    if mod.shape[0] == 1:
        return pl.BlockSpec((None, 1, d), lambda b, i: (0, 0, 0))
    return pl.BlockSpec((None, 1, d), lambda b, i: (b, 0, 0))


def ln_modulate(x, shift, scale, tt=256):
    bsz, t, d = x.shape
    return pl.pallas_call(
        _lnmod_kernel,
        grid=(bsz, t // tt),
        in_specs=[
            pl.BlockSpec((None, tt, d), lambda b, i: (b, i, 0)),
            _mod_spec(shift, d),
            _mod_spec(scale, d),
        ],
        out_specs=pl.BlockSpec((None, tt, d), lambda b, i: (b, i, 0)),
        out_shape=jax.ShapeDtypeStruct((bsz, t, d), BF16),
        compiler_params=_cparams("parallel", "parallel"),
        name="ln_modulate",
    )(x, shift, scale)


def _lnmod_router_kernel(n_exp, seq_len, x_ref, sh_ref, sc_ref, rw_ref, h_ref, aff_ref):
    y = _ln_rows(x_ref[...])
    h = y * (1.0 + sc_ref[...]) + sh_ref[...]
    h_ref[...] = h
    logits = _dot3(h, rw_ref[...])
    lane = _iota(logits.shape, 1)
    logits = jnp.where(lane < n_exp, logits, NEG)
    m = jnp.max(logits, axis=-1, keepdims=True)
    e = jnp.exp(logits - m)
    aff = e / jnp.sum(e, axis=-1, keepdims=True)
    tt = aff.shape[0]
    tok = pl.program_id(0) * seq_len + pl.program_id(1) * tt + _iota(aff.shape, 0)
    aff_ref[...] = jnp.where(lane == n_exp, tok.astype(F32), aff)


def ln_modulate_router(x, shift, scale, router_w, tt=256):
    bsz, t, d = x.shape
    n_exp = router_w.shape[1]
    assert n_exp < LANE and bsz * t < (1 << 24)
    rw = jnp.pad(router_w, ((0, 0), (0, LANE - n_exp)))
    return pl.pallas_call(
        functools.partial(_lnmod_router_kernel, n_exp, t),
        grid=(bsz, t // tt),
        in_specs=[
            pl.BlockSpec((None, tt, d), lambda b, i: (b, i, 0)),
            _mod_spec(shift, d),
            _mod_spec(scale, d),
            pl.BlockSpec((d, LANE), lambda b, i: (0, 0)),
        ],
        out_specs=[
            pl.BlockSpec((None, tt, d), lambda b, i: (b, i, 0)),
            pl.BlockSpec((None, tt, LANE), lambda b, i: (b, i, 0)),
        ],
        out_shape=[
            jax.ShapeDtypeStruct((bsz, t, d), F32),
            jax.ShapeDtypeStruct((bsz, t, LANE), F32),
        ],
        compiler_params=_cparams("parallel", "parallel"),
        name="ln_modulate_router",
    )(x, shift, scale, rw)


def _mm_kernel(a_ref, b_ref, o_ref):
    o_ref[...] = _dot(a_ref[...], b_ref[...]).astype(o_ref.dtype)


def _pick(n, pref):
    for c in pref:
        if n % c == 0:
            return c
    return n


def matmul(a, b, layer, out_dtype=F32, tm=None, tn=None, name="matmul"):
    m, k = a.shape
    n = b.shape[2]
    tm = tm or _pick(m, (1024, 512, 256, 128))
    tn = tn or _pick(n, (768, 512, 256, 128))
    return pl.pallas_call(
        _mm_kernel,
        grid=(m // tm, n // tn),
        in_specs=[
            pl.BlockSpec((tm, k), lambda i, j: (i, 0)),
            pl.BlockSpec((None, k, tn), lambda i, j: (layer, 0, j)),
        ],
        out_specs=pl.BlockSpec((tm, tn), lambda i, j: (i, j)),
        out_shape=jax.ShapeDtypeStruct((m, n), out_dtype),
        compiler_params=_cparams("parallel", "parallel"),
        name=name,
    )(a, b)


HEAD_SHIFT = 6
assert (1 << HEAD_SHIFT) == A_HEAD and 2 * A_HEAD == LANE and SCAN_CHUNK == A_HEAD


def _head_block_ones():
    i = _iota((LANE, LANE), 0) >> HEAD_SHIFT
    j = _iota((LANE, LANE), 1) >> HEAD_SHIFT
    return (i == j).astype(BF16)


def _seg_sum(x):
    bd = _head_block_ones()
    outs = []
    for c in range(x.shape[1] // LANE):
        hi, mid, lo = _split3(x[:, c * LANE:(c + 1) * LANE])
        outs.append(_dot(hi, bd) + (_dot(mid, bd) + _dot(lo, bd)))
    return jnp.concatenate(outs, axis=1)


SCAN_FIELDS = 6


def _rwkv_prep_kernel(aw, gp, gw, tiles_per_seq,
                      x_ref, xp_ref, xn_ref, mu_ref, w0_ref, w2_ref, a0_ref, a2_ref, g2_ref,
                      kkp_ref, kap_ref, rkp_ref,
                      packf_o, packb_o, bonus_o, g_o):
    i = pl.program_id(0)
    x = x_ref[...]
    tt = x.shape[0]
    row = _iota(x.shape, 0)
    pos = i % tiles_per_seq
    prev_row = jnp.where(pos == 0, 0.0, xp_ref[SUBLANE - 1:SUBLANE, :])
    next_row = jnp.where(pos == tiles_per_seq - 1, 0.0, xn_ref[0:1, :])
    prev = jnp.where(row == 0, prev_row, pltpu.roll(x, 1, 0))
    nxt = jnp.where(row == tt - 1, next_row, pltpu.roll(x, tt - 1, 0))
    xs = x + mu_ref[...] * (0.5 * (prev + nxt) - x)

    r = xs[:, 0:aw]
    k = xs[:, aw:2 * aw]
    v = xs[:, 2 * aw:3 * aw]
    o = 3 * aw
    wlo = jnp.tanh(xs[:, o:o + LANE])
    alo = xs[:, o + LANE:o + 2 * LANE]
    glo = _sigmoid(xs[:, o + 2 * LANE:o + 2 * LANE + gp])
    lane = _iota((1, LANE), 1)
    exp_mhalf = float(np.exp(-0.5))

    g_o[...] = _dot3_split_rhs(glo, g2_ref)

    kk = k * kkp_ref[...]
    nrm = jnp.sqrt(_seg_sum(kk * kk))
    kk = kk / jnp.maximum(nrm, 1e-12)
    bonus_o[...] = _seg_sum(r * k * rkp_ref[...]) * v

    for d, pack_o in enumerate((packf_o, packb_o)):
        sel = ((lane >> HEAD_SHIFT) == d).astype(F32)
        wl = w0_ref[d:d + 1, :] + _dot3_split_rhs(wlo * sel, w2_ref)
        lw = -_sigmoid(wl) * exp_mhalf
        a = _sigmoid(a0_ref[d:d + 1, :] + _dot3_split_rhs(alo * sel, a2_ref))
        fields = (lw, kk, kk * a, k * (1.0 + (a - 1.0) * kap_ref[...]), r, v)
        for g in range(aw // gw):
            for f, arr in enumerate(fields):
                c0 = (g * SCAN_FIELDS + f) * gw
                pack_o[:, c0:c0 + gw] = arr[:, g * gw:(g + 1) * gw]


def scan_group_width(aw, max_pairs=6):
    n_pairs = aw // LANE
    return LANE * max(p for p in range(1, max_pairs + 1) if n_pairs % p == 0)


def rwkv_prep(u, seq_len, rw, aw, ap, tt=128):
    n = u.shape[0]
    gw = scan_group_width(aw)
    gp = ap - 3 * aw - 2 * LANE
    n8 = n // SUBLANE
    tpb = tt // SUBLANE
    full = lambda shape: pl.BlockSpec(shape, lambda i: (0,) * len(shape))
    tok = pl.BlockSpec((tt, aw), lambda i: (i, 0))
    pack = pl.BlockSpec((tt, SCAN_FIELDS * aw), lambda i: (i, 0))
    pack_shape = jax.ShapeDtypeStruct((n, SCAN_FIELDS * aw), F32)
    outs = pl.pallas_call(
        functools.partial(_rwkv_prep_kernel, aw, gp, gw, seq_len // tt),
        grid=(n // tt,),
        in_specs=[
            pl.BlockSpec((tt, ap), lambda i: (i, 0)),
            pl.BlockSpec((SUBLANE, ap), lambda i: (jnp.maximum(i * tpb - 1, 0), 0)),
            pl.BlockSpec((SUBLANE, ap), lambda i: (jnp.minimum((i + 1) * tpb, n8 - 1), 0)),
            full((1, ap)), full((2, aw)), full((2, LANE, aw)), full((2, aw)), full((2, LANE, aw)),
            full((2, gp, aw)), full((1, aw)), full((1, aw)), full((1, aw)),
        ],
        out_specs=[pack, pack, tok, tok],
        out_shape=[pack_shape, pack_shape] + [jax.ShapeDtypeStruct((n, aw), F32)] * 2,
        compiler_params=_cparams("parallel"),
        name="rwkv_prep",
    )(u, u, u, rw["mu"], rw["w0"], rw["w2"], rw["a0"], rw["a2"], rw["g2"],
      rw["k_k"], rw["k_a"], rw["r_k"])
    return outs


def _scan_chunks(rev, toks, sts, consts):
    tri, strict, incl, blk, eye, head0, t2, s2 = consts
    c = SCAN_CHUNK
    ident = (t2 == s2).astype(F32)
    zero = jnp.zeros((), BF16)
    cat = lambda *xs: jnp.concatenate(xs, axis=0)

    def each(f, *lists):
        return [f(*args) for args in zip(*lists)]

    def bd(x):
        xb = x.astype(BF16)
        return jnp.where(blk, cat(xb, xb), zero)

    lw, kk, bb, kd, r, v = (list(t) for t in zip(*toks))
    big_l = each(lambda a: _dot_exact_lhs(tri, a), lw)
    l_tot = each(lambda a: a[0:1, :] if rev else a[c - 1:c, :], big_l)
    lhs = each(lambda k_, r_, l_, w_: cat(-k_ * jnp.exp(l_ - w_), r_ * jnp.exp(l_)).astype(BF16),
               kk, r, big_l, lw)
    gi = each(lambda l_: jnp.exp(-l_), big_l)
    bt = each(lambda b_, g_: (b_ * g_).astype(BF16), bb, gi)
    kt = each(lambda k_, g_: (k_ * g_).astype(BF16), kd, gi)
    rhs = each(lambda b_, k_: cat(jnp.where(head0, b_, zero), jnp.where(head0, zero, b_),
                                  jnp.where(head0, k_, zero), jnp.where(head0, zero, k_)), bt, kt)
    p = each(_dot_nt, lhs, rhs)
    nab = each(lambda p_: jnp.where(strict, p_[0:c, 0:LANE], 0.0).astype(BF16), p)
    nrb = each(lambda p_: jnp.where(incl, p_[c:2 * c, 0:LANE], 0.0).astype(BF16), p)
    nkk = each(lambda p_: cat(jnp.where(strict, p_[0:c, LANE:2 * LANE], 0.0),
                              jnp.where(incl, p_[c:2 * c, LANE:2 * LANE], 0.0)).astype(BF16), p)
    z = each(lambda l_, s_: _dot(l_, s_.astype(BF16)), lhs, sts)
    w = each(lambda n_, v_: _dot(n_, bd(v_)), nkk, v)
    x = each(lambda z_, w_: z_[0:c] + w_[0:c], z, w)
    blk8 = (t2 >> 3) == (s2 >> 3)
    n0 = each(lambda n_: jnp.where(blk8, n_, zero), nab)
    inv = each(lambda n_: ident + n_.astype(F32), n0)
    pw = each(lambda n_: _dot(n_, bd(n_)), n0)
    inv = each(lambda i_, p_: i_ + _dot(i_.astype(BF16), bd(p_)), inv, pw)
    pw = each(lambda p_: _dot(p_.astype(BF16), bd(p_)), pw)
    inv = each(lambda i_, p_: i_ + _dot(i_.astype(BF16), bd(p_)), inv, pw)
    for lb in range(3, HEAD_SHIFT):
        new = ((t2 >> (lb + 1)) == (s2 >> (lb + 1))) & ((t2 >> lb) != (s2 >> lb))
        half = each(lambda i_, n_: _dot(i_.astype(BF16), bd(jnp.where(new, n_, zero))), inv, nab)
        inv = each(lambda i_, h_: i_ + _dot(h_.astype(BF16), bd(i_)), inv, half)
    x = each(lambda i_, x_: _dot(i_.astype(BF16), bd(x_)), inv, x)
    y = each(lambda z_, w_, n_, x_: z_[c:2 * c] + w_[c:2 * c] + _dot(n_, bd(x_)), z, w, nrb, x)
    gr = each(lambda t_, l_: jnp.exp(t_ - l_), l_tot, big_l)
    upd = each(lambda b_, k_, g_, x_, v_: _dot(cat(b_ * g_, k_ * g_).T.astype(BF16),
                                               cat(x_, v_).astype(BF16)), bb, kd, gr, x, v)
    g_col = each(lambda t_: jnp.sum(jnp.where(eye, jnp.exp(t_), 0.0), axis=1, keepdims=True), l_tot)
    st_new = each(lambda s_, g_, u_: s_ * g_ + jnp.where(blk, u_, 0.0), sts, g_col, upd)
    return y, st_new


def _scan_kernel(rev, npar, n_chunks, tok_ref, s0_ref, y_ref, st_out_ref, st_scr):
    ci = pl.program_id(2)

    @pl.when(ci == 0)
    def _():
        st_scr[...] = s0_ref[...]

    c = SCAN_CHUNK
    ti = _iota((c, c), 0)
    si = _iota((c, c), 1)
    tri = ((si >= ti) if rev else (si <= ti)).astype(BF16)
    t2 = _iota((c, LANE), 0)
    s2 = _iota((c, LANE), 1) & (c - 1)
    strict = (s2 > t2) if rev else (s2 < t2)
    incl = (s2 >= t2) if rev else (s2 <= t2)
    bi = _iota((LANE, LANE), 0)
    bj = _iota((LANE, LANE), 1)
    blk = (bi >> HEAD_SHIFT) == (bj >> HEAD_SHIFT)
    eye = bi == bj
    head0 = _iota((c, LANE), 1) < A_HEAD
    consts = (tri, strict, incl, blk, eye, head0, t2, s2)
    lanes = [slice(p * LANE, (p + 1) * LANE) for p in range(npar)]
    gw = npar * LANE
    toks = [tuple(tok_ref[:, f * gw + p * LANE:f * gw + (p + 1) * LANE] for f in range(SCAN_FIELDS))
            for p in range(npar)]
    ys, sts = _scan_chunks(rev, toks, [st_scr[p] for p in range(npar)], consts)
    for p, sl in enumerate(lanes):
        y_ref[:, sl] = ys[p]
        st_scr[p] = sts[p]

    @pl.when(ci == n_chunks - 1)
    def _():
        st_out_ref[...] = st_scr[...]


def rwkv_scan(packed, s0, rev):
    bsz, t, width = packed.shape
    aw = width // SCAN_FIELDS
    n_pairs = aw // LANE
    npar = scan_group_width(aw) // LANE
    n_chunks = t // SCAN_CHUNK
    if rev:
        tmap = lambda b, g, c: (b, n_chunks - 1 - c, g)
    else:
        tmap = lambda b, g, c: (b, c, g)
    tok = pl.BlockSpec((None, SCAN_CHUNK, npar * LANE), tmap)
    tok_in = pl.BlockSpec((None, SCAN_CHUNK, SCAN_FIELDS * npar * LANE), tmap)
    st = pl.BlockSpec((None, npar, LANE, LANE), lambda b, g, c: (b, g, 0, 0))
    return pl.pallas_call(
        functools.partial(_scan_kernel, rev, npar, n_chunks),
        grid=(bsz, n_pairs // npar, n_chunks),
        in_specs=[tok_in, st],
        out_specs=[tok, st],
        out_shape=[jax.ShapeDtypeStruct((bsz, t, aw), F32),
                   jax.ShapeDtypeStruct(s0.shape, F32)],
        scratch_shapes=[pltpu.VMEM((npar, LANE, LANE), F32)],
        compiler_params=_cparams("parallel", "parallel", "arbitrary"),
        name="rwkv_scan_bwd" if rev else "rwkv_scan_fwd",
    )(packed, s0)


def _rwkv_post_kernel(yf_ref, yb_ref, bonus_ref, g_ref, lnw_ref, lnb_ref, o_ref):
    y = yf_ref[...] + yb_ref[...]
    inv = 1.0 / A_HEAD
    mu = _seg_sum(y) * inv
    yc = y - mu
    var = _seg_sum(yc * yc) * inv
    yn = yc * lax.rsqrt(var + GN_EPS) * lnw_ref[...] + lnb_ref[...]
    o_ref[...] = ((yn + bonus_ref[...]) * g_ref[...]).astype(o_ref.dtype)


def rwkv_post(yf, yb, bonus, g, ln_w, ln_b, tt=256):
    n, aw = yf.shape
    tok = pl.BlockSpec((tt, aw), lambda i: (i, 0))
    par = pl.BlockSpec((1, aw), lambda i: (0, 0))
    return pl.pallas_call(
        _rwkv_post_kernel,
        grid=(n // tt,),
        in_specs=[tok, tok, tok, tok, par, par],
        out_specs=tok,
        out_shape=jax.ShapeDtypeStruct((n, aw), BF16),
        compiler_params=_cparams("parallel"),
        name="rwkv_post",
    )(yf, yb, bonus, g, ln_w, ln_b)


def round_up(x, m):
    return (x + m - 1) // m * m


ATT_SCALE = HEAD_DIM ** -0.5


def _ctx_attn_kernel(q_ref, k_ref, v_ref, sink_ref, o_ref):
    s = _dot3_nt(q_ref[...] * ATT_SCALE, k_ref[...])
    sk = sink_ref[0:1, 0:1]
    m = jnp.maximum(jnp.max(s, axis=-1, keepdims=True), sk)
    e = jnp.exp(s - m)
    den = jnp.sum(e, axis=-1, keepdims=True) + jnp.exp(sk - m)
    o = _dot(e.astype(BF16), v_ref[...].astype(BF16))
    o_ref[...] = (o / den).astype(o_ref.dtype)


def ctx_attention(u, bsz, t, q0, k0, v0, n_heads, group, sink):
    sink = jnp.full((n_heads,), NEG, F32) if sink is None else sink.astype(F32)
    sink = jnp.broadcast_to(sink[:, None, None], (n_heads, 1, LANE))
    return pl.pallas_call(
        _ctx_attn_kernel,
        grid=(bsz, n_heads),
        in_specs=[
            pl.BlockSpec((t, HEAD_DIM), lambda b, h: (b, q0 + h)),
            pl.BlockSpec((t, HEAD_DIM), lambda b, h: (b, k0 + h // group)),
            pl.BlockSpec((t, HEAD_DIM), lambda b, h: (b, v0 + h // group)),
            pl.BlockSpec((None, 1, LANE), lambda b, h: (h, 0, 0)),
        ],
        out_specs=pl.BlockSpec((t, HEAD_DIM), lambda b, h: (b, h)),
        out_shape=jax.ShapeDtypeStruct((bsz * t, n_heads * HEAD_DIM), BF16),
        compiler_params=_cparams("parallel", "parallel"),
        name="ctx_attention",
    )(u, u, u, sink)


def rope_tables(n_tok):
    half = HEAD_DIM // 2
    quarter = half // 2
    tok = jnp.arange(n_tok)
    row = (tok // GRID_W).astype(F32)
    col = (tok % GRID_W).astype(F32)
    inv = ROPE_THETA ** (-jnp.arange(quarter, dtype=F32) / quarter)
    ang_r = row[:, None] * inv[None]
    ang_c = col[:, None] * inv[None]
    cos = jnp.concatenate([jnp.cos(ang_r)] * 2 + [jnp.cos(ang_c)] * 2, axis=1)
    sr, sc = jnp.sin(ang_r), jnp.sin(ang_c)
    z = jnp.zeros_like(sr)
    sin_a = jnp.concatenate([-sr, z, -sc, z], axis=1)
    sin_b = jnp.concatenate([z, sr, z, sc], axis=1)
    return cos, sin_a, sin_b


def _rope_kernel(n_heads, x_ref, cos_ref, sa_ref, sb_ref, o_ref):
    q = HEAD_DIM // 4
    cos, sa, sb = cos_ref[...], sa_ref[...], sb_ref[...]
    for h in range(n_heads):
        sl = slice(h * HEAD_DIM, (h + 1) * HEAD_DIM)
        x = x_ref[:, sl]
        o_ref[:, sl] = x * cos + pltpu.roll(x, HEAD_DIM - q, 1) * sa + pltpu.roll(x, q, 1) * sb


def rope(u, bsz, n_tok, col0, n_heads, tt=256):
    cos, sin_a, sin_b = rope_tables(n_tok)
    nt = n_tok // tt
    hb = max(k for k in range(1, n_heads + 1) if n_heads % k == 0 and col0 % k == 0)
    width = hb * HEAD_DIM
    cb = col0 // hb
    tab = pl.BlockSpec((tt, HEAD_DIM), lambda b, i, j: (i, 0))
    return pl.pallas_call(
        functools.partial(_rope_kernel, hb),
        grid=(bsz, nt, n_heads // hb),
        in_specs=[pl.BlockSpec((tt, width), lambda b, i, j: (b * nt + i, cb + j)), tab, tab, tab],
        out_specs=pl.BlockSpec((tt, width), lambda b, i, j: (b * nt + i, j)),
        out_shape=jax.ShapeDtypeStruct((bsz * n_tok, n_heads * HEAD_DIM), F32),
        compiler_params=_cparams("parallel", "parallel", "parallel"),
        name="rope",
    )(u, cos, sin_a, sin_b)


def _win_attn_kernel(group, n_tok, sink_ref, q_ref, kp_ref, kc_ref, kn_ref, vp_ref, vc_ref, vn_ref,
                     ck_ref, cv_ref, o_ref):
    kv = pl.program_id(1)
    i = pl.program_id(2)
    rows = group * QBLK
    q = jnp.concatenate([q_ref[:, g * HEAD_DIM:(g + 1) * HEAD_DIM] for g in range(group)], axis=0)
    q = q * ATT_SCALE
    k_loc = jnp.concatenate([kp_ref[...], kc_ref[...], kn_ref[...]], axis=0)
    v_loc = jnp.concatenate([vp_ref[...], vc_ref[...], vn_ref[...]], axis=0).astype(BF16)
    s_loc = _dot3_nt(q, k_loc)
    qi = _iota(s_loc.shape, 0) & (QBLK - 1)
    kj = _iota(s_loc.shape, 1)
    pos = i * QBLK - WIN + kj
    ok = (kj - qi >= 0) & (kj - qi <= 2 * WIN) & (pos >= 0) & (pos < n_tok)
    s_loc = jnp.where(ok, s_loc, NEG)
    s_ctx = _dot3_nt(q, ck_ref[...])
    rg = _iota((rows, 1), 0) >> int(np.log2(QBLK))
    sk = jnp.zeros((rows, 1), F32)
    for g in range(group):
        sk = jnp.where(rg == g, sink_ref[kv * group + g], sk)
    m = jnp.maximum(jnp.maximum(jnp.max(s_loc, axis=-1, keepdims=True),
                                jnp.max(s_ctx, axis=-1, keepdims=True)), sk)
    e_loc = jnp.exp(s_loc - m)
    e_ctx = jnp.exp(s_ctx - m)
    den = (jnp.sum(e_loc, axis=-1, keepdims=True) + jnp.sum(e_ctx, axis=-1, keepdims=True)
           + jnp.exp(sk - m))
    o = (_dot(e_loc.astype(BF16), v_loc) + _dot(e_ctx.astype(BF16), cv_ref[...].astype(BF16))) / den
    for g in range(group):
        o_ref[:, g * HEAD_DIM:(g + 1) * HEAD_DIM] = o[g * QBLK:(g + 1) * QBLK].astype(o_ref.dtype)


def window_attention(qk_rot, u, vb0, bsz, n_tok, n_kv, group, ck, cv, sink):
    assert WIN == QBLK
    nb = n_tok // QBLK
    nq = n_kv * group
    past = ck.shape[2]
    prv = lambda i: jnp.maximum(i - 1, 0)
    nxt = lambda i: jnp.minimum(i + 1, nb - 1)
    blk = lambda f, c0: pl.BlockSpec((QBLK, HEAD_DIM), lambda b, kv, i, s: (b * nb + f(i), c0 + kv))
    same = lambda i: i
    cache = pl.BlockSpec((None, None, past, HEAD_DIM), lambda b, kv, i, s: (b, kv, 0, 0))
    grid_spec = pltpu.PrefetchScalarGridSpec(
        num_scalar_prefetch=1,
        grid=(bsz, n_kv, nb),
        in_specs=[
            pl.BlockSpec((QBLK, group * HEAD_DIM), lambda b, kv, i, s: (b * nb + i, kv)),
            blk(prv, nq), blk(same, nq), blk(nxt, nq),
            blk(prv, vb0), blk(same, vb0), blk(nxt, vb0),
            cache, cache,
        ],
        out_specs=pl.BlockSpec((QBLK, group * HEAD_DIM), lambda b, kv, i, s: (b * nb + i, kv)),
    )
    return pl.pallas_call(
        functools.partial(_win_attn_kernel, group, n_tok),
        grid_spec=grid_spec,
        out_shape=jax.ShapeDtypeStruct((bsz * n_tok, nq * HEAD_DIM), BF16),
        compiler_params=_cparams("parallel", "parallel", "parallel"),
        name="window_attention",
    )(sink.astype(F32), qk_rot, qk_rot, qk_rot, qk_rot, u, u, u, ck, cv)


NB_QROWS = 4
NB_QTOK = NB_QROWS * GRID_W
NB_KTOK = 3 * NB_QTOK


def nb_tables(rpb, rows):
    kh = min(NB_ROWS, rows)
    ql = np.arange(NB_QTOK)
    kl = np.arange(NB_KTOK)
    r_rel, c = ql // GRID_W, ql % GRID_W
    kr_rel, kc = kl // GRID_W - NB_QROWS, kl % GRID_W
    n_heads = rpb.shape[0]
    gcol = np.arange(GRID_W)
    col_idx = np.clip(gcol[None, :] - gcol[:, None] + NB_COLS - 1, 0, 2 * NB_COLS - 2)
    t_col = jnp.take(rpb.astype(F32), jnp.asarray(col_idx.reshape(-1)), axis=2)
    t_col = t_col.reshape(n_heads, 2 * NB_ROWS - 1, GRID_W * GRID_W)
    qr = np.arange(NB_QROWS)
    krr = np.arange(3 * NB_QROWS) - NB_QROWS
    row_idx = np.clip(krr[None, :] - qr[:, None] + NB_ROWS - 1, 0, 2 * NB_ROWS - 2)
    bias = jnp.take(t_col, jnp.asarray(row_idx.reshape(-1)), axis=1)
    bias = bias.reshape(n_heads, NB_QROWS, 3 * NB_QROWS, GRID_W, GRID_W)
    bias = jnp.transpose(bias, (0, 1, 3, 2, 4)).reshape(n_heads, NB_QTOK, NB_KTOK)
    win_start = np.clip(c - NB_COLS // 2, 0, GRID_W - NB_COLS)
    col_ok = (kc[None, :] >= win_start[:, None]) & (kc[None, :] < win_start[:, None] + NB_COLS)
    masks = []
    for j in range(rows // NB_QROWS):
        r = j * NB_QROWS + r_rel
        kr = j * NB_QROWS + kr_rel
        row_start = np.clip(r - kh // 2, 0, rows - kh)
        row_ok = (kr[None, :] >= row_start[:, None]) & (kr[None, :] < row_start[:, None] + kh)
        masks.append(row_ok & col_ok)
    return bias, jnp.asarray(np.stack(masks).astype(np.float32))


def _nb_attn_kernel(q_ref, kp_ref, kc_ref, kn_ref, vp_ref, vc_ref, vn_ref, ck_ref, cv_ref,
                    bias_ref, mask_ref, o_ref):
    q = q_ref[...] * ATT_SCALE
    k_loc = jnp.concatenate([kp_ref[...], kc_ref[...], kn_ref[...]], axis=0)
    v_loc = jnp.concatenate([vp_ref[...], vc_ref[...], vn_ref[...]], axis=0).astype(BF16)
    s_loc = jnp.where(mask_ref[...] > 0.0, _dot3_nt(q, k_loc) + bias_ref[...], NEG)
    s_ctx = _dot3_nt(q, ck_ref[...])
    m = jnp.maximum(jnp.max(s_loc, axis=-1, keepdims=True), jnp.max(s_ctx, axis=-1, keepdims=True))
    e_loc = jnp.exp(s_loc - m)
    e_ctx = jnp.exp(s_ctx - m)
    den = jnp.sum(e_loc, axis=-1, keepdims=True) + jnp.sum(e_ctx, axis=-1, keepdims=True)
    o = _dot(e_loc.astype(BF16), v_loc) + _dot(e_ctx.astype(BF16), cv_ref[...].astype(BF16))
    o_ref[...] = (o / den).astype(o_ref.dtype)


def neighbourhood_attention(u, cols, bsz, n_tok, n_heads, ck, cv, rpb):
    q0, k0, v0 = cols
    rows = n_tok // GRID_W
    assert rows % NB_QROWS == 0 and rows >= NB_ROWS
    nj = rows // NB_QROWS
    past = ck.shape[2]
    bias, mask = nb_tables(rpb, rows)
    prv = lambda j: jnp.maximum(j - 1, 0)
    nxt = lambda j: jnp.minimum(j + 1, nj - 1)
    same = lambda j: j
    blk = lambda f, c0: pl.BlockSpec((NB_QTOK, HEAD_DIM), lambda b, h, j: (b * nj + f(j), c0 + h))
    cache = pl.BlockSpec((None, None, past, HEAD_DIM), lambda b, h, j: (b, h, 0, 0))
    return pl.pallas_call(
        _nb_attn_kernel,
        grid=(bsz, n_heads, nj),
        in_specs=[
            blk(same, q0),
            blk(prv, k0), blk(same, k0), blk(nxt, k0),
            blk(prv, v0), blk(same, v0), blk(nxt, v0),
            cache, cache,
            pl.BlockSpec((None, NB_QTOK, NB_KTOK), lambda b, h, j: (h, 0, 0)),
            pl.BlockSpec((None, NB_QTOK, NB_KTOK), lambda b, h, j: (j, 0, 0)),
        ],
        out_specs=pl.BlockSpec((NB_QTOK, HEAD_DIM), lambda b, h, j: (b * nj + j, h)),
        out_shape=jax.ShapeDtypeStruct((bsz * n_tok, n_heads * HEAD_DIM), BF16),
        compiler_params=_cparams("parallel", "parallel", "parallel"),
        name="neighbourhood_attention",
    )(u, u, u, u, u, u, u, ck, cv, bias, mask)


def _merge_kernel(ya_ref, yb_ref, yc_ref, wa_ref, wb_ref, wc_ref, ga_ref, gb_ref, gc_ref, o_ref):
    m = _sigmoid(ga_ref[...]) * _dot(ya_ref[...], wa_ref[...])
    m = m + _sigmoid(gb_ref[...]) * _dot(yb_ref[...], wb_ref[...])
    m = m + _sigmoid(gc_ref[...]) * _dot(yc_ref[...], wc_ref[...])
    o_ref[...] = m.astype(o_ref.dtype)


def merge_branches(ya, yb, yc, wa, wb, wc, u, gate_col0, tm=512, tn=512):
    n, aw = ya.shape
    bw, cw = wb.shape[0], wc.shape[0]
    d = wa.shape[1]
    tm = _pick(n, (tm, 256, 128))
    tn = _pick(d, (tn, 256, 128))
    assert gate_col0 % tn == 0
    g0 = gate_col0 // tn
    nd = d // tn
    gate = lambda gi: pl.BlockSpec((tm, tn), lambda i, j: (i, g0 + gi * nd + j))
    return pl.pallas_call(
        _merge_kernel,
        grid=(n // tm, nd),
        in_specs=[
            pl.BlockSpec((tm, aw), lambda i, j: (i, 0)),
            pl.BlockSpec((tm, bw), lambda i, j: (i, 0)),
            pl.BlockSpec((tm, cw), lambda i, j: (i, 0)),
            pl.BlockSpec((aw, tn), lambda i, j: (0, j)),
            pl.BlockSpec((bw, tn), lambda i, j: (0, j)),
            pl.BlockSpec((cw, tn), lambda i, j: (0, j)),
            gate(0), gate(1), gate(2),
        ],
        out_specs=pl.BlockSpec((tm, tn), lambda i, j: (i, j)),
        out_shape=jax.ShapeDtypeStruct((n, d), BF16),
        compiler_params=_cparams("parallel", "parallel"),
        name="merge_branches",
    )(ya, yb, yc, wa, wb, wc, u, u, u)


def _post_norm(alpha, x, gate, y, g, b):
    r = alpha * x + gate * y
    return _ln_rows(r) * g + b


def _outproj_ln_kernel(alpha, nk, m_ref, w_ref, x_ref, gate_ref, g_ref, b_ref, o_ref):
    k = pl.program_id(1)

    @pl.when(k == 0)
    def _():
        o_ref[...] = _dot(m_ref[...], w_ref[...])

    @pl.when(k > 0)
    def _():
        o_ref[...] += _dot(m_ref[...], w_ref[...])

    @pl.when(k == nk - 1)
    def _():
        rows = o_ref.shape[0]
        step = min(rows, 128)
        for r0 in range(0, rows, step):
            sl = slice(r0, r0 + step)
            o_ref[sl, :] = _post_norm(alpha, x_ref[sl, :], gate_ref[...], o_ref[sl, :],
                                      g_ref[...], b_ref[...])


def _tile_mod_spec(mod, d, tm, seq_len):
    if mod.shape[0] == 1:
        return pl.BlockSpec((None, 1, d), lambda i, *_: (0, 0, 0))
    return pl.BlockSpec((None, 1, d), lambda i, *_: (i * tm // seq_len, 0, 0))


def outproj_postnorm(m, w_out, x, gate, ln_g, ln_b, alpha, seq_len, tm=512, tk=512):
    n, d = x.shape
    kdim = m.shape[1]
    tm = _pick(n if gate.shape[0] == 1 else seq_len, (tm, 256, 128))
    tk = _pick(kdim, (tk, 256, 128))
    nk = kdim // tk
    vec = pl.BlockSpec((1, d), lambda i, k: (0, 0))
    return pl.pallas_call(
        functools.partial(_outproj_ln_kernel, alpha, nk),
        grid=(n // tm, nk),
        in_specs=[
            pl.BlockSpec((tm, tk), lambda i, k: (i, k)),
            pl.BlockSpec((tk, d), lambda i, k: (k, 0)),
            pl.BlockSpec((tm, d), lambda i, k: (i, 0)),
            _tile_mod_spec(gate, d, tm, seq_len),
            vec, vec,
        ],
        out_specs=pl.BlockSpec((tm, d), lambda i, k: (i, 0)),
        out_shape=jax.ShapeDtypeStruct((n, d), F32),
        compiler_params=_cparams("parallel", "arbitrary"),
        name="outproj_postnorm",
    )(m, w_out, x, gate, ln_g, ln_b)


def _select_kernel(cap, n, aff_ref, mask_ref):
    bits = lax.bitcast_convert_type(aff_ref[...], jnp.int32)
    n_exp = bits.shape[0]
    capf = jnp.float32(cap)

    def count(pred):
        return jnp.sum(pred.astype(F32), axis=1, keepdims=True)

    def value_step(_, carry):
        lo, hi = carry
        mid = lo + ((hi - lo + 1) >> 1)
        ok = count(bits >= mid) >= capf
        return jnp.where(ok, mid, lo), jnp.where(ok, hi, mid - 1)

    lo0 = jnp.zeros((n_exp, 1), jnp.int32)
    hi0 = jnp.full((n_exp, 1), 0x7F800000, jnp.int32)
    thr, _ = lax.fori_loop(0, 32, value_step, (lo0, hi0))
    gt = bits > thr
    eq = bits == thr
    need = capf - count(gt)
    idx = _iota(bits.shape, 1)

    def index_step(_, carry):
        lo, hi = carry
        mid = (lo + hi) >> 1
        ok = count(eq & (idx < mid)) >= need
        return jnp.where(ok, lo, mid), jnp.where(ok, mid, hi)

    _, bound = lax.fori_loop(0, int(np.ceil(np.log2(n))) + 1, index_step,
                             (jnp.zeros((n_exp, 1), jnp.int32), jnp.full((n_exp, 1), n, jnp.int32)))
    mask_ref[...] = (gt | (eq & (idx < bound))).astype(jnp.int32)


def expert_choice_mask(aff_t, cap):
    n_exp, n = aff_t.shape
    return pl.pallas_call(
        functools.partial(_select_kernel, cap, n),
        out_shape=jax.ShapeDtypeStruct((n_exp, n), jnp.int32),
        compiler_params=pltpu.CompilerParams(vmem_limit_bytes=VMEM_LIMIT_BYTES),
        name="expert_choice_mask",
    )(aff_t)


def _count_le(sorted_vals, x):
    return jnp.sum((sorted_vals[None, :] <= x[:, None]).astype(jnp.int32), axis=1)


def _visit_list(nvis, first_blk, n_work):
    na, nb = nvis.shape
    flat = nvis.reshape(-1)
    off_end = jnp.cumsum(flat)
    off_start = off_end - flat
    total = off_end[-1]
    w = jnp.minimum(jnp.arange(n_work, dtype=jnp.int32), total - 1)
    idx = jnp.minimum(_count_le(off_end, w), na * nb - 1)
    blk = first_blk.reshape(-1)[idx] + (w - off_start[idx])
    valid = (jnp.arange(n_work) < total).astype(jnp.int32)
    return idx // nb, idx % nb, blk.astype(jnp.int32), valid


MOE_GROUP = 8
NO_SLOT = 1 << 28


def _group_items(key, valid, n_keys, group, n_steps):
    n_items = key.shape[0]
    cnt = jnp.zeros((n_keys,), jnp.int32).at[key].add(valid)
    start = jnp.cumsum(cnt) - cnt
    per_key = (cnt + group - 1) // group
    s_end = jnp.cumsum(per_key)
    s_start = s_end - per_key
    total = s_end[-1]
    step = jnp.arange(n_steps, dtype=jnp.int32)
    real = step < total
    sc = jnp.minimum(step, total - 1)
    k = jnp.minimum(_count_le(s_end, sc), n_keys - 1)
    j = sc - s_start[k]
    within = j[None, :] * group + jnp.arange(group, dtype=jnp.int32)[:, None]
    ok = (within < cnt[k][None, :]) & real[None, :]
    item = jnp.clip(start[k][None, :] + jnp.minimum(within, cnt[k][None, :] - 1), 0, n_items - 1)
    first = ((j == 0) & real).astype(jnp.int32)
    last = ((j == per_key[k] - 1) & real).astype(jnp.int32)
    return item, ok.astype(jnp.int32), first, last


def routing_plan(mask, cap):
    n_exp, n = mask.shape
    nsb = cap // SLOT_BLK
    pos = jnp.cumsum(mask, axis=1) - mask
    posm = jnp.where(mask > 0, pos, -1).astype(jnp.int32)

    def tile_visits(tile):
        cnt = mask.reshape(n_exp, n // tile, tile).sum(-1)
        cend = jnp.cumsum(cnt, axis=1)
        lo = jnp.minimum((cend - cnt) // SLOT_BLK, nsb - 1)
        return jnp.where(cnt > 0, (cend - 1) // SLOT_BLK - lo + 1, 0), lo

    n_blk = n_exp * nsb
    grp = MOE_GROUP
    nvis, sb_lo = tile_visits(ROUTE_TILE)
    n_work = n_exp * (nsb + n // ROUTE_TILE)
    e, t, sb, valid = _visit_list(nvis, sb_lo, n_work)
    blk = e * nsb + sb
    item, ok, first, last = _group_items(blk, valid, n_blk, grp, n_work // grp + n_blk + 1)
    blk_g = blk[item[0]]
    dispatch = (blk_g // nsb, blk_g, first, last, t[item], jnp.where(ok > 0, sb[item] * SLOT_BLK, NO_SLOT))
    nvis, sb_lo = tile_visits(TOK_TILE)
    nt = n // TOK_TILE
    n_work = n_exp * (nsb + nt)
    nvis_t = nvis.T.at[:, 0].max(1)
    t2, e2, sb2, valid2 = _visit_list(nvis_t, sb_lo.T, n_work + nt)
    item, ok, first, last = _group_items(t2, valid2, nt, grp, (n_work + nt) // grp + nt + 1)
    combine = (t2[item[0]], first, last, e2[item], e2[item] * nsb + sb2[item],
               jnp.where(ok > 0, sb2[item] * SLOT_BLK, NO_SLOT))
    return posm, dispatch, combine


def _route_kernel(group, e_ref, blk_ref, first_ref, last_ref, tile_ref, base_ref, *refs):
    posm_refs, aff_refs = refs[:group], refs[group:2 * group]
    o_ref, acc_ref = refs[2 * group:]
    w = pl.program_id(0)
    rows = _iota((SLOT_BLK, ROUTE_TILE), 0)

    def total():
        tot = None
        for q in range(group):
            onehot = ((rows + base_ref[q, w]) == posm_refs[q][...]).astype(BF16)
            part = _dot_exact_lhs(onehot, aff_refs[q][...])
            tot = part if tot is None else tot + part
        return tot

    @pl.when(first_ref[w] == 1)
    def _():
        acc_ref[...] = total()

    @pl.when((first_ref[w] == 0) & (base_ref[0, w] != NO_SLOT))
    def _():
        acc_ref[...] += total()

    @pl.when(last_ref[w] == 1)
    def _():
        o_ref[...] = acc_ref[...]


def moe_route(aff, posm, plan, cap):
    n = aff.shape[0]
    n_exp = posm.shape[0]
    grp = MOE_GROUP
    n_steps = plan[0].shape[0]
    pos_spec = lambda q: pl.BlockSpec((None, 1, ROUTE_TILE), lambda w, e, b, f, l, t, s: (e[w], 0, t[q, w]))
    aff_spec = lambda q: pl.BlockSpec((ROUTE_TILE, LANE), lambda w, e, b, f, l, t, s: (t[q, w], 0))
    grid_spec = pltpu.PrefetchScalarGridSpec(
        num_scalar_prefetch=6,
        grid=(n_steps,),
        in_specs=[pos_spec(q) for q in range(grp)] + [aff_spec(q) for q in range(grp)],
        out_specs=pl.BlockSpec((SLOT_BLK, LANE), lambda w, e, b, f, l, t, s: (b[w], 0)),
        scratch_shapes=[pltpu.VMEM((SLOT_BLK, LANE), F32)],
    )
    posm3 = posm.reshape(n_exp, 1, n)
    return pl.pallas_call(
        functools.partial(_route_kernel, grp),
        grid_spec=grid_spec,
        out_shape=jax.ShapeDtypeStruct((n_exp * cap, LANE), F32),
        compiler_params=_cparams("arbitrary"),
        name="moe_route",
    )(*plan, *([posm3] * grp), *([aff] * grp))


GATHER_ROWS = 256


def _gather_kernel(idx_ref, tok_hbm, o_ref, buf, sem):
    base = pl.program_id(0) * GATHER_ROWS

    def row_copy(r):
        return pltpu.make_async_copy(tok_hbm.at[pl.ds(idx_ref[base + r], 1), :],
                                     buf.at[pl.ds(r, 1), :], sem)

    def start(g, carry):
        for j in range(SUBLANE):
            row_copy(g * SUBLANE + j).start(priority=j % 2)
        return carry

    def wait(r, carry):
        row_copy(r).wait()
        return carry

    lax.fori_loop(0, GATHER_ROWS // SUBLANE, start, 0)
    lax.fori_loop(0, GATHER_ROWS, wait, 0, unroll=8)
    o_ref[...] = buf[...].astype(o_ref.dtype)


def moe_gather(tokens, idx):
    n, d = tokens.shape
    rows = idx.shape[0]
    assert rows % GATHER_ROWS == 0
    grid_spec = pltpu.PrefetchScalarGridSpec(
        num_scalar_prefetch=1,
        grid=(rows // GATHER_ROWS,),
        in_specs=[pl.BlockSpec(memory_space=pl.ANY)],
        out_specs=pl.BlockSpec((GATHER_ROWS, d), lambda i, idx_ref: (i, 0)),
        scratch_shapes=[pltpu.VMEM((GATHER_ROWS, d), F32), pltpu.SemaphoreType.DMA(())],
    )
    return pl.pallas_call(
        _gather_kernel,
        grid_spec=grid_spec,
        out_shape=jax.ShapeDtypeStruct((rows, d), BF16),
        compiler_params=_cparams("arbitrary"),
        name="moe_gather",
    )(idx, tokens)


def _ffn_up_kernel(x_ref, w1_ref, w3_ref, o_ref):
    x = x_ref[...]
    a = _dot(x, w1_ref[...])
    o_ref[...] = (a * _sigmoid(a) * _dot(x, w3_ref[...])).astype(o_ref.dtype)


def _ffn_down_kernel(h_ref, w2_ref, aff_ref, o_ref):
    aff = aff_ref[...]
    gval = jnp.sum(jnp.where(_iota(aff.shape, 1) == pl.program_id(0), aff, 0.0), axis=1, keepdims=True)
    o_ref[...] = (_dot(h_ref[...], w2_ref[...]) * gval).astype(o_ref.dtype)


def expert_ffn(xe, aff_rows, w1, w3, w2, layer, cap, tn=512):
    _, n_exp, d, ff = w1.shape
    tm = _pick(cap, (1024, 512, 256, 128))
    nm = cap // tm
    tf = _pick(ff, (tn, 256, 128))
    hid = pl.pallas_call(
        _ffn_up_kernel,
        grid=(n_exp, nm, ff // tf),
        in_specs=[
            pl.BlockSpec((tm, d), lambda e, i, j: (e * nm + i, 0)),
            pl.BlockSpec((None, None, d, tf), lambda e, i, j: (layer, e, 0, j)),
            pl.BlockSpec((None, None, d, tf), lambda e, i, j: (layer, e, 0, j)),
        ],
        out_specs=pl.BlockSpec((tm, tf), lambda e, i, j: (e * nm + i, j)),
        out_shape=jax.ShapeDtypeStruct((n_exp * cap, ff), BF16),
        compiler_params=_cparams("parallel", "parallel", "parallel"),
        name="expert_ffn_up",
    )(xe, w1, w3)
    td = _pick(d, (tn, 256, 128))
    return pl.pallas_call(
        _ffn_down_kernel,
        grid=(n_exp, nm, d // td),
        in_specs=[
            pl.BlockSpec((tm, ff), lambda e, i, j: (e * nm + i, 0)),
            pl.BlockSpec((None, None, ff, td), lambda e, i, j: (layer, e, 0, j)),
            pl.BlockSpec((tm, LANE), lambda e, i, j: (e * nm + i, 0)),
        ],
        out_specs=pl.BlockSpec((tm, td), lambda e, i, j: (e * nm + i, j)),
        out_shape=jax.ShapeDtypeStruct((n_exp * cap, d), BF16),
        compiler_params=_cparams("parallel", "parallel", "parallel"),
        name="expert_ffn_down",
    )(hid, w2, aff_rows)


def _combine_kernel(alpha, group, tile_ref, first_ref, last_ref, e_ref, blk_ref, base_ref,
                    posm_ref, *refs):
    ye_refs = refs[:group]
    x_ref, gate_ref, g_ref, b_ref, o_ref, acc_ref = refs[group:]
    w = pl.program_id(0)
    posm = posm_ref[...].astype(F32)
    lane = _iota(posm.shape, 1)
    cols = _iota((TOK_TILE, SLOT_BLK), 1)

    def total():
        onehots = []
        for q in range(group):
            col = jnp.sum(jnp.where(lane == e_ref[q, w], posm, 0.0), axis=1, keepdims=True)
            onehots.append(((cols + base_ref[q, w]).astype(F32) == col).astype(BF16))
        return _dot(jnp.concatenate(onehots, axis=1),
                    jnp.concatenate([r[...] for r in ye_refs], axis=0))

    @pl.when(first_ref[w] == 1)
    def _():
        acc_ref[...] = total()

    @pl.when((first_ref[w] == 0) & (base_ref[0, w] != NO_SLOT))
    def _():
        acc_ref[...] += total()

    @pl.when(last_ref[w] == 1)
    def _():
        o_ref[...] = _post_norm(alpha, x_ref[...], gate_ref[...], acc_ref[...], g_ref[...], b_ref[...])


def moe_combine_postnorm(ye, posm_t, plan, x, gate, ln_g, ln_b, alpha, seq_len):
    n, d = x.shape
    n_exp = posm_t.shape[1]
    grp = MOE_GROUP
    n_steps = plan[0].shape[0]
    assert seq_len % TOK_TILE == 0
    tile = lambda w, t, *_: (t[w], 0)
    if gate.shape[0] == 1:
        gate_map = lambda w, t, *_: (0, 0, 0)
    else:
        gate_map = lambda w, t, *_: (t[w] * TOK_TILE // seq_len, 0, 0)
    vec = pl.BlockSpec((1, d), lambda w, *_: (0, 0))
    ye_spec = lambda q: pl.BlockSpec((SLOT_BLK, d), lambda w, t, f, l, e, b, s: (b[q, w], 0))
    grid_spec = pltpu.PrefetchScalarGridSpec(
        num_scalar_prefetch=6,
        grid=(n_steps,),
        in_specs=([pl.BlockSpec((TOK_TILE, n_exp), tile)] + [ye_spec(q) for q in range(grp)]
                  + [pl.BlockSpec((TOK_TILE, d), tile), pl.BlockSpec((None, 1, d), gate_map), vec, vec]),
        out_specs=pl.BlockSpec((TOK_TILE, d), tile),
        scratch_shapes=[pltpu.VMEM((TOK_TILE, d), F32)],
    )
    return pl.pallas_call(
        functools.partial(_combine_kernel, alpha, grp),
        grid_spec=grid_spec,
        out_shape=jax.ShapeDtypeStruct((n, d), F32),
        compiler_params=_cparams("arbitrary"),
        name="moe_combine_postnorm",
    )(*plan, posm_t, *([ye] * grp), x, gate, ln_g, ln_b)


def ec_moe_postnorm(x, mods_shift, mods_scale, gate, router_w, w1, w3, w2, layer, ln_g, ln_b, alpha):
    bsz, t, d = x.shape
    n = bsz * t
    n_exp = router_w.shape[1]
    cap = max(1, CAP_FACTOR * n // n_exp)
    assert cap % SLOT_BLK == 0 and n % TOK_TILE == 0 and n % ROUTE_TILE == 0
    h, aff = ln_modulate_router(x, mods_shift, mods_scale, router_w)
    h = h.reshape(n, d)
    aff = aff.reshape(n, LANE)
    mask = expert_choice_mask(aff[:, :n_exp].T, cap)
    posm, dispatch, combine = routing_plan(mask, cap)
    aff_rows = moe_route(aff, posm, dispatch, cap)
    slot_token = jnp.clip(aff_rows[:, n_exp].astype(jnp.int32), 0, n - 1)
    xe = moe_gather(h, slot_token)
    ye = expert_ffn(xe, aff_rows, w1, w3, w2, layer, cap)
    out = moe_combine_postnorm(ye, posm.T, combine, x.reshape(n, d), gate, ln_g, ln_b, alpha, t)
    return out.reshape(bsz, t, d)


def prep_rwkv_params(rw, aw):
    a_cols = rw["mu"].shape[0]
    ap = round_up(a_cols, LANE)
    gp = ap - 3 * aw - 2 * LANE
    assert rw["w2"].shape[:2] == (2, LANE // 2) and rw["a2"].shape[:2] == (2, LANE // 2)
    assert gp >= rw["g2"].shape[0] and aw % LANE == 0
    row = lambda a: a.reshape(1, aw)
    return dict(
        mu=jnp.pad(rw["mu"], (0, ap - a_cols)).reshape(1, ap),
        w0=rw["w0"], w2=_hi_lo(rw["w2"].reshape(LANE, aw)), a0=rw["a0"],
        a2=_hi_lo(rw["a2"].reshape(LANE, aw)),
        g2=_hi_lo(jnp.pad(rw["g2"], ((0, gp - rw["g2"].shape[0]), (0, 0)))),
        k_k=row(rw["k_k"]), k_a=row(rw["k_a"]), r_k=row(rw["r_k"]),
        ln_w=row(rw["ln_w"]), ln_b=row(rw["ln_b"]), ap=ap, aw=aw)


def states_to_pairs(s):
    bsz, h = s.shape[:2]
    st = jnp.swapaxes(s, -1, -2).reshape(bsz, h // 2, 2, A_HEAD, A_HEAD)
    z = jnp.zeros_like(st[:, :, 0])
    top = jnp.concatenate([st[:, :, 0], z], axis=-1)
    bot = jnp.concatenate([z, st[:, :, 1]], axis=-1)
    return jnp.concatenate([top, bot], axis=-2)


def pairs_to_states(sp):
    bsz = sp.shape[0]
    st = jnp.stack([sp[:, :, :A_HEAD, :A_HEAD], sp[:, :, A_HEAD:, A_HEAD:]], axis=2)
    return jnp.swapaxes(st.reshape(bsz, -1, A_HEAD, A_HEAD), -1, -2)


def rwkv_branch(u, bsz, t, rwp, s0f, s0b):
    aw, ap = rwp["aw"], rwp["ap"]
    n = bsz * t
    pack_f, pack_b, bonus, g = rwkv_prep(u, t, rwp, aw, ap)
    sh = lambda a: a.reshape(bsz, t, SCAN_FIELDS * aw)
    yf, sf = rwkv_scan(sh(pack_f), states_to_pairs(s0f), False)
    yb, sb = rwkv_scan(sh(pack_b), states_to_pairs(s0b), True)
    ya = rwkv_post(yf.reshape(n, aw), yb.reshape(n, aw), bonus, g, rwp["ln_w"], rwp["ln_b"])
    return ya, pairs_to_states(sf), pairs_to_states(sb)


def _mod_rows(m):
    return [m[:, i][:, None, :] for i in range(6)]


def kernel(x_prompt, x_sample, cache_win_k, cache_win_v, cache_nb_k, cache_nb_v, state_rwkv_fwd, state_rwkv_bwd, c, c_ctx, ada_w, ada_b, w_in, rwkv_mu, rwkv_w0, rwkv_w2, rwkv_a0, rwkv_a2, rwkv_g2, rwkv_kk, rwkv_ka, rwkv_rk, rwkv_lnx_w, rwkv_lnx_b, win_sink, nb_rpb, w_br_a, w_br_b, w_br_c, w_out, ln1_g, ln1_b, ln2_g, ln2_b, router_w, exp_w1, exp_w3, exp_w2):
    depth, d, in_w = w_in.shape
    bsz, seq, _ = x_prompt.shape
    dbsz, dseq, _ = x_sample.shape
    aw = rwkv_w0.shape[-1]
    a_heads = aw // A_HEAD
    a_cols = rwkv_mu.shape[-1]
    ap = round_up(a_cols, LANE)
    n_b = win_sink.shape[-1]
    n_kv = cache_win_k.shape[3]
    group = n_b // n_kv
    n_c = nb_rpb.shape[1]
    bw, cw = n_b * HEAD_DIM, n_c * HEAD_DIM
    alpha = float((2 * depth) ** 0.25)
    assert in_w == a_cols + bw + 2 * n_kv * HEAD_DIM + 3 * cw + 3 * d

    qb0 = ap // LANE
    kb0 = qb0 + n_b
    vb0 = kb0 + n_kv
    qn0 = vb0 + n_kv
    kn0 = qn0 + n_c
    vn0 = kn0 + n_c
    gate_col0 = (vn0 + n_c) * LANE

    rows = 1 + dbsz
    cvecs = jnp.pad(jnp.concatenate([c_ctx[None], c], axis=0), ((0, round_up(rows, 16) - rows), (0, 0)))
    mods = ada_mods(cvecs, ada_w, ada_b).reshape(depth, -1, 6, d)

    xp, xs = x_prompt, x_sample
    np_tok, ns_tok = bsz * seq, dbsz * dseq
    win_k, win_v, nb_k, nb_v, st_f, st_b = [], [], [], [], [], []
    zero_state = jnp.zeros((bsz, a_heads, A_HEAD, A_HEAD), F32)
    w_cat = jnp.concatenate(
        [w_in[:, :, :a_cols].astype(BF16), jnp.zeros((depth, d, ap - a_cols), BF16),
         w_in[:, :, a_cols:].astype(BF16)], axis=2)
    w1, w3, w2 = exp_w1.astype(BF16), exp_w3.astype(BF16), exp_w2.astype(BF16)
    for l in range(depth):
        rwp = prep_rwkv_params(
            {"mu": rwkv_mu[l], "w0": rwkv_w0[l], "w2": rwkv_w2[l], "a0": rwkv_a0[l], "a2": rwkv_a2[l],
             "g2": rwkv_g2[l], "k_k": rwkv_kk[l], "k_a": rwkv_ka[l], "r_k": rwkv_rk[l],
             "ln_w": rwkv_lnx_w[l], "ln_b": rwkv_lnx_b[l]}, aw)
        wa, wb, wc = w_br_a[l].astype(BF16), w_br_b[l].astype(BF16), w_br_c[l].astype(BF16)
        wo = w_out[l].astype(BF16)
        g1, b1 = ln1_g[l].reshape(1, d), ln1_b[l].reshape(1, d)
        g2, b2 = ln2_g[l].reshape(1, d), ln2_b[l].reshape(1, d)

        sh1, sc1, gt1, sh2, sc2, gt2 = _mod_rows(mods[l, 0:1])
        h = ln_modulate(xp, sh1, sc1).reshape(np_tok, d)
        u = matmul(h, w_cat, l, name="in_proj")
        ya, s_f, s_b = rwkv_branch(u, bsz, seq, rwp, zero_state, zero_state)
        yb = ctx_attention(u, bsz, seq, qb0, kb0, vb0, n_b, group, win_sink[l])
        yc = ctx_attention(u, bsz, seq, qn0, kn0, vn0, n_c, 1, None)
        m = merge_branches(ya, yb, yc, wa, wb, wc, u, gate_col0)
        x1 = outproj_postnorm(m, wo, xp.reshape(np_tok, d), gt1, g1, b1, alpha, seq)
        xp = ec_moe_postnorm(x1.reshape(bsz, seq, d), sh2, sc2, gt2, router_w[l], w1, w3, w2, l, g2, b2, alpha)
        cols = lambda c0, nh: u[:, c0 * LANE:(c0 + nh) * LANE].reshape(bsz, seq, nh, HEAD_DIM)
        win_k.append(cols(kb0, n_kv))
        win_v.append(cols(vb0, n_kv))
        nb_k.append(cols(kn0, n_c))
        nb_v.append(cols(vn0, n_c))
        st_f.append(s_f)
        st_b.append(s_b)

        sh1, sc1, gt1, sh2, sc2, gt2 = _mod_rows(mods[l, 1:1 + dbsz])
        h = ln_modulate(xs, sh1, sc1).reshape(ns_tok, d)
        u = matmul(h, w_cat, l, name="in_proj")
        ya, _, _ = rwkv_branch(u, dbsz, dseq, rwp, state_rwkv_fwd[:, l], state_rwkv_bwd[:, l])
        qk_rot = rope(u, dbsz, dseq, qb0, n_b + n_kv)
        heads_first = lambda a: jnp.swapaxes(a[:, l], 1, 2)
        yb = window_attention(qk_rot, u, vb0, dbsz, dseq, n_kv, group,
                              heads_first(cache_win_k), heads_first(cache_win_v), win_sink[l])
        yc = neighbourhood_attention(u, (qn0, kn0, vn0), dbsz, dseq, n_c,
                                     heads_first(cache_nb_k), heads_first(cache_nb_v), nb_rpb[l])
        m = merge_branches(ya, yb, yc, wa, wb, wc, u, gate_col0)
        x1 = outproj_postnorm(m, wo, xs.reshape(ns_tok, d), gt1, g1, b1, alpha, dseq)
        xs = ec_moe_postnorm(x1.reshape(dbsz, dseq, d), sh2, sc2, gt2, router_w[l], w1, w3, w2, l, g2, b2, alpha)

    stack = lambda xs_: jnp.stack(xs_, axis=1)
    return (xp, xs, stack(win_k), stack(win_v), stack(nb_k), stack(nb_v), stack(st_f), stack(st_b))
```
